```python
import math
import jax, jax.numpy as jnp
from jax import lax
import numpy as np

D_MODEL = 1024
BATCH = 8
SEQ = 2048
DEPTH = 2
DEC_BATCH = 128
DEC_SEQ = 8
PAST_LEN = 8192
PAGE_SIZE = 128

N_A = DEPTH // 2
N_B = DEPTH - N_A
CONV_K = 31
FFN_CONV_K = 3
D_FF = 2816
N_HEADS = 16
N_KV_HEADS = 4
HEAD_DIM = 64
GROUP = N_HEADS // N_KV_HEADS
WINDOW = 128
N_BUCKETS = 32
MAX_DISTANCE = 128
PLE_DIM = 256
EPS = 1e-6
SCALE = HEAD_DIM ** -0.5
NEG = -1e30

kernel_name = "yoco_conformer_swa_sink_decoder_step"


def _rmsnorm(x, g):
    xf = x.astype(jnp.float32)
    y = xf * lax.rsqrt(jnp.mean(xf * xf, axis=-1, keepdims=True) + EPS)
    return (y * g.astype(jnp.float32)).astype(x.dtype)


def _layernorm(x, g, b):
    xf = x.astype(jnp.float32)
    mu = jnp.mean(xf, axis=-1, keepdims=True)
    var = jnp.mean(jnp.square(xf - mu), axis=-1, keepdims=True)
    y = (xf - mu) * lax.rsqrt(var + EPS)
    return (y * g.astype(jnp.float32) + b.astype(jnp.float32)).astype(x.dtype)


def _dwconv_valid(xh, w):
    return lax.conv_general_dilated(xh, w[:, None, :].astype(xh.dtype), window_strides=(1,), padding='VALID',
                                    dimension_numbers=('NWC', 'WIO', 'NWC'),
                                    feature_group_count=xh.shape[-1])


def _conv_module(xn, hist, w_pw1, b_pw1, w_dw, b_dw, ln_g, ln_b, w_pw2, b_pw2):
    u = xn @ w_pw1 + b_pw1
    a, g = jnp.split(u, 2, axis=-1)
    u = a * jax.nn.sigmoid(g)
    if hist is None:
        hist = jnp.zeros((u.shape[0], CONV_K - 1, u.shape[-1]), u.dtype)
    uh = jnp.concatenate([hist.astype(u.dtype), u], axis=1)
    c = _dwconv_valid(uh, w_dw) + b_dw
    c = jax.nn.silu(_layernorm(c, ln_g, ln_b))
    return c @ w_pw2 + b_pw2, uh[:, -(CONV_K - 1):]


def _conv_ffn(xn, hist, w_up, w_dw, b_dw, w_down):
    u = xn @ w_up
    if hist is None:
        hist = jnp.zeros((u.shape[0], FFN_CONV_K - 1, u.shape[-1]), u.dtype)
    uh = jnp.concatenate([hist.astype(u.dtype), u], axis=1)
    c = _dwconv_valid(uh, w_dw) + b_dw
    gate, val = jnp.split(c, 2, axis=-1)
    return (jax.nn.silu(gate) * val) @ w_down, uh[:, -(FFN_CONV_K - 1):]


def _ple(h, p, g, w_gate, w_proj):
    gate = jax.nn.sigmoid(_rmsnorm(h, g) @ w_gate)
    return h + gate * (p @ w_proj)


def _t5_bucket(dist):
    max_exact = N_BUCKETS // 2
    d = jnp.maximum(dist, 0)
    df = jnp.maximum(d, 1).astype(jnp.float32)
    large = max_exact + (jnp.log(df / max_exact) / math.log(MAX_DISTANCE / max_exact)
                         * (N_BUCKETS - max_exact)).astype(jnp.int32)
    large = jnp.minimum(large, N_BUCKETS - 1)
    return jnp.where(d < max_exact, d, large)


def _rel_bias(dist, table):
    vals = table[_t5_bucket(dist)].astype(jnp.float32)
    return jnp.moveaxis(vals, -1, 0).reshape(N_KV_HEADS, GROUP, *dist.shape)


def _sink_softmax(logits, sinks):
    s = sinks.astype(jnp.float32).reshape(N_KV_HEADS, GROUP)[:, :, None, None]
    m = jnp.maximum(jnp.max(logits, axis=-1, keepdims=True), s)
    e = jnp.exp(logits - m)
    return e / (jnp.sum(e, axis=-1, keepdims=True) + jnp.exp(s - m))


def _swa_prompt(q, k, v, table, sinks):
    Bq, T = q.shape[0], q.shape[1]
    W = WINDOW
    nb = T // W
    qb = q.reshape(Bq, nb, W, N_KV_HEADS, GROUP, HEAD_DIM)
    pad = jnp.zeros((Bq, W, N_KV_HEADS, HEAD_DIM), k.dtype)
    kb = jnp.concatenate([pad, k], axis=1).reshape(Bq, nb + 1, W, N_KV_HEADS, HEAD_DIM)
    vb = jnp.concatenate([pad.astype(v.dtype), v], axis=1).reshape(Bq, nb + 1, W, N_KV_HEADS, HEAD_DIM)
    k_band = jnp.concatenate([kb[:, :-1], kb[:, 1:]], axis=2)
    v_band = jnp.concatenate([vb[:, :-1], vb[:, 1:]], axis=2)
    scores = jnp.einsum('bnqkgd,bnskd->bnkgqs', qb, k_band,
                        preferred_element_type=jnp.float32) * SCALE
    qi = jnp.arange(W)[:, None]
    sj = jnp.arange(2 * W)[None, :]
    dist = qi + W - sj
    kpos = jnp.arange(nb)[:, None, None] * W + sj[None] - W
    valid = (dist >= 0) & (dist < WINDOW) & (kpos >= 0)
    logits = jnp.where(valid[None, :, None, None], scores + _rel_bias(dist, table), NEG)
    probs = _sink_softmax(logits, sinks)
    out = jnp.einsum('bnkgqs,bnskd->bnqkgd', probs.astype(v.dtype), v_band)
    return out.reshape(Bq, T, N_HEADS * HEAD_DIM)


def _swa_sample(q, k_all, v_all, table, sinks):
    Bq, S = q.shape[0], q.shape[1]
    L = k_all.shape[1]
    WB = L - S
    qg = q.reshape(Bq, S, N_KV_HEADS, GROUP, HEAD_DIM)
    scores = jnp.einsum('bqkgd,bskd->bkgqs', qg, k_all,
                        preferred_element_type=jnp.float32) * SCALE
    qpos = PAST_LEN + jnp.arange(S)
    kpos = PAST_LEN - WB + jnp.arange(L)
    dist = qpos[:, None] - kpos[None, :]
    valid = (dist >= 0) & (dist < WINDOW)
    logits = jnp.where(valid, scores + _rel_bias(dist, table), NEG)
    probs = _sink_softmax(logits, sinks)
    out = jnp.einsum('bkgqs,bskd->bqkgd', probs.astype(v_all.dtype), v_all)
    return out.reshape(Bq, S, N_HEADS * HEAD_DIM)


def _trunk(x, p, conv_state, ffn_state, kc, vc, prm):
    prompt = conv_state is None
    Bx, T, _ = x.shape
    h = x
    new_conv, new_ffn = [], []
    k_use = v_use = new_k = new_v = None
    for i in range(DEPTH):
        hn = _rmsnorm(h, prm['g_mix'][i])
        if i < N_A:
            o, ch = _conv_module(hn, None if prompt else conv_state[i],
                                 prm['cm_w_pw1'][i], prm['cm_b_pw1'][i], prm['cm_w_dw'][i],
                                 prm['cm_b_dw'][i], prm['cm_ln_g'][i], prm['cm_ln_b'][i],
                                 prm['cm_w_pw2'][i], prm['cm_b_pw2'][i])
            new_conv.append(ch)
        else:
            b = i - N_A
            q = (hn @ prm['at_w_q'][b]).reshape(Bx, T, N_HEADS, HEAD_DIM)
            q = _rmsnorm(q, prm['at_g_q'][b])
            if prompt:
                o = _swa_prompt(q, k_use, v_use, prm['rel_bias'], prm['at_sinks'][b])
            else:
                o = _swa_sample(q, k_use, v_use, prm['rel_bias'], prm['at_sinks'][b])
            o = o @ prm['at_w_o'][b]
        h = h + o
        f, fh = _conv_ffn(_rmsnorm(h, prm['g_ffn'][i]), None if prompt else ffn_state[i],
                          prm['ffn_w_up'][i], prm['ffn_w_dw'][i], prm['ffn_b_dw'][i],
                          prm['ffn_w_down'][i])
        new_ffn.append(fh)
        h = h + f
        h = _ple(h, p[i], prm['g_ple'][i], prm['ple_w_gate'][i], prm['ple_w_proj'][i])
        if i == N_A - 1:
            hk = _rmsnorm(h, prm['kv_g'])
            k_sh = _rmsnorm((hk @ prm['kv_w_k']).reshape(Bx, T, N_KV_HEADS, HEAD_DIM), prm['kv_g_k'])
            v_sh = (hk @ prm['kv_w_v']).reshape(Bx, T, N_KV_HEADS, HEAD_DIM)
            if prompt:
                k_use, v_use = k_sh, v_sh
                keep = min(WINDOW, T)
            else:
                k_use = jnp.concatenate([kc.astype(k_sh.dtype), k_sh], axis=1)
                v_use = jnp.concatenate([vc.astype(v_sh.dtype), v_sh], axis=1)
                keep = kc.shape[1]
            new_k = k_use[:, -keep:]
            new_v = v_use[:, -keep:]
    return h, jnp.stack(new_conv), jnp.stack(new_ffn), new_k, new_v


def setup_inputs(seed: int = 0) -> dict:
    key = jax.random.key(seed)
    ks = iter(jax.random.split(key, 48))

    def nrm(shape, scale):
        return jax.random.normal(next(ks), shape, jnp.float32) * scale

    def gain(shape):
        return 1.0 + nrm(shape, 0.05)

    D, F = D_MODEL, D_FF
    QD = N_HEADS * HEAD_DIM
    KD = N_KV_HEADS * HEAD_DIM
    WB = min(WINDOW, PAST_LEN)
    return {
        'x_prompt': nrm((BATCH, SEQ, D), 1.0),
        'x_sample': nrm((DEC_BATCH, DEC_SEQ, D), 1.0),
        'state_conv': nrm((N_A, DEC_BATCH, CONV_K - 1, D), 0.5),
        'state_ffn': nrm((DEPTH, DEC_BATCH, FFN_CONV_K - 1, 2 * F), 0.5),
        'cache_k': nrm((DEC_BATCH, WB, N_KV_HEADS, HEAD_DIM), 1.0),
        'cache_v': nrm((DEC_BATCH, WB, N_KV_HEADS, HEAD_DIM), 1.0),
        'p_prompt': nrm((DEPTH, BATCH, SEQ, PLE_DIM), 1.0),
        'p_sample': nrm((DEPTH, DEC_BATCH, DEC_SEQ, PLE_DIM), 1.0),
        'g_mix': gain((DEPTH, D)),
        'cm_w_pw1': nrm((N_A, D, 2 * D), D ** -0.5),
        'cm_b_pw1': nrm((N_A, 2 * D), 0.02),
        'cm_w_dw': nrm((N_A, CONV_K, D), CONV_K ** -0.5),
        'cm_b_dw': nrm((N_A, D), 0.02),
        'cm_ln_g': gain((N_A, D)),
        'cm_ln_b': nrm((N_A, D), 0.02),
        'cm_w_pw2': nrm((N_A, D, D), 0.5 * D ** -0.5),
        'cm_b_pw2': nrm((N_A, D), 0.02),
        'at_w_q': nrm((N_B, D, QD), D ** -0.5),
        'at_g_q': gain((N_B, HEAD_DIM)),
        'at_sinks': nrm((N_B, N_HEADS), 1.0),
        'at_w_o': nrm((N_B, QD, D), 0.5 * QD ** -0.5),
        'kv_g': gain((D,)),
        'kv_w_k': nrm((D, KD), D ** -0.5),
        'kv_w_v': nrm((D, KD), D ** -0.5),
        'kv_g_k': gain((HEAD_DIM,)),
        'rel_bias': nrm((N_BUCKETS, N_HEADS), 0.5),
        'g_ffn': gain((DEPTH, D)),
        'ffn_w_up': nrm((DEPTH, D, 2 * F), D ** -0.5),
        'ffn_w_dw': nrm((DEPTH, FFN_CONV_K, 2 * F), FFN_CONV_K ** -0.5),
        'ffn_b_dw': nrm((DEPTH, 2 * F), 0.02),
        'ffn_w_down': nrm((DEPTH, F, D), 0.5 * F ** -0.5),
        'g_ple': gain((DEPTH, D)),
        'ple_w_gate': nrm((DEPTH, D, D), D ** -0.5),
        'ple_w_proj': nrm((DEPTH, PLE_DIM, D), 0.5 * PLE_DIM ** -0.5),
    }


def reference(x_prompt, x_sample, state_conv, state_ffn, cache_k, cache_v, p_prompt, p_sample,
              g_mix, cm_w_pw1, cm_b_pw1, cm_w_dw, cm_b_dw, cm_ln_g, cm_ln_b, cm_w_pw2, cm_b_pw2,
              at_w_q, at_g_q, at_sinks, at_w_o, kv_g, kv_w_k, kv_w_v, kv_g_k, rel_bias,
              g_ffn, ffn_w_up, ffn_w_dw, ffn_b_dw, ffn_w_down, g_ple, ple_w_gate, ple_w_proj):
    prm = {
        'g_mix': g_mix, 'cm_w_pw1': cm_w_pw1, 'cm_b_pw1': cm_b_pw1, 'cm_w_dw': cm_w_dw,
        'cm_b_dw': cm_b_dw, 'cm_ln_g': cm_ln_g, 'cm_ln_b': cm_ln_b, 'cm_w_pw2': cm_w_pw2,
        'cm_b_pw2': cm_b_pw2, 'at_w_q': at_w_q, 'at_g_q': at_g_q, 'at_sinks': at_sinks,
        'at_w_o': at_w_o, 'kv_g': kv_g, 'kv_w_k': kv_w_k, 'kv_w_v': kv_w_v, 'kv_g_k': kv_g_k,
        'rel_bias': rel_bias, 'g_ffn': g_ffn, 'ffn_w_up': ffn_w_up, 'ffn_w_dw': ffn_w_dw,
        'ffn_b_dw': ffn_b_dw, 'ffn_w_down': ffn_w_down, 'g_ple': g_ple,
        'ple_w_gate': ple_w_gate, 'ple_w_proj': ple_w_proj,
    }
    y_prompt, conv_p, ffn_p, k_p, v_p = _trunk(x_prompt, p_prompt, None, None, None, None, prm)
    y_sample, conv_s, ffn_s, k_s, v_s = _trunk(x_sample, p_sample, state_conv, state_ffn,
                                               cache_k, cache_v, prm)
    return (y_prompt, y_sample, conv_p, conv_s, ffn_p, ffn_s, k_p, k_s, v_p, v_s)
```

```python
import functools
import math

import jax
import jax.numpy as jnp
import numpy as np
from jax import lax
from jax.experimental import pallas as pl
from jax.experimental.pallas import tpu as pltpu

D_MODEL = 1024
BATCH = 8
SEQ = 2048
DEC_BATCH = 128
DEC_SEQ = 8
CONV_K = 31
FFN_CONV_K = 3
D_FF = 2816
N_HEADS = 16
N_KV_HEADS = 4
HEAD_DIM = 64
GROUP = N_HEADS // N_KV_HEADS
WINDOW = 128
N_BUCKETS = 32
MAX_DISTANCE = 128
PLE_DIM = 256
EPS = 1e-6
SCALE = HEAD_DIM ** -0.5
NEG = -1e30
KD = N_KV_HEADS * HEAD_DIM

LANES = 128
SUBLANES = 8
VMEM_LIMIT = 56 * 1024 * 1024

T_TILE = 256
S_TILE = 32
CONV_HALO = 32
FFN_HALO = 8
F_CHUNK = 256
D_SLABS = D_MODEL // LANES
F_SLABS = D_FF // LANES
ATT_SEQS = 16

_BF = jnp.bfloat16
_F32 = jnp.float32


def _bdot(a, w):
    return jnp.dot(a.astype(_BF), w, preferred_element_type=_F32)


def _rms(x, g):
    return x * lax.rsqrt(jnp.mean(x * x, axis=-1, keepdims=True) + EPS) * g


def _sigmoid(x):
    return 1.0 / (1.0 + jnp.exp(-x))


def _const_spec(shape):
    nd = len(shape)
    return pl.BlockSpec(shape, lambda *_: (0,) * nd, pipeline_mode=pl.Buffered(1))


def _params(n_grid):
    return pltpu.CompilerParams(dimension_semantics=("arbitrary",) * n_grid,
                                vmem_limit_bytes=VMEM_LIMIT)


def _mixer_tail(x, c, bdw, lng, lnb, w2, b2):
    c = c + bdw
    mu = jnp.mean(c, axis=-1, keepdims=True)
    cz = c - mu
    var = jnp.mean(cz * cz, axis=-1, keepdims=True)
    y = cz * lax.rsqrt(var + EPS) * lng + lnb
    s = y * _sigmoid(y)
    return x + _bdot(s, w2) + b2


def _mixer_prompt_kernel(x_ref, g_ref, w1_ref, b1_ref, wdw_ref, bdw_ref, lng_ref, lnb_ref,
                         w2_ref, b2_ref, h_ref, cs_ref, slab_ref, oslab_ref):
    T = T_TILE
    H = T // 2
    j = pl.program_id(1)

    @pl.when(j == 0)
    def _():
        slab_ref[:, 0:CONV_HALO, :] = jnp.zeros((D_SLABS, CONV_HALO, LANES), _F32)

    x = x_ref[...]
    u = _bdot(_rms(x, g_ref[...]), w1_ref[...]) + b1_ref[...]
    glu = u[:, :D_MODEL] * _sigmoid(u[:, D_MODEL:])
    for c in range(D_SLABS):
        slab_ref[c, CONV_HALO:CONV_HALO + T, :] = glu[:, c * LANES:(c + 1) * LANES]
    cs_ref[...] = glu[T - (CONV_K - 1):, :]

    base = CONV_HALO - (CONV_K - 1)
    for c in range(D_SLABS):
        wc = wdw_ref[:, c * LANES:(c + 1) * LANES]
        for p in range(2):
            acc = slab_ref[c, pl.ds(base + p, H, stride=2), :] * wc[0:1, :]
            for k in range(1, CONV_K):
                acc = acc + slab_ref[c, pl.ds(base + k + p, H, stride=2), :] * wc[k:k + 1, :]
            oslab_ref[c, pl.ds(p, H, stride=2), :] = acc
    conv = jnp.concatenate([oslab_ref[c] for c in range(D_SLABS)], axis=1)
    h_ref[...] = _mixer_tail(x, conv, bdw_ref[...], lng_ref[...], lnb_ref[...],
                             w2_ref[...], b2_ref[...])
    for c in range(D_SLABS):
        slab_ref[c, 0:CONV_HALO, :] = slab_ref[c, T:T + CONV_HALO, :]


def _mixer_sample_kernel(x_ref, st_ref, g_ref, w1_ref, b1_ref, wdw_ref, bdw_ref, lng_ref,
                         lnb_ref, w2_ref, b2_ref, h_ref, cs_ref, win_ref):
    S = S_TILE
    KH = CONV_K - 1
    x = x_ref[...]
    u = _bdot(_rms(x, g_ref[...]), w1_ref[...]) + b1_ref[...]
    glu = u[:, :D_MODEL] * _sigmoid(u[:, D_MODEL:])
    win_ref[:, 0:KH, :] = st_ref[...]
    win_ref[:, KH:KH + DEC_SEQ, :] = glu.reshape(S, DEC_SEQ, D_MODEL)
    cs_ref[...] = win_ref[:, DEC_SEQ:DEC_SEQ + KH, :]
    acc = win_ref[:, 0:DEC_SEQ, :] * wdw_ref[0:1, :]
    for k in range(1, CONV_K):
        acc = acc + win_ref[:, k:k + DEC_SEQ, :] * wdw_ref[k:k + 1, :]
    conv = acc.reshape(S * DEC_SEQ, D_MODEL)
    h_ref[...] = _mixer_tail(x, conv, bdw_ref[...], lng_ref[...], lnb_ref[...],
                             w2_ref[...], b2_ref[...])


def _mixer_weight_specs():
    D = D_MODEL
    return [_const_spec((1, D)), _const_spec((D, 2 * D)), _const_spec((1, 2 * D)),
            _const_spec((CONV_K, D)), _const_spec((1, D)), _const_spec((1, D)),
            _const_spec((1, D)), _const_spec((D, D)), _const_spec((1, D))]


def _mixer_prompt(x2d, wts):
    T = T_TILE
    nt = SEQ // T
    row = lambda b, j: (b * nt + j, 0)
    return pl.pallas_call(
        _mixer_prompt_kernel,
        grid=(BATCH, nt),
        in_specs=[pl.BlockSpec((T, D_MODEL), row)] + _mixer_weight_specs(),
        out_specs=[pl.BlockSpec((T, D_MODEL), row),
                   pl.BlockSpec((None, CONV_K - 1, D_MODEL), lambda b, j: (b, 0, 0))],
        out_shape=[jax.ShapeDtypeStruct((BATCH * SEQ, D_MODEL), _F32),
                   jax.ShapeDtypeStruct((BATCH, CONV_K - 1, D_MODEL), _F32)],
        scratch_shapes=[pltpu.VMEM((D_SLABS, CONV_HALO + T, LANES), _F32),
                        pltpu.VMEM((D_SLABS, T, LANES), _F32)],
        compiler_params=_params(2),
        name="mixer_prompt",
    )(x2d, *wts)


def _mixer_sample(x2d, state4d, wts):
    S = S_TILE
    R = S * DEC_SEQ
    return pl.pallas_call(
        _mixer_sample_kernel,
        grid=(DEC_BATCH // S,),
        in_specs=[pl.BlockSpec((R, D_MODEL), lambda i: (i, 0)),
                  pl.BlockSpec((None, S, CONV_K - 1, D_MODEL), lambda i: (0, i, 0, 0))]
                 + _mixer_weight_specs(),
        out_specs=[pl.BlockSpec((R, D_MODEL), lambda i: (i, 0)),
                   pl.BlockSpec((S, CONV_K - 1, D_MODEL), lambda i: (i, 0, 0))],
        out_shape=[jax.ShapeDtypeStruct((DEC_BATCH * DEC_SEQ, D_MODEL), _F32),
                   jax.ShapeDtypeStruct((DEC_BATCH, CONV_K - 1, D_MODEL), _F32)],
        scratch_shapes=[pltpu.VMEM((S, CONV_K - 1 + DEC_SEQ + 2, D_MODEL), _F32)],
        compiler_params=_params(1),
        name="mixer_sample",
    )(x2d, state4d, *wts)


def _ffn_kernel(*refs, prompt, with_oproj, with_kvq):
    it = iter(refs)
    h_ref = next(it)
    p_ref = next(it)
    st_ref = None if prompt else next(it)
    if with_oproj:
        o_ref = next(it)
        wo_ref = next(it)
    gf_ref, wup_ref, wdw_ref, bdw_ref, wdn_ref = (next(it) for _ in range(5))
    gp_ref, wg_ref, wp_ref = (next(it) for _ in range(3))
    if with_kvq:
        gkv_ref, wk_ref, wv_ref, gk_ref, gq_ref, wq_ref = (next(it) for _ in range(6))
    hout_ref = next(it)
    fs_ref = next(it)
    if with_kvq:
        k_ref, v_ref, q_ref = (next(it) for _ in range(3))
    if prompt:
        slab_ref, oslab_ref = next(it), next(it)
    else:
        win_ref = next(it)

    F = D_FF
    R = T_TILE if prompt else S_TILE * DEC_SEQ
    H = R // 2
    KH = FFN_CONV_K - 1

    if prompt:
        @pl.when(pl.program_id(1) == 0)
        def _():
            slab_ref[:, 0:FFN_HALO, :] = jnp.zeros((2 * F_SLABS, FFN_HALO, LANES), _F32)
    else:
        win_ref[:, FFN_HALO - KH:FFN_HALO, :] = st_ref[...]

    h = h_ref[...]
    if with_oproj:
        h = h + _bdot(o_ref[...], wo_ref[...])
    xn = _rms(h, gf_ref[...]).astype(_BF)

    n_chunks = F // F_CHUNK
    spc = F_CHUNK // LANES
    acc = None
    for ci in range(n_chunks):
        halves = []
        for half in range(2):
            c0 = half * F + ci * F_CHUNK
            uc = jnp.dot(xn, wup_ref[:, c0:c0 + F_CHUNK], preferred_element_type=_F32)
            w = wdw_ref[:, c0:c0 + F_CHUNK]
            b = bdw_ref[:, c0:c0 + F_CHUNK]
            if prompt:
                cols = []
                for s in range(spc):
                    sl = half * F_SLABS + ci * spc + s
                    slab_ref[sl, FFN_HALO:FFN_HALO + R, :] = uc[:, s * LANES:(s + 1) * LANES]
                    ws = w[:, s * LANES:(s + 1) * LANES]
                    ph = []
                    for p in range(2):
                        cv = b[:, s * LANES:(s + 1) * LANES]
                        for k in range(FFN_CONV_K):
                            cv = cv + (slab_ref[sl, pl.ds(FFN_HALO - KH + k + p, H, stride=2), :]
                                       * ws[k:k + 1, :])
                        ph.append(cv)
                    cols.append(jnp.concatenate(ph, axis=0))
                halves.append(jnp.concatenate(cols, axis=1))
            else:
                win_ref[:, FFN_HALO:FFN_HALO + DEC_SEQ, c0:c0 + F_CHUNK] = uc.reshape(
                    S_TILE, DEC_SEQ, F_CHUNK)
                cv = b
                for k in range(FFN_CONV_K):
                    o = FFN_HALO - KH + k
                    cv = cv + win_ref[:, o:o + DEC_SEQ, c0:c0 + F_CHUNK] * w[k:k + 1, :]
                halves.append(cv.reshape(R, F_CHUNK))
        gate, val = halves
        act = (gate * _sigmoid(gate) * val).astype(_BF)
        part = jnp.dot(act, wdn_ref[ci * F_CHUNK:(ci + 1) * F_CHUNK, :],
                       preferred_element_type=_F32)
        acc = part if acc is None else acc + part

    if prompt:
        for c in range(D_SLABS):
            oslab_ref[c, pl.ds(0, H, stride=2), :] = acc[:H, c * LANES:(c + 1) * LANES]
            oslab_ref[c, pl.ds(1, H, stride=2), :] = acc[H:, c * LANES:(c + 1) * LANES]
        f = jnp.concatenate([oslab_ref[c] for c in range(D_SLABS)], axis=1)
        fs_ref[...] = jnp.concatenate(
            [slab_ref[s, FFN_HALO + R - SUBLANES:FFN_HALO + R, :] for s in range(2 * F_SLABS)],
            axis=1)[SUBLANES - KH:, :]
        for s in range(2 * F_SLABS):
            slab_ref[s, 0:FFN_HALO, :] = slab_ref[s, R:R + FFN_HALO, :]
    else:
        f = acc
        fs_ref[...] = win_ref[:, FFN_HALO + DEC_SEQ - KH:FFN_HALO + DEC_SEQ, :]

    h2 = h + f
    gate = _sigmoid(_bdot(_rms(h2, gp_ref[...]), wg_ref[...]))
    h3 = h2 + gate * _bdot(p_ref[...], wp_ref[...])
    hout_ref[...] = h3

    if with_kvq:
        hk = _rms(h3, gkv_ref[...]).astype(_BF)
        kk = jnp.dot(hk, wk_ref[...], preferred_element_type=_F32)
        v_ref[...] = jnp.dot(hk, wv_ref[...], preferred_element_type=_F32)
        lane = lax.broadcasted_iota(jnp.int32, (R, LANES), 1)
        low = lane < HEAD_DIM
        kn = []
        for c in range(KD // LANES):
            kc = kk[:, c * LANES:(c + 1) * LANES]
            sq = kc * kc
            ss_lo = jnp.sum(jnp.where(low, sq, 0.0), axis=-1, keepdims=True)
            ss_hi = jnp.sum(jnp.where(low, 0.0, sq), axis=-1, keepdims=True)
            r = lax.rsqrt(jnp.where(low, ss_lo, ss_hi) * (1.0 / HEAD_DIM) + EPS)
            kn.append(kc * r * gk_ref[...])
        k_ref[...] = jnp.concatenate(kn, axis=1)
        q_ref[...] = _bdot(_rms(h3, gq_ref[...]), wq_ref[...])


def _ffn_call(h2d, p3d, state4d, o2d, w, *, layer, prompt, with_oproj, with_kvq):
    D, F = D_MODEL, D_FF
    if prompt:
        R = T_TILE
        nt = SEQ // R
        grid = (BATCH, nt)
        row = lambda b, j: (b * nt + j, 0)
        p_spec = pl.BlockSpec((None, R, PLE_DIM), lambda b, j: (layer, b * nt + j, 0))
        n_rows, n_seq = BATCH * SEQ, BATCH
        fs_spec = pl.BlockSpec((None, FFN_CONV_K - 1, 2 * F), lambda b, j: (b, 0, 0))
        scratch = [pltpu.VMEM((2 * F_SLABS, FFN_HALO + R, LANES), _F32),
                   pltpu.VMEM((D_SLABS, R, LANES), _F32)]
    else:
        R = S_TILE * DEC_SEQ
        grid = (DEC_BATCH // S_TILE,)
        row = lambda i: (i, 0)
        p_spec = pl.BlockSpec((None, R, PLE_DIM), lambda i: (layer, i, 0))
        n_rows, n_seq = DEC_BATCH * DEC_SEQ, DEC_BATCH
        fs_spec = pl.BlockSpec((S_TILE, FFN_CONV_K - 1, 2 * F), lambda i: (i, 0, 0))
        scratch = [pltpu.VMEM((S_TILE, FFN_HALO + DEC_SEQ, 2 * F), _F32)]

    args = [h2d, p3d]
    in_specs = [pl.BlockSpec((R, D), row), p_spec]
    if not prompt:
        args.append(state4d)
        in_specs.append(pl.BlockSpec((None, S_TILE, FFN_CONV_K - 1, 2 * F),
                                     lambda i: (layer, i, 0, 0)))
    if with_oproj:
        args += [o2d, w["wo"]]
        in_specs += [pl.BlockSpec((R, D), row), _const_spec((D, D))]
    args += [w["g_ffn"], w["w_up"], w["w_dw"], w["b_dw"], w["w_down"],
             w["g_ple"], w["w_gate"], w["w_proj"]]
    in_specs += [_const_spec((1, D)), _const_spec((D, 2 * F)), _const_spec((FFN_CONV_K, 2 * F)),
                 _const_spec((1, 2 * F)), _const_spec((F, D)),
                 _const_spec((1, D)), _const_spec((D, D)), _const_spec((PLE_DIM, D))]
    out_specs = [pl.BlockSpec((R, D), row), fs_spec]
    out_shape = [jax.ShapeDtypeStruct((n_rows, D), _F32),
                 jax.ShapeDtypeStruct((n_seq, FFN_CONV_K - 1, 2 * F), _F32)]
    if with_kvq:
        args += [w["g_kv"], w["w_k"], w["w_v"], w["g_k2"], w["g_q_in"], w["w_q"]]
        in_specs += [_const_spec((1, D)), _const_spec((D, KD)), _const_spec((D, KD)),
                     _const_spec((1, LANES)), _const_spec((1, D)), _const_spec((D, D))]
        out_specs += [pl.BlockSpec((R, KD), row), pl.BlockSpec((R, KD), row),
                      pl.BlockSpec((R, D), row)]
        out_shape += [jax.ShapeDtypeStruct((n_rows, KD), _F32),
                      jax.ShapeDtypeStruct((n_rows, KD), _F32),
                      jax.ShapeDtypeStruct((n_rows, D), _F32)]
    kern = functools.partial(_ffn_kernel, prompt=prompt, with_oproj=with_oproj,
                             with_kvq=with_kvq)
    name = "ffn_%s_%d" % ("prompt" if prompt else "sample", layer)
    return pl.pallas_call(
        kern, grid=grid, in_specs=in_specs, out_specs=out_specs, out_shape=out_shape,
        scratch_shapes=scratch, compiler_params=_params(len(grid)), name=name,
    )(*args)


def _t5_buckets(dist):
    max_exact = N_BUCKETS // 2
    d = np.maximum(dist, 0)
    df = np.maximum(d, 1).astype(np.float32)
    large = max_exact + (np.log(df / np.float32(max_exact))
                         / np.float32(math.log(MAX_DISTANCE / max_exact))
                         * np.float32(N_BUCKETS - max_exact)).astype(np.int32)
    large = np.minimum(large, N_BUCKETS - 1)
    return np.where(d < max_exact, d, large).astype(np.int32)


def _build_bias(bias_ref, bucket_ref, table_ref):
    bucket = bucket_ref[...]
    for hd in range(N_HEADS):
        acc = jnp.zeros(bucket.shape, _F32)
        for bk in range(N_BUCKETS):
            acc = jnp.where(bucket == bk, table_ref[bk, hd], acc)
        bias_ref[hd] = acc


def _attend_group(q_pairs, kdup, v_lo, v_hi, gq, bias_ref, sinks_ref, valid, g, nq):
    lane = lax.broadcasted_iota(jnp.int32, (nq, LANES), 1)
    low = lane < HEAD_DIM
    rows = []
    for pair in q_pairs:
        for hf in range(2):
            qm = jnp.where(low if hf == 0 else jnp.logical_not(low), pair, 0.0)
            ss = jnp.sum(qm * qm, axis=-1, keepdims=True) * (1.0 / HEAD_DIM)
            rows.append(qm * lax.rsqrt(ss + EPS) * gq * SCALE)
    qs = jnp.concatenate(rows, axis=0).astype(_BF)
    s = lax.dot_general(qs, kdup, (((1,), (1,)), ((), ())), preferred_element_type=_F32)
    probs = []
    for hl in range(GROUP):
        hd = g * GROUP + hl
        logits = jnp.where(valid, s[hl * nq:(hl + 1) * nq, :] + bias_ref[hd], NEG)
        sink = sinks_ref[hd]
        m = jnp.maximum(jnp.max(logits, axis=-1, keepdims=True), sink)
        e = jnp.exp(logits - m)
        denom = jnp.sum(e, axis=-1, keepdims=True) + jnp.exp(sink - m)
        probs.append(e * (1.0 / denom))
    pm = jnp.concatenate(probs, axis=0).astype(_BF)
    if nq % (2 * SUBLANES) == 0:
        blk = lambda t, hl: jnp.dot(pm[hl * nq:(hl + 1) * nq, :], t, preferred_element_type=_F32)
    else:
        full = {id(t): jnp.dot(pm, t, preferred_element_type=_F32) for t in (v_lo, v_hi)}
        blk = lambda t, hl: full[id(t)][hl * nq:(hl + 1) * nq, :]
    return [blk(v_lo, 2 * pi) + blk(v_hi, 2 * pi + 1) for pi in range(2)]


def _kv_operands(kb, vb, g):
    S = kb.shape[0]
    lane = lax.broadcasted_iota(jnp.int32, (S, LANES), 1)
    low = lane < HEAD_DIM
    c = g // 2
    kc = kb[:, c * LANES:(c + 1) * LANES]
    vc = vb[:, c * LANES:(c + 1) * LANES]
    kr = pltpu.roll(kc, HEAD_DIM, 1)
    vr = pltpu.roll(vc, HEAD_DIM, 1)
    if g % 2 == 0:
        kdup = jnp.where(low, kc, kr)
        v_lo = jnp.where(low, vc, 0.0)
        v_hi = jnp.where(low, 0.0, vr)
    else:
        kdup = jnp.where(low, kr, kc)
        v_lo = jnp.where(low, vr, 0.0)
        v_hi = jnp.where(low, 0.0, vc)
    return kdup.astype(_BF), v_lo.astype(_BF), v_hi.astype(_BF)


def _attn_prompt_kernel(q_ref, kp_ref, kc_ref, vp_ref, vc_ref, bucket_ref, table_ref,
                        sinks_ref, gq_ref, o_ref, bias_ref):
    W = WINDOW
    n = pl.program_id(1)

    @pl.when(jnp.logical_and(pl.program_id(0) == 0, n == 0))
    def _():
        _build_bias(bias_ref, bucket_ref, table_ref)

    qi = lax.broadcasted_iota(jnp.int32, (W, 2 * W), 0)
    sj = lax.broadcasted_iota(jnp.int32, (W, 2 * W), 1)
    dist = qi + W - sj
    first_key = jnp.where(n > 0, 0, W)
    valid = (dist >= 0) & (dist < WINDOW) & (sj >= first_key)
    kb = jnp.concatenate([kp_ref[...], kc_ref[...]], axis=0)
    vb = jnp.concatenate([vp_ref[...], vc_ref[...]], axis=0)
    q = q_ref[...]
    gq = gq_ref[...]
    outs = []
    for g in range(N_KV_HEADS):
        kdup, v_lo, v_hi = _kv_operands(kb, vb, g)
        pairs = [q[:, (2 * g + i) * LANES:(2 * g + i + 1) * LANES] for i in range(2)]
        outs += _attend_group(pairs, kdup, v_lo, v_hi, gq, bias_ref, sinks_ref, valid, g, W)
    o_ref[...] = jnp.concatenate(outs, axis=1).astype(o_ref.dtype)


def _attn_prompt(q2d, k2d, v2d, bucket, table, sinks, gq2):
    W = WINDOW
    nb = SEQ // W
    cur = lambda b, n: (b * nb + n, 0)
    prev = lambda b, n: (b * nb + jnp.maximum(n - 1, 0), 0)
    smem = pl.BlockSpec(memory_space=pltpu.SMEM)
    return pl.pallas_call(
        _attn_prompt_kernel,
        grid=(BATCH, nb),
        in_specs=[pl.BlockSpec((W, D_MODEL), cur),
                  pl.BlockSpec((W, KD), prev), pl.BlockSpec((W, KD), cur),
                  pl.BlockSpec((W, KD), prev), pl.BlockSpec((W, KD), cur),
                  _const_spec((W, 2 * W)), smem, smem, _const_spec((1, LANES))],
        out_specs=pl.BlockSpec((W, D_MODEL), cur),
        out_shape=jax.ShapeDtypeStruct((BATCH * SEQ, D_MODEL), _BF),
        scratch_shapes=[pltpu.VMEM((N_HEADS, W, 2 * W), _F32)],
        compiler_params=_params(2),
        name="attn_prompt",
    )(q2d, k2d, k2d, v2d, v2d, bucket, table, sinks, gq2)


def _attn_sample_kernel(q_ref, kn_ref, vn_ref, ck_ref, cv_ref, bucket_ref, table_ref,
                        sinks_ref, gq_ref, o_ref, ko_ref, vo_ref, bias_ref, kbuf_ref, vbuf_ref):
    W = WINDOW
    L = 2 * W

    @pl.when(pl.program_id(0) == 0)
    def _():
        _build_bias(bias_ref, bucket_ref, table_ref)
        kbuf_ref[...] = jnp.zeros((L, KD), _F32)
        vbuf_ref[...] = jnp.zeros((L, KD), _F32)

    qi = lax.broadcasted_iota(jnp.int32, (DEC_SEQ, L), 0)
    sj = lax.broadcasted_iota(jnp.int32, (DEC_SEQ, L), 1)
    dist = qi + W - sj
    valid = (dist >= 0) & (dist < WINDOW)
    gq = gq_ref[...]

    def body(s, carry):
        kbuf_ref[0:W, :] = ck_ref[s]
        vbuf_ref[0:W, :] = cv_ref[s]
        kbuf_ref[W:W + DEC_SEQ, :] = kn_ref[s]
        vbuf_ref[W:W + DEC_SEQ, :] = vn_ref[s]
        ko_ref[s] = kbuf_ref[DEC_SEQ:DEC_SEQ + W, :]
        vo_ref[s] = vbuf_ref[DEC_SEQ:DEC_SEQ + W, :]
        kb = kbuf_ref[...]
        vb = vbuf_ref[...]
        q = q_ref[s]
        outs = []
        for g in range(N_KV_HEADS):
            kdup, v_lo, v_hi = _kv_operands(kb, vb, g)
            pairs = [q[:, (2 * g + i) * LANES:(2 * g + i + 1) * LANES] for i in range(2)]
            outs += _attend_group(pairs, kdup, v_lo, v_hi, gq, bias_ref, sinks_ref, valid, g,
                                  DEC_SEQ)
        o_ref[s] = jnp.concatenate(outs, axis=1)
        return carry

    lax.fori_loop(0, ATT_SEQS, body, 0)


def _attn_sample(q3d, kn3d, vn3d, ck, cv, bucket, table, sinks, gq2):
    W = WINDOW
    A = ATT_SEQS
    blk = lambda r, c: pl.BlockSpec((A, r, c), lambda i: (i, 0, 0))
    smem = pl.BlockSpec(memory_space=pltpu.SMEM)
    return pl.pallas_call(
        _attn_sample_kernel,
        grid=(DEC_BATCH // A,),
        in_specs=[blk(DEC_SEQ, D_MODEL), blk(DEC_SEQ, KD), blk(DEC_SEQ, KD),
                  blk(W, KD), blk(W, KD),
                  _const_spec((DEC_SEQ, 2 * W)), smem, smem, _const_spec((1, LANES))],
        out_specs=[blk(DEC_SEQ, D_MODEL), blk(W, KD), blk(W, KD)],
        out_shape=[jax.ShapeDtypeStruct((DEC_BATCH, DEC_SEQ, D_MODEL), _F32),
                   jax.ShapeDtypeStruct((DEC_BATCH, W, KD), _F32),
                   jax.ShapeDtypeStruct((DEC_BATCH, W, KD), _F32)],
        scratch_shapes=[pltpu.VMEM((N_HEADS, DEC_SEQ, 2 * W), _F32),
                        pltpu.VMEM((2 * W, KD), _F32), pltpu.VMEM((2 * W, KD), _F32)],
        compiler_params=_params(1),
        name="attn_sample",
    )(q3d, kn3d, vn3d, ck, cv, bucket, table, sinks, gq2)


def kernel(x_prompt, x_sample, state_conv, state_ffn, cache_k, cache_v, p_prompt, p_sample, g_mix, cm_w_pw1, cm_b_pw1, cm_w_dw, cm_b_dw, cm_ln_g, cm_ln_b, cm_w_pw2, cm_b_pw2, at_w_q, at_g_q, at_sinks, at_w_o, kv_g, kv_w_k, kv_w_v, kv_g_k, rel_bias, g_ffn, ffn_w_up, ffn_w_dw, ffn_b_dw, ffn_w_down, g_ple, ple_w_gate, ple_w_proj):
    D, F = D_MODEL, D_FF
    row = lambda a: a.reshape(1, -1)
    bf = lambda a: a.astype(_BF)

    mixer_w = [row(g_mix[0]), bf(cm_w_pw1[0]), row(cm_b_pw1[0]), cm_w_dw[0], row(cm_b_dw[0]),
               row(cm_ln_g[0]), row(cm_ln_b[0]), bf(cm_w_pw2[0]), row(cm_b_pw2[0])]
    ffn_w = []
    for i in range(2):
        ffn_w.append(dict(
            g_ffn=row(g_ffn[i]), w_up=bf(ffn_w_up[i]), w_dw=ffn_w_dw[i], b_dw=row(ffn_b_dw[i]),
            w_down=bf(ffn_w_down[i]), g_ple=row(g_ple[i]), w_gate=bf(ple_w_gate[i]),
            w_proj=bf(ple_w_proj[i])))
    ffn_w[0].update(g_kv=row(kv_g), w_k=bf(kv_w_k), w_v=bf(kv_w_v),
                    g_k2=row(jnp.tile(kv_g_k, 2)), g_q_in=row(g_mix[1]), w_q=bf(at_w_q[0]))
    ffn_w[1].update(wo=bf(at_w_o[0]))
    gq2 = row(jnp.tile(at_g_q[0], 2))
    sinks = at_sinks[0]

    W = WINDOW
    qi = np.arange(W)[:, None]
    bucket_p = jnp.asarray(_t5_buckets(qi + W - np.arange(2 * W)[None, :]))
    bucket_s = jnp.asarray(_t5_buckets(np.arange(DEC_SEQ)[:, None] + W
                                       - np.arange(2 * W)[None, :]))

    xp = x_prompt.reshape(BATCH * SEQ, D)
    pp = p_prompt.reshape(2, BATCH * SEQ, PLE_DIM)
    h1, conv_p = _mixer_prompt(xp, mixer_w)
    h3, ffn_p0, k_p, v_p, q_p = _ffn_call(h1, pp, None, None, ffn_w[0], layer=0, prompt=True,
                                          with_oproj=False, with_kvq=True)
    o_p = _attn_prompt(q_p, k_p, v_p, bucket_p, rel_bias, sinks, gq2)
    y_p, ffn_p1 = _ffn_call(h3, pp, None, o_p, ffn_w[1], layer=1, prompt=True,
                            with_oproj=True, with_kvq=False)

    xs = x_sample.reshape(DEC_BATCH * DEC_SEQ, D)
    ps = p_sample.reshape(2, DEC_BATCH * DEC_SEQ, PLE_DIM)
    g1, conv_s = _mixer_sample(xs, state_conv, mixer_w)
    g3, ffn_s0, k_s, v_s, q_s = _ffn_call(g1, ps, state_ffn, None, ffn_w[0], layer=0,
                                          prompt=False, with_oproj=False, with_kvq=True)
    o_s, k_out, v_out = _attn_sample(
        q_s.reshape(DEC_BATCH, DEC_SEQ, D), k_s.reshape(DEC_BATCH, DEC_SEQ, KD),
        v_s.reshape(DEC_BATCH, DEC_SEQ, KD), cache_k.reshape(DEC_BATCH, W, KD),
        cache_v.reshape(DEC_BATCH, W, KD), bucket_s, rel_bias, sinks, gq2)
    y_s, ffn_s1 = _ffn_call(g3, ps, state_ffn, o_s.reshape(DEC_BATCH * DEC_SEQ, D),
                            ffn_w[1], layer=1, prompt=False, with_oproj=True, with_kvq=False)

    kv_shape_p = (BATCH, W, N_KV_HEADS, HEAD_DIM)
    kv_shape_s = (DEC_BATCH, W, N_KV_HEADS, HEAD_DIM)
    return (y_p.reshape(BATCH, SEQ, D), y_s.reshape(DEC_BATCH, DEC_SEQ, D),
            conv_p[None], conv_s[None],
            jnp.stack([ffn_p0, ffn_p1]), jnp.stack([ffn_s0, ffn_s1]),
            k_p.reshape(BATCH, SEQ, KD)[:, SEQ - W:].reshape(kv_shape_p),
            k_out.reshape(kv_shape_s),
            v_p.reshape(BATCH, SEQ, KD)[:, SEQ - W:].reshape(kv_shape_p),
            v_out.reshape(kv_shape_s))
```

```python
import functools
import math

import jax
import jax.numpy as jnp
import numpy as np
from jax import lax
from jax.experimental import pallas as pl
from jax.experimental.pallas import tpu as pltpu

D_MODEL = 1024
BATCH = 8
SEQ = 2048
DEC_BATCH = 128
DEC_SEQ = 8
CONV_K = 31
FFN_CONV_K = 3
D_FF = 2816
N_HEADS = 16
N_KV_HEADS = 4
HEAD_DIM = 64
GROUP = N_HEADS // N_KV_HEADS
WINDOW = 128
N_BUCKETS = 32
MAX_DISTANCE = 128
PLE_DIM = 256
EPS = 1e-6
SCALE = HEAD_DIM ** -0.5
NEG = -1e30
KD = N_KV_HEADS * HEAD_DIM

LANES = 128
SUBLANES = 8
VMEM_LIMIT = 56 * 1024 * 1024

T_TILE = 512
S_TILE = 32
CONV_HALO = 32
FFN_HALO = 8
F_CHUNK = 2816
D_SLABS = D_MODEL // LANES
F_SLABS = D_FF // LANES
ATT_SEQS = 16

_BF = jnp.bfloat16
_F32 = jnp.float32


def _bdot(a, w):
    return jnp.dot(a.astype(_BF), w, preferred_element_type=_F32)


def _rms(x, g):
    return x * lax.rsqrt(jnp.mean(x * x, axis=-1, keepdims=True) + EPS) * g


def _sigmoid(x):
    return 1.0 / (1.0 + jnp.exp(-x))


def _const_spec(shape):
    nd = len(shape)
    return pl.BlockSpec(shape, lambda *_: (0,) * nd, pipeline_mode=pl.Buffered(1))


def _layer_spec(shape, layer):
    nd = len(shape)
    return pl.BlockSpec((None,) + tuple(shape), lambda *_: (layer,) + (0,) * nd,
                        pipeline_mode=pl.Buffered(1))


def _params(n_grid):
    return pltpu.CompilerParams(dimension_semantics=("arbitrary",) * n_grid,
                                vmem_limit_bytes=VMEM_LIMIT)


def _mixer_tail(x, c, bdw, lng, lnb, w2, b2):
    c = c + bdw
    mu = jnp.mean(c, axis=-1, keepdims=True)
    cz = c - mu
    var = jnp.mean(cz * cz, axis=-1, keepdims=True)
    y = cz * lax.rsqrt(var + EPS) * lng + lnb
    s = y * _sigmoid(y)
    return x + _bdot(s, w2) + b2


def _mixer_prompt_kernel(x_ref, g_ref, w1_ref, b1_ref, wdw_ref, bdw_ref, lng_ref, lnb_ref,
                         w2_ref, b2_ref, h_ref, cs_ref, slab_ref, oslab_ref):
    T = T_TILE
    H = T // 2
    j = pl.program_id(1)

    @pl.when(j == 0)
    def _():
        slab_ref[:, 0:CONV_HALO, :] = jnp.zeros((D_SLABS, CONV_HALO, LANES), _F32)

    x = x_ref[...]
    u = _bdot(_rms(x, g_ref[...]), w1_ref[...]) + b1_ref[...]
    glu = u[:, :D_MODEL] * _sigmoid(u[:, D_MODEL:])
    for c in range(D_SLABS):
        slab_ref[c, CONV_HALO:CONV_HALO + T, :] = glu[:, c * LANES:(c + 1) * LANES]
    cs_ref[...] = glu[T - (CONV_K - 1):, :]

    base = CONV_HALO - (CONV_K - 1)
    for c in range(D_SLABS):
        wc = wdw_ref[:, c * LANES:(c + 1) * LANES]
        for p in range(2):
            acc = slab_ref[c, pl.ds(base + p, H, stride=2), :] * wc[0:1, :]
            for k in range(1, CONV_K):
                acc = acc + slab_ref[c, pl.ds(base + k + p, H, stride=2), :] * wc[k:k + 1, :]
            oslab_ref[c, pl.ds(p, H, stride=2), :] = acc
    conv = jnp.concatenate([oslab_ref[c] for c in range(D_SLABS)], axis=1)
    h_ref[...] = _mixer_tail(x, conv, bdw_ref[...], lng_ref[...], lnb_ref[...],
                             w2_ref[...], b2_ref[...])
    for c in range(D_SLABS):
        slab_ref[c, 0:CONV_HALO, :] = slab_ref[c, T:T + CONV_HALO, :]


def _mixer_sample_kernel(x_ref, st_ref, g_ref, w1_ref, b1_ref, wdw_ref, bdw_ref, lng_ref,
                         lnb_ref, w2_ref, b2_ref, h_ref, cs_ref, win_ref):
    S = S_TILE
    KH = CONV_K - 1
    x = x_ref[...]
    u = _bdot(_rms(x, g_ref[...]), w1_ref[...]) + b1_ref[...]
    glu = u[:, :D_MODEL] * _sigmoid(u[:, D_MODEL:])
    win_ref[:, 0:KH, :] = st_ref[...]
    win_ref[:, KH:KH + DEC_SEQ, :] = glu.reshape(S, DEC_SEQ, D_MODEL)
    cs_ref[...] = win_ref[:, DEC_SEQ:DEC_SEQ + KH, :]
    acc = win_ref[:, 0:DEC_SEQ, :] * wdw_ref[0:1, :]
    for k in range(1, CONV_K):
        acc = acc + win_ref[:, k:k + DEC_SEQ, :] * wdw_ref[k:k + 1, :]
    conv = acc.reshape(S * DEC_SEQ, D_MODEL)
    h_ref[...] = _mixer_tail(x, conv, bdw_ref[...], lng_ref[...], lnb_ref[...],
                             w2_ref[...], b2_ref[...])


def _mixer_weight_specs():
    D = D_MODEL
    return [_const_spec((1, D)), _const_spec((D, 2 * D)), _const_spec((1, 2 * D)),
            _const_spec((CONV_K, D)), _const_spec((1, D)), _const_spec((1, D)),
            _const_spec((1, D)), _const_spec((D, D)), _const_spec((1, D))]


def _mixer_prompt(x2d, wts):
    T = T_TILE
    nt = SEQ // T
    row = lambda b, j: (b * nt + j, 0)
    return pl.pallas_call(
        _mixer_prompt_kernel,
        grid=(BATCH, nt),
        in_specs=[pl.BlockSpec((T, D_MODEL), row)] + _mixer_weight_specs(),
        out_specs=[pl.BlockSpec((T, D_MODEL), row),
                   pl.BlockSpec((None, CONV_K - 1, D_MODEL), lambda b, j: (b, 0, 0))],
        out_shape=[jax.ShapeDtypeStruct((BATCH * SEQ, D_MODEL), _F32),
                   jax.ShapeDtypeStruct((BATCH, CONV_K - 1, D_MODEL), _F32)],
        scratch_shapes=[pltpu.VMEM((D_SLABS, CONV_HALO + T, LANES), _F32),
                        pltpu.VMEM((D_SLABS, T, LANES), _F32)],
        compiler_params=_params(2),
        name="mixer_prompt",
    )(x2d, *wts)


def _mixer_sample(x2d, state4d, wts):
    S = S_TILE
    R = S * DEC_SEQ
    return pl.pallas_call(
        _mixer_sample_kernel,
        grid=(DEC_BATCH // S,),
        in_specs=[pl.BlockSpec((R, D_MODEL), lambda i: (i, 0)),
                  pl.BlockSpec((None, S, CONV_K - 1, D_MODEL), lambda i: (0, i, 0, 0))]
                 + _mixer_weight_specs(),
        out_specs=[pl.BlockSpec((R, D_MODEL), lambda i: (i, 0)),
                   pl.BlockSpec((S, CONV_K - 1, D_MODEL), lambda i: (i, 0, 0))],
        out_shape=[jax.ShapeDtypeStruct((DEC_BATCH * DEC_SEQ, D_MODEL), _F32),
                   jax.ShapeDtypeStruct((DEC_BATCH, CONV_K - 1, D_MODEL), _F32)],
        scratch_shapes=[pltpu.VMEM((S, CONV_K - 1 + DEC_SEQ + 2, D_MODEL), _F32)],
        compiler_params=_params(1),
        name="mixer_sample",
    )(x2d, state4d, *wts)


def _ffn_kernel(*refs, prompt, with_oproj, with_kvq):
    it = iter(refs)
    h_ref = next(it)
    p_ref = next(it)
    st_ref = None if prompt else next(it)
    if with_oproj:
        o_ref = next(it)
        wo_ref = next(it)
    gf_ref, wup_ref, wdw_ref, bdw_ref, wdn_ref = (next(it) for _ in range(5))
    gp_ref, wg_ref, wp_ref = (next(it) for _ in range(3))
    if with_kvq:
        gkv_ref, wk_ref, wv_ref, gk_ref, gq_ref, wq_ref = (next(it) for _ in range(6))
    hout_ref = next(it)
    fs_ref = next(it)
    if with_kvq:
        k_ref, v_ref, q_ref = (next(it) for _ in range(3))
    if prompt:
        slab_ref, oslab_ref = next(it), next(it)
    else:
        win_ref = next(it)

    F = D_FF
    R = T_TILE if prompt else S_TILE * DEC_SEQ
    H = R // 2
    KH = FFN_CONV_K - 1

    if prompt:
        @pl.when(pl.program_id(1) == 0)
        def _():
            slab_ref[:, 0:FFN_HALO, :] = jnp.zeros((2 * F_SLABS, FFN_HALO, LANES), _F32)
    else:
        win_ref[:, FFN_HALO - KH:FFN_HALO, :] = st_ref[...]

    h = h_ref[...]
    if with_oproj:
        h = h + _bdot(o_ref[...], wo_ref[...])
    xn = _rms(h, gf_ref[...]).astype(_BF)

    n_chunks = F // F_CHUNK
    spc = F_CHUNK // LANES
    acc = None
    for ci in range(n_chunks):
        halves = []
        for half in range(2):
            c0 = half * F + ci * F_CHUNK
            uc = jnp.dot(xn, wup_ref[:, c0:c0 + F_CHUNK], preferred_element_type=_F32)
            w = wdw_ref[:, c0:c0 + F_CHUNK]
            b = bdw_ref[:, c0:c0 + F_CHUNK]
            if prompt:
                cols = []
                for s in range(spc):
                    sl = half * F_SLABS + ci * spc + s
                    slab_ref[sl, FFN_HALO:FFN_HALO + R, :] = uc[:, s * LANES:(s + 1) * LANES]
                    ws = w[:, s * LANES:(s + 1) * LANES]
                    ph = []
                    for p in range(2):
                        cv = b[:, s * LANES:(s + 1) * LANES]
                        for k in range(FFN_CONV_K):
                            cv = cv + (slab_ref[sl, pl.ds(FFN_HALO - KH + k + p, H, stride=2), :]
                                       * ws[k:k + 1, :])
                        ph.append(cv)
                    cols.append(jnp.concatenate(ph, axis=0))
                halves.append(jnp.concatenate(cols, axis=1))
            else:
                win_ref[:, FFN_HALO:FFN_HALO + DEC_SEQ, c0:c0 + F_CHUNK] = uc.reshape(
                    S_TILE, DEC_SEQ, F_CHUNK)
                cv = b
                for k in range(FFN_CONV_K):
                    o = FFN_HALO - KH + k
                    cv = cv + win_ref[:, o:o + DEC_SEQ, c0:c0 + F_CHUNK] * w[k:k + 1, :]
                halves.append(cv.reshape(R, F_CHUNK))
        gate, val = halves
        act = (gate * _sigmoid(gate) * val).astype(_BF)
        part = jnp.dot(act, wdn_ref[ci * F_CHUNK:(ci + 1) * F_CHUNK, :],
                       preferred_element_type=_F32)
        acc = part if acc is None else acc + part

    if prompt:
        for c in range(D_SLABS):
            oslab_ref[c, pl.ds(0, H, stride=2), :] = acc[:H, c * LANES:(c + 1) * LANES]
            oslab_ref[c, pl.ds(1, H, stride=2), :] = acc[H:, c * LANES:(c + 1) * LANES]
        f = jnp.concatenate([oslab_ref[c] for c in range(D_SLABS)], axis=1)
        fs_ref[...] = jnp.concatenate(
            [slab_ref[s, FFN_HALO + R - SUBLANES:FFN_HALO + R, :] for s in range(2 * F_SLABS)],
            axis=1)[SUBLANES - KH:, :]
        for s in range(2 * F_SLABS):
            slab_ref[s, 0:FFN_HALO, :] = slab_ref[s, R:R + FFN_HALO, :]
    else:
        f = acc
        fs_ref[...] = win_ref[:, FFN_HALO + DEC_SEQ - KH:FFN_HALO + DEC_SEQ, :]

    h2 = h + f
    gate = _sigmoid(_bdot(_rms(h2, gp_ref[...]), wg_ref[...]))
    h3 = h2 + gate * _bdot(p_ref[...], wp_ref[...])
    hout_ref[...] = h3

    if with_kvq:
        hk = _rms(h3, gkv_ref[...]).astype(_BF)
        kk = jnp.dot(hk, wk_ref[...], preferred_element_type=_F32)
        v_ref[...] = jnp.dot(hk, wv_ref[...], preferred_element_type=_F32)
        lane = lax.broadcasted_iota(jnp.int32, (R, LANES), 1)
        low = lane < HEAD_DIM
        kn = []
        for c in range(KD // LANES):
            kc = kk[:, c * LANES:(c + 1) * LANES]
            sq = kc * kc
            ss_lo = jnp.sum(jnp.where(low, sq, 0.0), axis=-1, keepdims=True)
            ss_hi = jnp.sum(jnp.where(low, 0.0, sq), axis=-1, keepdims=True)
            r = lax.rsqrt(jnp.where(low, ss_lo, ss_hi) * (1.0 / HEAD_DIM) + EPS)
            kn.append(kc * r * gk_ref[...])
        k_ref[...] = jnp.concatenate(kn, axis=1)
        q_ref[...] = _bdot(_rms(h3, gq_ref[...]), wq_ref[...])


def _ffn_call(h2d, p3d, state4d, o2d, w, *, layer, prompt, with_oproj, with_kvq):
    D, F = D_MODEL, D_FF
    if prompt:
        R = T_TILE
        nt = SEQ // R
        grid = (BATCH, nt)
        row = lambda b, j: (b * nt + j, 0)
        p_spec = pl.BlockSpec((None, R, PLE_DIM), lambda b, j: (layer, b * nt + j, 0))
        n_rows, n_seq = BATCH * SEQ, BATCH
        fs_spec = pl.BlockSpec((None, FFN_CONV_K - 1, 2 * F), lambda b, j: (b, 0, 0))
        scratch = [pltpu.VMEM((2 * F_SLABS, FFN_HALO + R, LANES), _F32),
                   pltpu.VMEM((D_SLABS, R, LANES), _F32)]
    else:
        R = S_TILE * DEC_SEQ
        grid = (DEC_BATCH // S_TILE,)
        row = lambda i: (i, 0)
        p_spec = pl.BlockSpec((None, R, PLE_DIM), lambda i: (layer, i, 0))
        n_rows, n_seq = DEC_BATCH * DEC_SEQ, DEC_BATCH
        fs_spec = pl.BlockSpec((S_TILE, FFN_CONV_K - 1, 2 * F), lambda i: (i, 0, 0))
        scratch = [pltpu.VMEM((S_TILE, FFN_HALO + DEC_SEQ, 2 * F), _F32)]

    args = [h2d, p3d]
    in_specs = [pl.BlockSpec((R, D), row), p_spec]
    if not prompt:
        args.append(state4d)
        in_specs.append(pl.BlockSpec((None, S_TILE, FFN_CONV_K - 1, 2 * F),
                                     lambda i: (layer, i, 0, 0)))
    if with_oproj:
        args += [o2d, w["wo"]]
        in_specs += [pl.BlockSpec((R, D), row), _const_spec((D, D))]
    args += [w["g_ffn"], w["w_up"], w["w_dw"], w["b_dw"], w["w_down"],
             w["g_ple"], w["w_gate"], w["w_proj"]]
    in_specs += [_layer_spec((1, D), layer), _layer_spec((D, 2 * F), layer),
                 _layer_spec((FFN_CONV_K, 2 * F), layer), _layer_spec((1, 2 * F), layer),
                 _layer_spec((F, D), layer), _layer_spec((1, D), layer),
                 _layer_spec((D, D), layer), _layer_spec((PLE_DIM, D), layer)]
    out_specs = [pl.BlockSpec((R, D), row), fs_spec]
    out_shape = [jax.ShapeDtypeStruct((n_rows, D), _F32),
                 jax.ShapeDtypeStruct((n_seq, FFN_CONV_K - 1, 2 * F), _F32)]
    if with_kvq:
        args += [w["g_kv"], w["w_k"], w["w_v"], w["g_k2"], w["g_q_in"], w["w_q"]]
        in_specs += [_const_spec((1, D)), _const_spec((D, KD)), _const_spec((D, KD)),
                     _const_spec((1, LANES)), _const_spec((1, D)), _const_spec((D, D))]
        out_specs += [pl.BlockSpec((R, KD), row), pl.BlockSpec((R, KD), row),
                      pl.BlockSpec((R, D), row)]
        out_shape += [jax.ShapeDtypeStruct((n_rows, KD), _F32),
                      jax.ShapeDtypeStruct((n_rows, KD), _F32),
                      jax.ShapeDtypeStruct((n_rows, D), _F32)]
    kern = functools.partial(_ffn_kernel, prompt=prompt, with_oproj=with_oproj,
                             with_kvq=with_kvq)
    name = "ffn_%s_%d" % ("prompt" if prompt else "sample", layer)
    return pl.pallas_call(
        kern, grid=grid, in_specs=in_specs, out_specs=out_specs, out_shape=out_shape,
        scratch_shapes=scratch, compiler_params=_params(len(grid)), name=name,
    )(*args)


def _t5_buckets(dist):
    max_exact = N_BUCKETS // 2
    d = np.maximum(dist, 0)
    df = np.maximum(d, 1).astype(np.float32)
    large = max_exact + (np.log(df / np.float32(max_exact))
                         / np.float32(math.log(MAX_DISTANCE / max_exact))
                         * np.float32(N_BUCKETS - max_exact)).astype(np.int32)
    large = np.minimum(large, N_BUCKETS - 1)
    return np.where(d < max_exact, d, large).astype(np.int32)


def _build_bias(bias_ref, bucket_ref, table_ref):
    bucket = bucket_ref[...]
    for hd in range(N_HEADS):
        acc = jnp.zeros(bucket.shape, _F32)
        for bk in range(N_BUCKETS):
            acc = jnp.where(bucket == bk, table_ref[bk, hd], acc)
        bias_ref[hd] = acc


def _attend_group(q_pairs, kdup, v_lo, v_hi, gq, bias_ref, sinks_ref, valid, g, nq):
    lane = lax.broadcasted_iota(jnp.int32, (nq, LANES), 1)
    low = lane < HEAD_DIM
    rows = []
    for pair in q_pairs:
        for hf in range(2):
            qm = jnp.where(low if hf == 0 else jnp.logical_not(low), pair, 0.0)
            ss = jnp.sum(qm * qm, axis=-1, keepdims=True) * (1.0 / HEAD_DIM)
            rows.append(qm * lax.rsqrt(ss + EPS) * gq * SCALE)
    qs = jnp.concatenate(rows, axis=0).astype(_BF)
    s = lax.dot_general(qs, kdup, (((1,), (1,)), ((), ())), preferred_element_type=_F32)
    probs = []
    for hl in range(GROUP):
        hd = g * GROUP + hl
        logits = jnp.where(valid, s[hl * nq:(hl + 1) * nq, :] + bias_ref[hd], NEG)
        sink = sinks_ref[hd]
        m = jnp.maximum(jnp.max(logits, axis=-1, keepdims=True), sink)
        e = jnp.exp(logits - m)
        denom = jnp.sum(e, axis=-1, keepdims=True) + jnp.exp(sink - m)
        probs.append(e * (1.0 / denom))
    pm = jnp.concatenate(probs, axis=0).astype(_BF)
    if nq % (2 * SUBLANES) == 0:
        blk = lambda t, hl: jnp.dot(pm[hl * nq:(hl + 1) * nq, :], t, preferred_element_type=_F32)
    else:
        full = {id(t): jnp.dot(pm, t, preferred_element_type=_F32) for t in (v_lo, v_hi)}
        blk = lambda t, hl: full[id(t)][hl * nq:(hl + 1) * nq, :]
    return [blk(v_lo, 2 * pi) + blk(v_hi, 2 * pi + 1) for pi in range(2)]


def _kv_operands(kb, vb, g):
    S = kb.shape[0]
    lane = lax.broadcasted_iota(jnp.int32, (S, LANES), 1)
    low = lane < HEAD_DIM
    c = g // 2
    kc = kb[:, c * LANES:(c + 1) * LANES]
    vc = vb[:, c * LANES:(c + 1) * LANES]
    kr = pltpu.roll(kc, HEAD_DIM, 1)
    vr = pltpu.roll(vc, HEAD_DIM, 1)
    if g % 2 == 0:
        kdup = jnp.where(low, kc, kr)
        v_lo = jnp.where(low, vc, 0.0)
        v_hi = jnp.where(low, 0.0, vr)
    else:
        kdup = jnp.where(low, kr, kc)
        v_lo = jnp.where(low, vr, 0.0)
        v_hi = jnp.where(low, 0.0, vc)
    return kdup.astype(_BF), v_lo.astype(_BF), v_hi.astype(_BF)


def _attn_prompt_kernel(q_ref, kp_ref, kc_ref, vp_ref, vc_ref, bucket_ref, table_ref,
                        sinks_ref, gq_ref, o_ref, bias_ref):
    W = WINDOW
    n = pl.program_id(1)

    @pl.when(jnp.logical_and(pl.program_id(0) == 0, n == 0))
    def _():
        _build_bias(bias_ref, bucket_ref, table_ref)

    qi = lax.broadcasted_iota(jnp.int32, (W, 2 * W), 0)
    sj = lax.broadcasted_iota(jnp.int32, (W, 2 * W), 1)
    dist = qi + W - sj
    first_key = jnp.where(n > 0, 0, W)
    valid = (dist >= 0) & (dist < WINDOW) & (sj >= first_key)
    kb = jnp.concatenate([kp_ref[...], kc_ref[...]], axis=0)
    vb = jnp.concatenate([vp_ref[...], vc_ref[...]], axis=0)
    q = q_ref[...]
    gq = gq_ref[...]
    outs = []
    for g in range(N_KV_HEADS):
        kdup, v_lo, v_hi = _kv_operands(kb, vb, g)
        pairs = [q[:, (2 * g + i) * LANES:(2 * g + i + 1) * LANES] for i in range(2)]
        outs += _attend_group(pairs, kdup, v_lo, v_hi, gq, bias_ref, sinks_ref, valid, g, W)
    o_ref[...] = jnp.concatenate(outs, axis=1).astype(o_ref.dtype)


def _attn_prompt(q2d, k2d, v2d, bucket, table, sinks, gq2):
    W = WINDOW
    nb = SEQ // W
    cur = lambda b, n: (b * nb + n, 0)
    prev = lambda b, n: (b * nb + jnp.maximum(n - 1, 0), 0)
    smem = pl.BlockSpec(memory_space=pltpu.SMEM)
    return pl.pallas_call(
        _attn_prompt_kernel,
        grid=(BATCH, nb),
        in_specs=[pl.BlockSpec((W, D_MODEL), cur),
                  pl.BlockSpec((W, KD), prev), pl.BlockSpec((W, KD), cur),
                  pl.BlockSpec((W, KD), prev), pl.BlockSpec((W, KD), cur),
                  _const_spec((W, 2 * W)), smem, smem, _const_spec((1, LANES))],
        out_specs=pl.BlockSpec((W, D_MODEL), cur),
        out_shape=jax.ShapeDtypeStruct((BATCH * SEQ, D_MODEL), _BF),
        scratch_shapes=[pltpu.VMEM((N_HEADS, W, 2 * W), _F32)],
        compiler_params=_params(2),
        name="attn_prompt",
    )(q2d, k2d, k2d, v2d, v2d, bucket, table, sinks, gq2)


def _attn_sample_kernel(q_ref, kn_ref, vn_ref, ck_ref, cv_ref, bucket_ref, table_ref,
                        sinks_ref, gq_ref, o_ref, ko_ref, vo_ref, bias_ref, kbuf_ref, vbuf_ref):
    W = WINDOW
    L = 2 * W

    @pl.when(pl.program_id(0) == 0)
    def _():
        _build_bias(bias_ref, bucket_ref, table_ref)
        kbuf_ref[...] = jnp.zeros((L, KD), _F32)
        vbuf_ref[...] = jnp.zeros((L, KD), _F32)

    qi = lax.broadcasted_iota(jnp.int32, (DEC_SEQ, L), 0)
    sj = lax.broadcasted_iota(jnp.int32, (DEC_SEQ, L), 1)
    dist = qi + W - sj
    valid = (dist >= 0) & (dist < WINDOW)
    gq = gq_ref[...]

    def body(s, carry):
        kbuf_ref[0:W, :] = ck_ref[s]
        vbuf_ref[0:W, :] = cv_ref[s]
        kbuf_ref[W:W + DEC_SEQ, :] = kn_ref[s]
        vbuf_ref[W:W + DEC_SEQ, :] = vn_ref[s]
        ko_ref[s] = kbuf_ref[DEC_SEQ:DEC_SEQ + W, :]
        vo_ref[s] = vbuf_ref[DEC_SEQ:DEC_SEQ + W, :]
        kb = kbuf_ref[...]
        vb = vbuf_ref[...]
        q = q_ref[s]
        outs = []
        for g in range(N_KV_HEADS):
            kdup, v_lo, v_hi = _kv_operands(kb, vb, g)
            pairs = [q[:, (2 * g + i) * LANES:(2 * g + i + 1) * LANES] for i in range(2)]
            outs += _attend_group(pairs, kdup, v_lo, v_hi, gq, bias_ref, sinks_ref, valid, g,
                                  DEC_SEQ)
        o_ref[s] = jnp.concatenate(outs, axis=1)
        return carry

    lax.fori_loop(0, ATT_SEQS, body, 0)


def _attn_sample(q3d, kn3d, vn3d, ck, cv, bucket, table, sinks, gq2):
    W = WINDOW
    A = ATT_SEQS
    blk = lambda r, c: pl.BlockSpec((A, r, c), lambda i: (i, 0, 0))
    smem = pl.BlockSpec(memory_space=pltpu.SMEM)
    return pl.pallas_call(
        _attn_sample_kernel,
        grid=(DEC_BATCH // A,),
        in_specs=[blk(DEC_SEQ, D_MODEL), blk(DEC_SEQ, KD), blk(DEC_SEQ, KD),
                  blk(W, KD), blk(W, KD),
                  _const_spec((DEC_SEQ, 2 * W)), smem, smem, _const_spec((1, LANES))],
        out_specs=[blk(DEC_SEQ, D_MODEL), blk(W, KD), blk(W, KD)],
        out_shape=[jax.ShapeDtypeStruct((DEC_BATCH, DEC_SEQ, D_MODEL), _F32),
                   jax.ShapeDtypeStruct((DEC_BATCH, W, KD), _F32),
                   jax.ShapeDtypeStruct((DEC_BATCH, W, KD), _F32)],
        scratch_shapes=[pltpu.VMEM((N_HEADS, DEC_SEQ, 2 * W), _F32),
                        pltpu.VMEM((2 * W, KD), _F32), pltpu.VMEM((2 * W, KD), _F32)],
        compiler_params=_params(1),
        name="attn_sample",
    )(q3d, kn3d, vn3d, ck, cv, bucket, table, sinks, gq2)


def kernel(x_prompt, x_sample, state_conv, state_ffn, cache_k, cache_v, p_prompt, p_sample, g_mix, cm_w_pw1, cm_b_pw1, cm_w_dw, cm_b_dw, cm_ln_g, cm_ln_b, cm_w_pw2, cm_b_pw2, at_w_q, at_g_q, at_sinks, at_w_o, kv_g, kv_w_k, kv_w_v, kv_g_k, rel_bias, g_ffn, ffn_w_up, ffn_w_dw, ffn_b_dw, ffn_w_down, g_ple, ple_w_gate, ple_w_proj):
    D, F = D_MODEL, D_FF
    row = lambda a: a.reshape(1, -1)
    bf = lambda a: a.astype(_BF)

    mixer_w = [row(g_mix[0]), bf(cm_w_pw1[0]), row(cm_b_pw1[0]), cm_w_dw[0], row(cm_b_dw[0]),
               row(cm_ln_g[0]), row(cm_ln_b[0]), bf(cm_w_pw2[0]), row(cm_b_pw2[0])]
    stacked = dict(
        g_ffn=g_ffn[:, None, :], w_up=bf(ffn_w_up), w_dw=ffn_w_dw, b_dw=ffn_b_dw[:, None, :],
        w_down=bf(ffn_w_down), g_ple=g_ple[:, None, :], w_gate=bf(ple_w_gate),
        w_proj=bf(ple_w_proj))
    ffn_w = [dict(stacked), dict(stacked)]
    ffn_w[0].update(g_kv=row(kv_g), w_k=bf(kv_w_k), w_v=bf(kv_w_v),
                    g_k2=row(jnp.tile(kv_g_k, 2)), g_q_in=row(g_mix[1]), w_q=bf(at_w_q[0]))
    ffn_w[1].update(wo=bf(at_w_o[0]))
    gq2 = row(jnp.tile(at_g_q[0], 2))
    sinks = at_sinks[0]

    W = WINDOW
    qi = np.arange(W)[:, None]
    bucket_p = jnp.asarray(_t5_buckets(qi + W - np.arange(2 * W)[None, :]))
    bucket_s = jnp.asarray(_t5_buckets(np.arange(DEC_SEQ)[:, None] + W
                                       - np.arange(2 * W)[None, :]))

    xp = x_prompt.reshape(BATCH * SEQ, D)
    pp = p_prompt.reshape(2, BATCH * SEQ, PLE_DIM)
    h1, conv_p = _mixer_prompt(xp, mixer_w)
    h3, ffn_p0, k_p, v_p, q_p = _ffn_call(h1, pp, None, None, ffn_w[0], layer=0, prompt=True,
                                          with_oproj=False, with_kvq=True)
    o_p = _attn_prompt(q_p, k_p, v_p, bucket_p, rel_bias, sinks, gq2)
    y_p, ffn_p1 = _ffn_call(h3, pp, None, o_p, ffn_w[1], layer=1, prompt=True,
                            with_oproj=True, with_kvq=False)

    xs = x_sample.reshape(DEC_BATCH * DEC_SEQ, D)
    ps = p_sample.reshape(2, DEC_BATCH * DEC_SEQ, PLE_DIM)
    g1, conv_s = _mixer_sample(xs, state_conv, mixer_w)
    g3, ffn_s0, k_s, v_s, q_s = _ffn_call(g1, ps, state_ffn, None, ffn_w[0], layer=0,
                                          prompt=False, with_oproj=False, with_kvq=True)
    o_s, k_out, v_out = _attn_sample(
        q_s.reshape(DEC_BATCH, DEC_SEQ, D), k_s.reshape(DEC_BATCH, DEC_SEQ, KD),
        v_s.reshape(DEC_BATCH, DEC_SEQ, KD), cache_k.reshape(DEC_BATCH, W, KD),
        cache_v.reshape(DEC_BATCH, W, KD), bucket_s, rel_bias, sinks, gq2)
    y_s, ffn_s1 = _ffn_call(g3, ps, state_ffn, o_s.reshape(DEC_BATCH * DEC_SEQ, D),
                            ffn_w[1], layer=1, prompt=False, with_oproj=True, with_kvq=False)

    kv_shape_p = (BATCH, W, N_KV_HEADS, HEAD_DIM)
    kv_shape_s = (DEC_BATCH, W, N_KV_HEADS, HEAD_DIM)
    return (y_p.reshape(BATCH, SEQ, D), y_s.reshape(DEC_BATCH, DEC_SEQ, D),
            conv_p[None], conv_s[None],
            jnp.stack([ffn_p0, ffn_p1]), jnp.stack([ffn_s0, ffn_s1]),
            k_p.reshape(BATCH, SEQ, KD)[:, SEQ - W:].reshape(kv_shape_p),
            k_out.reshape(kv_shape_s),
            v_p.reshape(BATCH, SEQ, KD)[:, SEQ - W:].reshape(kv_shape_p),
            v_out.reshape(kv_shape_s))
```

```python
import functools
import math

import jax
import jax.numpy as jnp
import numpy as np
from jax import lax
from jax.experimental import pallas as pl
from jax.experimental.pallas import tpu as pltpu

D_MODEL = 1024
BATCH = 8
SEQ = 2048
DEC_BATCH = 128
DEC_SEQ = 8
CONV_K = 31
FFN_CONV_K = 3
D_FF = 2816
N_HEADS = 16
N_KV_HEADS = 4
HEAD_DIM = 64
GROUP = N_HEADS // N_KV_HEADS
WINDOW = 128
N_BUCKETS = 32
MAX_DISTANCE = 128
PLE_DIM = 256
EPS = 1e-6
SCALE = HEAD_DIM ** -0.5
NEG = -1e30
KD = N_KV_HEADS * HEAD_DIM

LANES = 128
SUBLANES = 8
VMEM_LIMIT = 60 * 1024 * 1024

T_TILE = 512
S_TILE = 32
CONV_HALO = 32
FFN_HALO = 8
F_CHUNK = 2816
D_SLABS = D_MODEL // LANES
F_SLABS = D_FF // LANES
ATT_SEQS = 16
ATT_UNROLL = 8
ATT_BLOCKS = 2

_BF = jnp.bfloat16
_F32 = jnp.float32


def _bdot(a, w):
    return jnp.dot(a.astype(_BF), w, preferred_element_type=_F32)


def _rms(x, g):
    return x * lax.rsqrt(jnp.mean(x * x, axis=-1, keepdims=True) + EPS) * g


def _sigmoid(x):
    return 1.0 / (1.0 + jnp.exp(-x))


def _const_spec(shape):
    nd = len(shape)
    return pl.BlockSpec(shape, lambda *_: (0,) * nd, pipeline_mode=pl.Buffered(1))


def _layer_spec(shape, layer):
    nd = len(shape)
    return pl.BlockSpec((None,) + tuple(shape), lambda *_: (layer,) + (0,) * nd,
                        pipeline_mode=pl.Buffered(1))


def _params(n_grid):
    return pltpu.CompilerParams(dimension_semantics=("arbitrary",) * n_grid,
                                vmem_limit_bytes=VMEM_LIMIT)


def _mixer_tail(x, c, bdw, lng, lnb, w2, b2):
    c = c + bdw
    mu = jnp.mean(c, axis=-1, keepdims=True)
    cz = c - mu
    var = jnp.mean(cz * cz, axis=-1, keepdims=True)
    y = cz * lax.rsqrt(var + EPS) * lng + lnb
    s = y * _sigmoid(y)
    return x + _bdot(s, w2) + b2


def _mixer_prompt_kernel(x_ref, g_ref, w1_ref, b1_ref, wdw_ref, bdw_ref, lng_ref, lnb_ref,
                         w2_ref, b2_ref, h_ref, cs_ref, slab_ref, oslab_ref):
    T = T_TILE
    H = T // 2
    j = pl.program_id(1)

    @pl.when(j == 0)
    def _():
        slab_ref[:, 0:CONV_HALO, :] = jnp.zeros((D_SLABS, CONV_HALO, LANES), _F32)

    x = x_ref[...]
    u = _bdot(_rms(x, g_ref[...]), w1_ref[...]) + b1_ref[...]
    glu = u[:, :D_MODEL] * _sigmoid(u[:, D_MODEL:])
    for c in range(D_SLABS):
        slab_ref[c, CONV_HALO:CONV_HALO + T, :] = glu[:, c * LANES:(c + 1) * LANES]
    cs_ref[...] = glu[T - (CONV_K - 1):, :]

    base = CONV_HALO - (CONV_K - 1)
    for c in range(D_SLABS):
        wc = wdw_ref[:, c * LANES:(c + 1) * LANES]
        for p in range(2):
            acc = slab_ref[c, pl.ds(base + p, H, stride=2), :] * wc[0:1, :]
            for k in range(1, CONV_K):
                acc = acc + slab_ref[c, pl.ds(base + k + p, H, stride=2), :] * wc[k:k + 1, :]
            oslab_ref[c, pl.ds(p, H, stride=2), :] = acc
    conv = jnp.concatenate([oslab_ref[c] for c in range(D_SLABS)], axis=1)
    h_ref[...] = _mixer_tail(x, conv, bdw_ref[...], lng_ref[...], lnb_ref[...],
                             w2_ref[...], b2_ref[...])
    for c in range(D_SLABS):
        slab_ref[c, 0:CONV_HALO, :] = slab_ref[c, T:T + CONV_HALO, :]


def _mixer_sample_kernel(x_ref, st_ref, g_ref, w1_ref, b1_ref, wdw_ref, bdw_ref, lng_ref,
                         lnb_ref, w2_ref, b2_ref, h_ref, cs_ref, win_ref):
    S = S_TILE
    KH = CONV_K - 1
    x = x_ref[...]
    u = _bdot(_rms(x, g_ref[...]), w1_ref[...]) + b1_ref[...]
    glu = u[:, :D_MODEL] * _sigmoid(u[:, D_MODEL:])
    win_ref[:, 0:KH, :] = st_ref[...]
    win_ref[:, KH:KH + DEC_SEQ, :] = glu.reshape(S, DEC_SEQ, D_MODEL)
    cs_ref[...] = win_ref[:, DEC_SEQ:DEC_SEQ + KH, :]
    acc = win_ref[:, 0:DEC_SEQ, :] * wdw_ref[0:1, :]
    for k in range(1, CONV_K):
        acc = acc + win_ref[:, k:k + DEC_SEQ, :] * wdw_ref[k:k + 1, :]
    conv = acc.reshape(S * DEC_SEQ, D_MODEL)
    h_ref[...] = _mixer_tail(x, conv, bdw_ref[...], lng_ref[...], lnb_ref[...],
                             w2_ref[...], b2_ref[...])


def _mixer_weight_specs():
    D = D_MODEL
    return [_const_spec((1, D)), _const_spec((D, 2 * D)), _const_spec((1, 2 * D)),
            _const_spec((CONV_K, D)), _const_spec((1, D)), _const_spec((1, D)),
            _const_spec((1, D)), _const_spec((D, D)), _const_spec((1, D))]


def _mixer_prompt(x2d, wts):
    T = T_TILE
    nt = SEQ // T
    row = lambda b, j: (b * nt + j, 0)
    return pl.pallas_call(
        _mixer_prompt_kernel,
        grid=(BATCH, nt),
        in_specs=[pl.BlockSpec((T, D_MODEL), row)] + _mixer_weight_specs(),
        out_specs=[pl.BlockSpec((T, D_MODEL), row),
                   pl.BlockSpec((None, CONV_K - 1, D_MODEL), lambda b, j: (b, 0, 0))],
        out_shape=[jax.ShapeDtypeStruct((BATCH * SEQ, D_MODEL), _F32),
                   jax.ShapeDtypeStruct((BATCH, CONV_K - 1, D_MODEL), _F32)],
        scratch_shapes=[pltpu.VMEM((D_SLABS, CONV_HALO + T, LANES), _F32),
                        pltpu.VMEM((D_SLABS, T, LANES), _F32)],
        compiler_params=_params(2),
        name="mixer_prompt",
    )(x2d, *wts)


def _mixer_sample(x2d, state4d, wts):
    S = S_TILE
    R = S * DEC_SEQ
    return pl.pallas_call(
        _mixer_sample_kernel,
        grid=(DEC_BATCH // S,),
        in_specs=[pl.BlockSpec((R, D_MODEL), lambda i: (i, 0)),
                  pl.BlockSpec((None, S, CONV_K - 1, D_MODEL), lambda i: (0, i, 0, 0))]
                 + _mixer_weight_specs(),
        out_specs=[pl.BlockSpec((R, D_MODEL), lambda i: (i, 0)),
                   pl.BlockSpec((S, CONV_K - 1, D_MODEL), lambda i: (i, 0, 0))],
        out_shape=[jax.ShapeDtypeStruct((DEC_BATCH * DEC_SEQ, D_MODEL), _F32),
                   jax.ShapeDtypeStruct((DEC_BATCH, CONV_K - 1, D_MODEL), _F32)],
        scratch_shapes=[pltpu.VMEM((S, CONV_K - 1 + DEC_SEQ + 2, D_MODEL), _F32)],
        compiler_params=_params(1),
        name="mixer_sample",
    )(x2d, state4d, *wts)


def _ffn_kernel(*refs, prompt, with_oproj, with_kvq):
    it = iter(refs)
    h_ref = next(it)
    p_ref = next(it)
    st_ref = None if prompt else next(it)
    if with_oproj:
        o_ref = next(it)
        wo_ref = next(it)
    gf_ref, wup_ref, wdw_ref, bdw_ref, wdn_ref = (next(it) for _ in range(5))
    gp_ref, wg_ref, wp_ref = (next(it) for _ in range(3))
    if with_kvq:
        gkv_ref, wk_ref, wv_ref, gk_ref, gq_ref, wq_ref = (next(it) for _ in range(6))
    hout_ref = next(it)
    fs_ref = next(it)
    dup_kv = with_kvq and prompt
    if with_kvq:
        k_ref, v_ref, q_ref = (next(it) for _ in range(3))
    if dup_kv:
        kd_ref, vd_ref = next(it), next(it)
    if prompt:
        slab_ref, oslab_ref = next(it), next(it)
    else:
        win_ref = next(it)

    F = D_FF
    R = T_TILE if prompt else S_TILE * DEC_SEQ
    H = R // 2
    KH = FFN_CONV_K - 1

    if prompt:
        @pl.when(pl.program_id(1) == 0)
        def _():
            slab_ref[:, 0:FFN_HALO, :] = jnp.zeros((2 * F_SLABS, FFN_HALO, LANES), _F32)
    else:
        win_ref[:, FFN_HALO - KH:FFN_HALO, :] = st_ref[...]

    h = h_ref[...]
    if with_oproj:
        h = h + _bdot(o_ref[...], wo_ref[...])
    xn = _rms(h, gf_ref[...]).astype(_BF)

    n_chunks = F // F_CHUNK
    spc = F_CHUNK // LANES
    acc = None
    for ci in range(n_chunks):
        halves = []
        for half in range(2):
            c0 = half * F + ci * F_CHUNK
            uc = jnp.dot(xn, wup_ref[:, c0:c0 + F_CHUNK], preferred_element_type=_F32)
            w = wdw_ref[:, c0:c0 + F_CHUNK]
            b = bdw_ref[:, c0:c0 + F_CHUNK]
            if prompt:
                cols = []
                for s in range(spc):
                    sl = half * F_SLABS + ci * spc + s
                    slab_ref[sl, FFN_HALO:FFN_HALO + R, :] = uc[:, s * LANES:(s + 1) * LANES]
                    ws = w[:, s * LANES:(s + 1) * LANES]
                    ph = []
                    for p in range(2):
                        cv = b[:, s * LANES:(s + 1) * LANES]
                        for k in range(FFN_CONV_K):
                            cv = cv + (slab_ref[sl, pl.ds(FFN_HALO - KH + k + p, H, stride=2), :]
                                       * ws[k:k + 1, :])
                        ph.append(cv)
                    cols.append(jnp.concatenate(ph, axis=0))
                halves.append(jnp.concatenate(cols, axis=1))
            else:
                win_ref[:, FFN_HALO:FFN_HALO + DEC_SEQ, c0:c0 + F_CHUNK] = uc.reshape(
                    S_TILE, DEC_SEQ, F_CHUNK)
                cv = b
                for k in range(FFN_CONV_K):
                    o = FFN_HALO - KH + k
                    cv = cv + win_ref[:, o:o + DEC_SEQ, c0:c0 + F_CHUNK] * w[k:k + 1, :]
                halves.append(cv.reshape(R, F_CHUNK))
        gate, val = halves
        act = (gate * _sigmoid(gate) * val).astype(_BF)
        part = jnp.dot(act, wdn_ref[ci * F_CHUNK:(ci + 1) * F_CHUNK, :],
                       preferred_element_type=_F32)
        acc = part if acc is None else acc + part

    if prompt:
        for c in range(D_SLABS):
            oslab_ref[c, pl.ds(0, H, stride=2), :] = acc[:H, c * LANES:(c + 1) * LANES]
            oslab_ref[c, pl.ds(1, H, stride=2), :] = acc[H:, c * LANES:(c + 1) * LANES]
        f = jnp.concatenate([oslab_ref[c] for c in range(D_SLABS)], axis=1)
        fs_ref[...] = jnp.concatenate(
            [slab_ref[s, FFN_HALO + R - SUBLANES:FFN_HALO + R, :] for s in range(2 * F_SLABS)],
            axis=1)[SUBLANES - KH:, :]
        for s in range(2 * F_SLABS):
            slab_ref[s, 0:FFN_HALO, :] = slab_ref[s, R:R + FFN_HALO, :]
    else:
        f = acc
        fs_ref[...] = win_ref[:, FFN_HALO + DEC_SEQ - KH:FFN_HALO + DEC_SEQ, :]

    h2 = h + f
    gate = _sigmoid(_bdot(_rms(h2, gp_ref[...]), wg_ref[...]))
    h3 = h2 + gate * _bdot(p_ref[...], wp_ref[...])
    hout_ref[...] = h3

    if with_kvq:
        hk = _rms(h3, gkv_ref[...]).astype(_BF)
        kk = jnp.dot(hk, wk_ref[...], preferred_element_type=_F32)
        vv = jnp.dot(hk, wv_ref[...], preferred_element_type=_F32)
        low = lax.broadcasted_iota(jnp.int32, (R, LANES), 1) < HEAD_DIM
        kn = []
        for g in range(N_KV_HEADS):
            kg = kk[:, g * LANES:(g + 1) * LANES]
            r = lax.rsqrt(jnp.mean(kg * kg, axis=-1, keepdims=True) + EPS)
            kn.append(kg * r * gk_ref[...])
        k_ref[...] = jnp.concatenate(
            [jnp.where(low, kn[2 * c], kn[2 * c + 1]) for c in range(KD // LANES)], axis=1)
        v_ref[...] = jnp.concatenate(
            [jnp.where(low, vv[:, 2 * c * LANES:(2 * c + 1) * LANES],
                       vv[:, (2 * c + 1) * LANES:(2 * c + 2) * LANES])
             for c in range(KD // LANES)], axis=1)
        if dup_kv:
            kd_ref[...] = jnp.concatenate(kn, axis=1).astype(_BF)
            vd_ref[...] = vv.astype(_BF)
        q_ref[...] = _bdot(_rms(h3, gq_ref[...]), wq_ref[...])


def _ffn_call(h2d, p3d, state4d, o2d, w, *, layer, prompt, with_oproj, with_kvq):
    D, F = D_MODEL, D_FF
    if prompt:
        R = T_TILE
        nt = SEQ // R
        grid = (BATCH, nt)
        row = lambda b, j: (b * nt + j, 0)
        p_spec = pl.BlockSpec((None, R, PLE_DIM), lambda b, j: (layer, b * nt + j, 0))
        n_rows, n_seq = BATCH * SEQ, BATCH
        fs_spec = pl.BlockSpec((None, FFN_CONV_K - 1, 2 * F), lambda b, j: (b, 0, 0))
        scratch = [pltpu.VMEM((2 * F_SLABS, FFN_HALO + R, LANES), _F32),
                   pltpu.VMEM((D_SLABS, R, LANES), _F32)]
    else:
        R = S_TILE * DEC_SEQ
        grid = (DEC_BATCH // S_TILE,)
        row = lambda i: (i, 0)
        p_spec = pl.BlockSpec((None, R, PLE_DIM), lambda i: (layer, i, 0))
        n_rows, n_seq = DEC_BATCH * DEC_SEQ, DEC_BATCH
        fs_spec = pl.BlockSpec((S_TILE, FFN_CONV_K - 1, 2 * F), lambda i: (i, 0, 0))
        scratch = [pltpu.VMEM((S_TILE, FFN_HALO + DEC_SEQ, 2 * F), _F32)]

    args = [h2d, p3d]
    in_specs = [pl.BlockSpec((R, D), row), p_spec]
    if not prompt:
        args.append(state4d)
        in_specs.append(pl.BlockSpec((None, S_TILE, FFN_CONV_K - 1, 2 * F),
                                     lambda i: (layer, i, 0, 0)))
    if with_oproj:
        args += [o2d, w["wo"]]
        in_specs += [pl.BlockSpec((R, D), row), _const_spec((D, D))]
    args += [w["g_ffn"], w["w_up"], w["w_dw"], w["b_dw"], w["w_down"],
             w["g_ple"], w["w_gate"], w["w_proj"]]
    in_specs += [_layer_spec((1, D), layer), _layer_spec((D, 2 * F), layer),
                 _layer_spec((FFN_CONV_K, 2 * F), layer), _layer_spec((1, 2 * F), layer),
                 _layer_spec((F, D), layer), _layer_spec((1, D), layer),
                 _layer_spec((D, D), layer), _layer_spec((PLE_DIM, D), layer)]
    out_specs = [pl.BlockSpec((R, D), row), fs_spec]
    out_shape = [jax.ShapeDtypeStruct((n_rows, D), _F32),
                 jax.ShapeDtypeStruct((n_seq, FFN_CONV_K - 1, 2 * F), _F32)]
    if with_kvq:
        args += [w["g_kv"], w["w_k"], w["w_v"], w["g_k2"], w["g_q_in"], w["w_q"]]
        in_specs += [_const_spec((1, D)), _const_spec((D, 2 * KD)), _const_spec((D, 2 * KD)),
                     _const_spec((1, LANES)), _const_spec((1, D)), _const_spec((D, D))]
        out_specs += [pl.BlockSpec((R, KD), row), pl.BlockSpec((R, KD), row),
                      pl.BlockSpec((R, D), row)]
        out_shape += [jax.ShapeDtypeStruct((n_rows, KD), _F32),
                      jax.ShapeDtypeStruct((n_rows, KD), _F32),
                      jax.ShapeDtypeStruct((n_rows, D), _F32)]
        if prompt:
            out_specs += [pl.BlockSpec((R, 2 * KD), row), pl.BlockSpec((R, 2 * KD), row)]
            out_shape += [jax.ShapeDtypeStruct((n_rows, 2 * KD), _BF),
                          jax.ShapeDtypeStruct((n_rows, 2 * KD), _BF)]
    kern = functools.partial(_ffn_kernel, prompt=prompt, with_oproj=with_oproj,
                             with_kvq=with_kvq)
    name = "ffn_%s_%d" % ("prompt" if prompt else "sample", layer)
    return pl.pallas_call(
        kern, grid=grid, in_specs=in_specs, out_specs=out_specs, out_shape=out_shape,
        scratch_shapes=scratch, compiler_params=_params(len(grid)), name=name,
    )(*args)


def _t5_buckets(dist):
    max_exact = N_BUCKETS // 2
    d = np.maximum(dist, 0)
    df = np.maximum(d, 1).astype(np.float32)
    large = max_exact + (np.log(df / np.float32(max_exact))
                         / np.float32(math.log(MAX_DISTANCE / max_exact))
                         * np.float32(N_BUCKETS - max_exact)).astype(np.int32)
    large = np.minimum(large, N_BUCKETS - 1)
    return np.where(d < max_exact, d, large).astype(np.int32)


LOG2E = math.log2(math.e)
NEG2 = NEG * LOG2E


def _low_lanes(rows):
    return lax.broadcasted_iota(jnp.int32, (rows, LANES), 1) < HEAD_DIM


def _bias_rows(revb_ref, table_ref, hd, nq):
    revb = revb_ref[...]
    row = jnp.zeros(revb.shape, _F32)
    for bk in range(N_BUCKETS):
        row = jnp.where(revb == bk, table_ref[bk, hd] * LOG2E, row)
    base = jnp.concatenate([jnp.broadcast_to(row, (nq, LANES)),
                            jnp.full((nq, LANES), NEG2, _F32)], axis=1)
    return pltpu.roll(base, 1, 1, stride=1, stride_axis=0)


def _norm_heads(q, seg, gqs):
    n = q.shape[0]
    nc = D_MODEL // LANES
    sq = jnp.concatenate([q[:, c * LANES:(c + 1) * LANES] for c in range(nc)], axis=0)
    ss = jnp.dot((sq * sq).astype(_BF), seg, preferred_element_type=_F32)
    r = lax.rsqrt(ss * (1.0 / HEAD_DIM) + EPS)
    return [q[:, c * LANES:(c + 1) * LANES] * r[c * n:(c + 1) * n, :] * gqs for c in range(nc)]


def _sink_softmax2(logits2, sink2):
    m = jnp.maximum(jnp.max(logits2, axis=-1, keepdims=True), sink2)
    e = jnp.exp2(logits2 - m)
    denom = jnp.sum(e, axis=-1, keepdims=True) + jnp.exp2(sink2 - m)
    return e * (1.0 / denom)


def _dot_t(a, b):
    return lax.dot_general(a, b, (((1,), (1,)), ((), ())), preferred_element_type=_F32)


def _attend_blocks(blocks, sinks_ref, seg, gqs):
    units = []
    for (q, kd, vd, bias_of) in blocks:
        nq = q.shape[0]
        low = _low_lanes(nq)
        qn = _norm_heads(q, seg, gqs)
        for g in range(N_KV_HEADS):
            qs = jnp.concatenate(
                [jnp.where(low if hl % 2 == 0 else jnp.logical_not(low), qn[2 * g + hl // 2], 0.0)
                 for hl in range(GROUP)], axis=0).astype(_BF)
            units.append((g, nq, bias_of, _dot_t(qs, kd[:, g * LANES:(g + 1) * LANES]),
                          vd[:, g * LANES:(g + 1) * LANES]))
    probs = []
    for (g, nq, bias_of, s, _) in units:
        probs.append(jnp.concatenate(
            [_sink_softmax2(s[hl * nq:(hl + 1) * nq, :] + bias_of(g * GROUP + hl),
                            sinks_ref[g * GROUP + hl] * LOG2E) for hl in range(GROUP)],
            axis=0).astype(_BF))
    res, outs = [], []
    for (g, nq, _, _, vg), p in zip(units, probs):
        low_k = _low_lanes(vg.shape[0])
        zero = jnp.zeros_like(vg)
        vlo, vhi = jnp.where(low_k, vg, zero), jnp.where(low_k, zero, vg)
        for pair in range(2):
            r0 = 2 * pair * nq
            outs.append(jnp.dot(p[r0:r0 + nq], vlo, preferred_element_type=_F32)
                        + jnp.dot(p[r0 + nq:r0 + 2 * nq], vhi, preferred_element_type=_F32))
        if g == N_KV_HEADS - 1:
            res.append(jnp.concatenate(outs, axis=1))
            outs = []
    return res


def _attn_prompt_kernel(q_ref, kp_ref, kc_ref, vp_ref, vc_ref, revb_ref, table_ref,
                        sinks_ref, gq_ref, seg_ref, o_ref, bias_ref):
    W = WINDOW
    i = pl.program_id(1)

    @pl.when(jnp.logical_and(pl.program_id(0) == 0, i == 0))
    def _():
        first_half = lax.broadcasted_iota(jnp.int32, (W, 2 * W), 1) < W
        for hd in range(N_HEADS):
            rows = _bias_rows(revb_ref, table_ref, hd, W)
            bias_ref[0, hd] = rows
            bias_ref[1, hd] = jnp.where(first_half, NEG2, rows)

    gqs = gq_ref[...] * (SCALE * LOG2E)
    kp, kc = kp_ref[...], kc_ref[...]
    vp, vc = vp_ref[...], vc_ref[...]
    first = jnp.where(i == 0, 1, 0)
    blocks = []
    for blk in range(ATT_BLOCKS):
        if blk == 0:
            kd = jnp.concatenate([kp, kc[0:W]], axis=0)
            vd = jnp.concatenate([vp, vc[0:W]], axis=0)
            bias_of = lambda hd: bias_ref[first, hd]
        else:
            kd, vd = kc[(blk - 1) * W:(blk + 1) * W], vc[(blk - 1) * W:(blk + 1) * W]
            bias_of = lambda hd: bias_ref[0, hd]
        blocks.append((q_ref[blk * W:(blk + 1) * W, :], kd, vd, bias_of))
    for blk, o in enumerate(_attend_blocks(blocks, sinks_ref, seg_ref[...], gqs)):
        o_ref[blk * W:(blk + 1) * W, :] = o.astype(o_ref.dtype)


def _attn_prompt(q2d, kd2d, vd2d, revb, table, sinks, gq2, seg):
    W = WINDOW
    R = ATT_BLOCKS * W
    ns = SEQ // R
    cur = lambda b, i: (b * ns + i, 0)
    prev = lambda b, i: (b * (SEQ // W) + jnp.maximum(ATT_BLOCKS * i - 1, 0), 0)
    smem = pl.BlockSpec(memory_space=pltpu.SMEM)
    return pl.pallas_call(
        _attn_prompt_kernel,
        grid=(BATCH, ns),
        in_specs=[pl.BlockSpec((R, D_MODEL), cur),
                  pl.BlockSpec((W, 2 * KD), prev), pl.BlockSpec((R, 2 * KD), cur),
                  pl.BlockSpec((W, 2 * KD), prev), pl.BlockSpec((R, 2 * KD), cur),
                  _const_spec((1, LANES)), smem, smem, _const_spec((1, LANES)),
                  _const_spec((LANES, LANES))],
        out_specs=pl.BlockSpec((R, D_MODEL), cur),
        out_shape=jax.ShapeDtypeStruct((BATCH * SEQ, D_MODEL), _BF),
        scratch_shapes=[pltpu.VMEM((2, N_HEADS, W, 2 * W), _F32)],
        compiler_params=_params(2),
        name="attn_prompt",
    )(q2d, kd2d, kd2d, vd2d, vd2d, revb, table, sinks, gq2, seg)


def _step_logits(q, kb, bias_all, gqs):
    n = DEC_SEQ
    low = _low_lanes(n)
    zero = jnp.zeros((n, LANES), _F32)
    rows = []
    for hd in range(N_HEADS):
        g = hd // GROUP
        x = q[:, (hd // 2) * LANES:(hd // 2 + 1) * LANES]
        if hd % 2 != g % 2:
            x = pltpu.roll(x, HEAD_DIM, 1)
        x = jnp.where(low if g % 2 == 0 else jnp.logical_not(low), x, 0.0)
        ss = jnp.sum(x * x, axis=-1, keepdims=True) * (1.0 / HEAD_DIM)
        x = x * lax.rsqrt(ss + EPS) * gqs
        rows.append(jnp.concatenate([x, zero] if g // 2 == 0 else [zero, x], axis=1))
    q_all = jnp.concatenate(rows, axis=0).astype(_BF)
    return _dot_t(q_all, kb) + bias_all


def _split_heads(o_all):
    n = DEC_SEQ
    low = _low_lanes(n)
    outs = []
    for c in range(N_HEADS // 2):
        halves = []
        for hf in range(2):
            hd = 2 * c + hf
            g = hd // GROUP
            y = o_all[hd * n:(hd + 1) * n, (g // 2) * LANES:(g // 2 + 1) * LANES]
            halves.append(y if g % 2 == hf else pltpu.roll(y, HEAD_DIM, 1))
        outs.append(jnp.where(low, halves[0], halves[1]))
    return jnp.concatenate(outs, axis=1)


def _attn_sample_kernel(q_ref, kn_ref, vn_ref, ck_ref, cv_ref, revb_ref, table_ref,
                        sinks_ref, gq_ref, o_ref, ko_ref, vo_ref, bias_ref, kbuf_ref, vbuf_ref):
    W = WINDOW
    L = 2 * W
    U = ATT_UNROLL

    @pl.when(pl.program_id(0) == 0)
    def _():
        for hd in range(N_HEADS):
            bias_ref[hd * DEC_SEQ:(hd + 1) * DEC_SEQ, :] = _bias_rows(revb_ref, table_ref, hd,
                                                                      DEC_SEQ)
        kbuf_ref[...] = jnp.zeros((U, L, KD), _F32)
        vbuf_ref[...] = jnp.zeros((U, L, KD), _F32)

    gqs = gq_ref[...] * (SCALE * LOG2E)
    bias_all = bias_ref[...]
    sink_col = jnp.concatenate(
        [jnp.full((DEC_SEQ, 1), sinks_ref[hd] * LOG2E, _F32) for hd in range(N_HEADS)], axis=0)

    def body(it, carry):
        logits = []
        for u in range(U):
            s = it * U + u
            kbuf_ref[u, 0:W, :] = ck_ref[s]
            vbuf_ref[u, 0:W, :] = cv_ref[s]
            kbuf_ref[u, W:W + DEC_SEQ, :] = kn_ref[s]
            vbuf_ref[u, W:W + DEC_SEQ, :] = vn_ref[s]
            ko_ref[s] = kbuf_ref[u, DEC_SEQ:DEC_SEQ + W, :]
            vo_ref[s] = vbuf_ref[u, DEC_SEQ:DEC_SEQ + W, :]
            logits.append(_step_logits(q_ref[s], kbuf_ref[u].astype(_BF), bias_all, gqs))
        probs = [_sink_softmax2(l, sink_col).astype(_BF) for l in logits]
        for u in range(U):
            o_all = jnp.dot(probs[u], vbuf_ref[u].astype(_BF), preferred_element_type=_F32)
            o_ref[it * U + u] = _split_heads(o_all)
        return carry

    lax.fori_loop(0, ATT_SEQS // U, body, 0)


def _attn_sample(q3d, kn3d, vn3d, ck, cv, revb, table, sinks, gq2):
    W = WINDOW
    A = ATT_SEQS
    blk = lambda r, c: pl.BlockSpec((A, r, c), lambda i: (i, 0, 0))
    smem = pl.BlockSpec(memory_space=pltpu.SMEM)
    return pl.pallas_call(
        _attn_sample_kernel,
        grid=(DEC_BATCH // A,),
        in_specs=[blk(DEC_SEQ, D_MODEL), blk(DEC_SEQ, KD), blk(DEC_SEQ, KD),
                  blk(W, KD), blk(W, KD),
                  _const_spec((1, LANES)), smem, smem, _const_spec((1, LANES))],
        out_specs=[blk(DEC_SEQ, D_MODEL), blk(W, KD), blk(W, KD)],
        out_shape=[jax.ShapeDtypeStruct((DEC_BATCH, DEC_SEQ, D_MODEL), _F32),
                   jax.ShapeDtypeStruct((DEC_BATCH, W, KD), _F32),
                   jax.ShapeDtypeStruct((DEC_BATCH, W, KD), _F32)],
        scratch_shapes=[pltpu.VMEM((N_HEADS * DEC_SEQ, 2 * W), _F32),
                        pltpu.VMEM((ATT_UNROLL, 2 * W, KD), _F32),
                        pltpu.VMEM((ATT_UNROLL, 2 * W, KD), _F32)],
        compiler_params=_params(1),
        name="attn_sample",
    )(q3d, kn3d, vn3d, ck, cv, revb, table, sinks, gq2)


def kernel(x_prompt, x_sample, state_conv, state_ffn, cache_k, cache_v, p_prompt, p_sample, g_mix, cm_w_pw1, cm_b_pw1, cm_w_dw, cm_b_dw, cm_ln_g, cm_ln_b, cm_w_pw2, cm_b_pw2, at_w_q, at_g_q, at_sinks, at_w_o, kv_g, kv_w_k, kv_w_v, kv_g_k, rel_bias, g_ffn, ffn_w_up, ffn_w_dw, ffn_b_dw, ffn_w_down, g_ple, ple_w_gate, ple_w_proj):
    D, F = D_MODEL, D_FF
    row = lambda a: a.reshape(1, -1)
    bf = lambda a: a.astype(_BF)

    mixer_w = [row(g_mix[0]), bf(cm_w_pw1[0]), row(cm_b_pw1[0]), cm_w_dw[0], row(cm_b_dw[0]),
               row(cm_ln_g[0]), row(cm_ln_b[0]), bf(cm_w_pw2[0]), row(cm_b_pw2[0])]
    stacked = dict(
        g_ffn=g_ffn[:, None, :], w_up=bf(ffn_w_up), w_dw=ffn_w_dw, b_dw=ffn_b_dw[:, None, :],
        w_down=bf(ffn_w_down), g_ple=g_ple[:, None, :], w_gate=bf(ple_w_gate),
        w_proj=bf(ple_w_proj))
    ffn_w = [dict(stacked), dict(stacked)]
    dup = lambda w: bf(jnp.tile(w.reshape(D, N_KV_HEADS, 1, HEAD_DIM), (1, 1, 2, 1))
                       .reshape(D, 2 * KD))
    ffn_w[0].update(g_kv=row(kv_g), w_k=dup(kv_w_k), w_v=dup(kv_w_v),
                    g_k2=row(jnp.tile(kv_g_k, 2)), g_q_in=row(g_mix[1]), w_q=bf(at_w_q[0]))
    ffn_w[1].update(wo=bf(at_w_o[0]))
    gq2 = row(jnp.tile(at_g_q[0], 2))
    sinks = at_sinks[0]

    W = WINDOW
    revb = jnp.asarray(_t5_buckets(W - 1 - np.arange(W))[None, :])
    half = np.arange(LANES) // HEAD_DIM
    seg = jnp.asarray(half[:, None] == half[None, :], dtype=_BF)

    xp = x_prompt.reshape(BATCH * SEQ, D)
    pp = p_prompt.reshape(2, BATCH * SEQ, PLE_DIM)
    h1, conv_p = _mixer_prompt(xp, mixer_w)
    h3, ffn_p0, k_p, v_p, q_p, kd_p, vd_p = _ffn_call(
        h1, pp, None, None, ffn_w[0], layer=0, prompt=True, with_oproj=False, with_kvq=True)
    o_p = _attn_prompt(q_p, kd_p, vd_p, revb, rel_bias, sinks, gq2, seg)
    y_p, ffn_p1 = _ffn_call(h3, pp, None, o_p, ffn_w[1], layer=1, prompt=True,
                            with_oproj=True, with_kvq=False)

    xs = x_sample.reshape(DEC_BATCH * DEC_SEQ, D)
    ps = p_sample.reshape(2, DEC_BATCH * DEC_SEQ, PLE_DIM)
    g1, conv_s = _mixer_sample(xs, state_conv, mixer_w)
    g3, ffn_s0, k_s, v_s, q_s = _ffn_call(g1, ps, state_ffn, None, ffn_w[0], layer=0,
                                          prompt=False, with_oproj=False, with_kvq=True)
    o_s, k_out, v_out = _attn_sample(
        q_s.reshape(DEC_BATCH, DEC_SEQ, D), k_s.reshape(DEC_BATCH, DEC_SEQ, KD),
        v_s.reshape(DEC_BATCH, DEC_SEQ, KD), cache_k.reshape(DEC_BATCH, W, KD),
        cache_v.reshape(DEC_BATCH, W, KD), revb, rel_bias, sinks, gq2)
    y_s, ffn_s1 = _ffn_call(g3, ps, state_ffn, o_s.reshape(DEC_BATCH * DEC_SEQ, D),
                            ffn_w[1], layer=1, prompt=False, with_oproj=True, with_kvq=False)

    kv_shape_p = (BATCH, W, N_KV_HEADS, HEAD_DIM)
    kv_shape_s = (DEC_BATCH, W, N_KV_HEADS, HEAD_DIM)
    return (y_p.reshape(BATCH, SEQ, D), y_s.reshape(DEC_BATCH, DEC_SEQ, D),
            conv_p[None], conv_s[None],
            jnp.stack([ffn_p0, ffn_p1]), jnp.stack([ffn_s0, ffn_s1]),
            k_p.reshape(BATCH, SEQ, KD)[:, SEQ - W:].reshape(kv_shape_p),
            k_out.reshape(kv_shape_s),
            v_p.reshape(BATCH, SEQ, KD)[:, SEQ - W:].reshape(kv_shape_p),
            v_out.reshape(kv_shape_s))
```

```python
import functools
import math

import jax
import jax.numpy as jnp
import numpy as np
from jax import lax
from jax.experimental import pallas as pl
from jax.experimental.pallas import tpu as pltpu

D_MODEL = 1024
BATCH = 8
SEQ = 2048
DEC_BATCH = 128
DEC_SEQ = 8
CONV_K = 31
FFN_CONV_K = 3
D_FF = 2816
N_HEADS = 16
N_KV_HEADS = 4
HEAD_DIM = 64
GROUP = N_HEADS // N_KV_HEADS
WINDOW = 128
N_BUCKETS = 32
MAX_DISTANCE = 128
PLE_DIM = 256
EPS = 1e-6
SCALE = HEAD_DIM ** -0.5
NEG = -1e30
KD = N_KV_HEADS * HEAD_DIM

LANES = 128
SUBLANES = 8
VMEM_LIMIT = 60 * 1024 * 1024

T_TILE = 512
S_TILE = 32
CONV_HALO = 32
FFN_HALO = 8
F_CHUNK = 2816
D_SLABS = D_MODEL // LANES
F_SLABS = D_FF // LANES
ATT_SEQS = 16
ATT_UNROLL = 8
ATT_BLOCKS = 2

_BF = jnp.bfloat16
_F32 = jnp.float32


def _bdot(a, w):
    return jnp.dot(a.astype(_BF), w, preferred_element_type=_F32)


def _rms(x, g):
    return x * lax.rsqrt(jnp.mean(x * x, axis=-1, keepdims=True) + EPS) * g


def _sigmoid(x):
    return 1.0 / (1.0 + jnp.exp(-x))


def _const_spec(shape):
    nd = len(shape)
    return pl.BlockSpec(shape, lambda *_: (0,) * nd, pipeline_mode=pl.Buffered(1))


def _layer_spec(shape, layer):
    nd = len(shape)
    return pl.BlockSpec((None,) + tuple(shape), lambda *_: (layer,) + (0,) * nd,
                        pipeline_mode=pl.Buffered(1))


def _params(n_grid):
    return pltpu.CompilerParams(dimension_semantics=("arbitrary",) * n_grid,
                                vmem_limit_bytes=VMEM_LIMIT)


def _mixer_tail(x, c, bdw, lng, lnb, w2, b2):
    c = c + bdw
    mu = jnp.mean(c, axis=-1, keepdims=True)
    cz = c - mu
    var = jnp.mean(cz * cz, axis=-1, keepdims=True)
    y = cz * lax.rsqrt(var + EPS) * lng + lnb
    s = y * _sigmoid(y)
    return x + _bdot(s, w2) + b2


def _mixer_prompt_kernel(x_ref, g_ref, w1_ref, b1_ref, wdw_ref, bdw_ref, lng_ref, lnb_ref,
                         w2_ref, b2_ref, *rest):
    n_cast = len(_CAST_ROWS)
    cast_in, rest = rest[:n_cast], rest[n_cast:]
    h_ref, cs_ref = rest[0], rest[1]
    cast_out, (slab_ref, oslab_ref) = rest[2:2 + n_cast], rest[2 + n_cast:]
    T = T_TILE
    H = T // 2
    j = pl.program_id(1)

    for src, dst in zip(cast_in, cast_out):
        dst[...] = src[...].astype(_BF)

    @pl.when(j == 0)
    def _():
        slab_ref[:, 0:CONV_HALO, :] = jnp.zeros((D_SLABS, CONV_HALO, LANES), _F32)

    x = x_ref[...]
    u = _bdot(_rms(x, g_ref[...]), w1_ref[...]) + b1_ref[...]
    glu = u[:, :D_MODEL] * _sigmoid(u[:, D_MODEL:])
    for c in range(D_SLABS):
        slab_ref[c, CONV_HALO:CONV_HALO + T, :] = glu[:, c * LANES:(c + 1) * LANES]
    cs_ref[...] = glu[T - (CONV_K - 1):, :]

    base = CONV_HALO - (CONV_K - 1)
    for c in range(D_SLABS):
        wc = wdw_ref[:, c * LANES:(c + 1) * LANES]
        for p in range(2):
            acc = slab_ref[c, pl.ds(base + p, H, stride=2), :] * wc[0:1, :]
            for k in range(1, CONV_K):
                acc = acc + slab_ref[c, pl.ds(base + k + p, H, stride=2), :] * wc[k:k + 1, :]
            oslab_ref[c, pl.ds(p, H, stride=2), :] = acc
    conv = jnp.concatenate([oslab_ref[c] for c in range(D_SLABS)], axis=1)
    h_ref[...] = _mixer_tail(x, conv, bdw_ref[...], lng_ref[...], lnb_ref[...],
                             w2_ref[...], b2_ref[...])
    for c in range(D_SLABS):
        slab_ref[c, 0:CONV_HALO, :] = slab_ref[c, T:T + CONV_HALO, :]


def _mixer_sample_kernel(x_ref, st_ref, g_ref, w1_ref, b1_ref, wdw_ref, bdw_ref, lng_ref,
                         lnb_ref, w2_ref, b2_ref, h_ref, cs_ref):
    S = S_TILE
    KH = CONV_K - 1
    x = x_ref[...].reshape(DEC_SEQ * S, D_MODEL)
    u = _bdot(_rms(x, g_ref[...]), w1_ref[...]) + b1_ref[...]
    glu = u[:, :D_MODEL] * _sigmoid(u[:, D_MODEL:])
    cs_ref[0:KH - DEC_SEQ] = st_ref[DEC_SEQ:KH]
    cs_ref[KH - DEC_SEQ:KH] = glu.reshape(DEC_SEQ, S, D_MODEL)

    def plane(j):
        return st_ref[j] if j < KH else glu[(j - KH) * S:(j - KH + 1) * S, :]

    outs = []
    for t in range(DEC_SEQ):
        acc = plane(t) * wdw_ref[0:1, :]
        for k in range(1, CONV_K):
            acc = acc + plane(t + k) * wdw_ref[k:k + 1, :]
        outs.append(acc)
    conv = jnp.concatenate(outs, axis=0)
    h = _mixer_tail(x, conv, bdw_ref[...], lng_ref[...], lnb_ref[...], w2_ref[...], b2_ref[...])
    h_ref[...] = h.reshape(DEC_SEQ, S, D_MODEL)


def _mixer_weight_specs():
    D = D_MODEL
    return [_const_spec((1, D)), _const_spec((D, 2 * D)), _const_spec((1, 2 * D)),
            _const_spec((CONV_K, D)), _const_spec((1, D)), _const_spec((1, D)),
            _const_spec((1, D)), _const_spec((D, D)), _const_spec((1, D))]


_CAST_ROWS = (2 * D_MODEL, 2 * D_FF, 2 * D_MODEL, 2 * PLE_DIM)


def _mixer_prompt(x2d, wts, cast_srcs):
    T = T_TILE
    nt = SEQ // T
    steps = BATCH * nt
    row = lambda b, j: (b * nt + j, 0)
    cast_specs, cast_shapes = [], []
    for src, rows in zip(cast_srcs, _CAST_ROWS):
        assert src.shape[0] == rows and rows % steps == 0
        cast_specs.append(pl.BlockSpec((rows // steps, src.shape[1]), row))
        cast_shapes.append(jax.ShapeDtypeStruct(src.shape, _BF))
    return pl.pallas_call(
        _mixer_prompt_kernel,
        grid=(BATCH, nt),
        in_specs=[pl.BlockSpec((T, D_MODEL), row)] + _mixer_weight_specs() + cast_specs,
        out_specs=[pl.BlockSpec((T, D_MODEL), row),
                   pl.BlockSpec((None, CONV_K - 1, D_MODEL), lambda b, j: (b, 0, 0))] + cast_specs,
        out_shape=[jax.ShapeDtypeStruct((BATCH * SEQ, D_MODEL), _F32),
                   jax.ShapeDtypeStruct((BATCH, CONV_K - 1, D_MODEL), _F32)] + cast_shapes,
        scratch_shapes=[pltpu.VMEM((D_SLABS, CONV_HALO + T, LANES), _F32),
                        pltpu.VMEM((D_SLABS, T, LANES), _F32)],
        compiler_params=_params(2),
        name="mixer_prompt",
    )(x2d, *wts, *cast_srcs)


def _mixer_sample(x_tm, state_tm, wts):
    S = S_TILE
    blk = lambda n: pl.BlockSpec((n, S, D_MODEL), lambda i: (0, i, 0))
    return pl.pallas_call(
        _mixer_sample_kernel,
        grid=(DEC_BATCH // S,),
        in_specs=[blk(DEC_SEQ), blk(CONV_K - 1)] + _mixer_weight_specs(),
        out_specs=[blk(DEC_SEQ), blk(CONV_K - 1)],
        out_shape=[jax.ShapeDtypeStruct((DEC_SEQ, DEC_BATCH, D_MODEL), _F32),
                   jax.ShapeDtypeStruct((CONV_K - 1, DEC_BATCH, D_MODEL), _F32)],
        compiler_params=_params(1),
        name="mixer_sample",
    )(x_tm, state_tm, *wts)


def _ffn_kernel(*refs, prompt, with_oproj, with_kvq):
    it = iter(refs)
    h_ref = next(it)
    p_ref = next(it)
    st_ref = None if prompt else next(it)
    if with_oproj:
        o_ref = next(it)
        wo_ref = next(it)
        fs0_ref = next(it)
    gf_ref, wup_ref, wdw_ref, bdw_ref, wdn_ref = (next(it) for _ in range(5))
    gp_ref, wg_ref, wp_ref = (next(it) for _ in range(3))
    if with_kvq:
        gkv_ref, wk_ref, wv_ref, gk_ref, gq_ref, wq_ref = (next(it) for _ in range(6))
    hout_ref = next(it)
    fs_ref = next(it)
    dup_kv = with_kvq and prompt
    if with_kvq:
        k_ref, v_ref, q_ref = (next(it) for _ in range(3))
    if dup_kv:
        kd_ref, vd_ref = next(it), next(it)
    if prompt:
        slab_ref, oslab_ref = next(it), next(it)
    else:
        win_ref = next(it)

    F = D_FF
    R = T_TILE if prompt else S_TILE * DEC_SEQ
    H = R // 2
    KH = FFN_CONV_K - 1

    if prompt:
        @pl.when(pl.program_id(1) == 0)
        def _():
            slab_ref[:, 0:FFN_HALO, :] = jnp.zeros((2 * F_SLABS, FFN_HALO, LANES), _F32)
    else:
        win_ref[:, FFN_HALO - KH:FFN_HALO, :] = st_ref[...]

    h = h_ref[...]
    if with_oproj:
        h = h + _bdot(o_ref[...], wo_ref[...])
    xn = _rms(h, gf_ref[...]).astype(_BF)

    n_chunks = F // F_CHUNK
    spc = F_CHUNK // LANES
    acc = None
    for ci in range(n_chunks):
        halves = []
        for half in range(2):
            c0 = half * F + ci * F_CHUNK
            uc = jnp.dot(xn, wup_ref[:, c0:c0 + F_CHUNK], preferred_element_type=_F32)
            w = wdw_ref[:, c0:c0 + F_CHUNK]
            b = bdw_ref[:, c0:c0 + F_CHUNK]
            if prompt:
                cols = []
                for s in range(spc):
                    sl = half * F_SLABS + ci * spc + s
                    slab_ref[sl, FFN_HALO:FFN_HALO + R, :] = uc[:, s * LANES:(s + 1) * LANES]
                    ws = w[:, s * LANES:(s + 1) * LANES]
                    ph = []
                    for p in range(2):
                        cv = b[:, s * LANES:(s + 1) * LANES]
                        for k in range(FFN_CONV_K):
                            cv = cv + (slab_ref[sl, pl.ds(FFN_HALO - KH + k + p, H, stride=2), :]
                                       * ws[k:k + 1, :])
                        ph.append(cv)
                    cols.append(jnp.concatenate(ph, axis=0))
                halves.append(jnp.concatenate(cols, axis=1))
            else:
                win_ref[:, FFN_HALO:FFN_HALO + DEC_SEQ, c0:c0 + F_CHUNK] = uc.reshape(
                    S_TILE, DEC_SEQ, F_CHUNK)
                cv = b
                for k in range(FFN_CONV_K):
                    o = FFN_HALO - KH + k
                    cv = cv + win_ref[:, o:o + DEC_SEQ, c0:c0 + F_CHUNK] * w[k:k + 1, :]
                halves.append(cv.reshape(R, F_CHUNK))
        gate, val = halves
        act = (gate * _sigmoid(gate) * val).astype(_BF)
        part = jnp.dot(act, wdn_ref[ci * F_CHUNK:(ci + 1) * F_CHUNK, :],
                       preferred_element_type=_F32)
        acc = part if acc is None else acc + part

    if prompt:
        for c in range(D_SLABS):
            oslab_ref[c, pl.ds(0, H, stride=2), :] = acc[:H, c * LANES:(c + 1) * LANES]
            oslab_ref[c, pl.ds(1, H, stride=2), :] = acc[H:, c * LANES:(c + 1) * LANES]
        f = jnp.concatenate([oslab_ref[c] for c in range(D_SLABS)], axis=1)
        new_state = jnp.concatenate(
            [slab_ref[s, FFN_HALO + R - SUBLANES:FFN_HALO + R, :] for s in range(2 * F_SLABS)],
            axis=1)[SUBLANES - KH:, :]
        for s in range(2 * F_SLABS):
            slab_ref[s, 0:FFN_HALO, :] = slab_ref[s, R:R + FFN_HALO, :]
    else:
        f = acc
        new_state = win_ref[:, FFN_HALO + DEC_SEQ - KH:FFN_HALO + DEC_SEQ, :]
    if with_oproj:
        fs_ref[0] = fs0_ref[...]
        fs_ref[1] = new_state
    else:
        fs_ref[...] = new_state

    h2 = h + f
    gate = _sigmoid(_bdot(_rms(h2, gp_ref[...]), wg_ref[...]))
    h3 = h2 + gate * _bdot(p_ref[...], wp_ref[...])
    hout_ref[...] = h3

    if with_kvq:
        hk = _rms(h3, gkv_ref[...]).astype(_BF)
        kk = jnp.dot(hk, wk_ref[...], preferred_element_type=_F32)
        vv = jnp.dot(hk, wv_ref[...], preferred_element_type=_F32)
        v_ref[...] = vv
        low = lax.broadcasted_iota(jnp.int32, (R, LANES), 1) < HEAD_DIM

        def dup_heads(x):
            out = []
            for c in range(KD // LANES):
                xc = x[:, c * LANES:(c + 1) * LANES]
                xr = pltpu.roll(xc, HEAD_DIM, 1)
                out += [jnp.where(low, xc, xr), jnp.where(low, xr, xc)]
            return out

        kn = []
        for kg in dup_heads(kk):
            r = lax.rsqrt(jnp.mean(kg * kg, axis=-1, keepdims=True) + EPS)
            kn.append(kg * r * gk_ref[...])
        k_ref[...] = jnp.concatenate(
            [jnp.where(low, kn[2 * c], kn[2 * c + 1]) for c in range(KD // LANES)], axis=1)
        if dup_kv:
            kd_ref[...] = jnp.concatenate(kn, axis=1).astype(_BF)
            vd_ref[...] = jnp.concatenate(dup_heads(vv), axis=1).astype(_BF)
        q_ref[...] = _bdot(_rms(h3, gq_ref[...]), wq_ref[...])


def _ffn_call(h2d, p3d, state4d, o2d, w, *, layer, prompt, with_oproj, with_kvq,
              prev_state=None):
    D, F = D_MODEL, D_FF
    if prompt:
        R = T_TILE
        nt = SEQ // R
        grid = (BATCH, nt)
        row = lambda b, j: (b * nt + j, 0)
        p_spec = pl.BlockSpec((None, R, PLE_DIM), lambda b, j: (layer, b * nt + j, 0))
        n_rows, n_seq = BATCH * SEQ, BATCH
        fs_spec = pl.BlockSpec((None, FFN_CONV_K - 1, 2 * F), lambda b, j: (b, 0, 0))
        scratch = [pltpu.VMEM((2 * F_SLABS, FFN_HALO + R, LANES), _F32),
                   pltpu.VMEM((D_SLABS, R, LANES), _F32)]
    else:
        R = S_TILE * DEC_SEQ
        grid = (DEC_BATCH // S_TILE,)
        row = lambda i: (i, 0)
        p_spec = pl.BlockSpec((None, R, PLE_DIM), lambda i: (layer, i, 0))
        n_rows, n_seq = DEC_BATCH * DEC_SEQ, DEC_BATCH
        fs_spec = pl.BlockSpec((S_TILE, FFN_CONV_K - 1, 2 * F), lambda i: (i, 0, 0))
        scratch = [pltpu.VMEM((S_TILE, FFN_HALO + DEC_SEQ, 2 * F), _F32)]

    args = [h2d, p3d]
    in_specs = [pl.BlockSpec((R, D), row), p_spec]
    if not prompt:
        args.append(state4d)
        in_specs.append(pl.BlockSpec((None, S_TILE, FFN_CONV_K - 1, 2 * F),
                                     lambda i: (layer, i, 0, 0)))
    fs_shape = (n_seq, FFN_CONV_K - 1, 2 * F)
    if with_oproj:
        args += [o2d, w["wo"], prev_state]
        in_specs += [pl.BlockSpec((R, D), row), _const_spec((D, D)), fs_spec]
        blk, imap = fs_spec.block_shape, fs_spec.index_map
        fs_spec = pl.BlockSpec((2,) + tuple(blk), lambda *g: (0,) + tuple(imap(*g)))
        fs_shape = (2,) + fs_shape
    args += [w["g_ffn"], w["w_up"], w["w_dw"], w["b_dw"], w["w_down"],
             w["g_ple"], w["w_gate"], w["w_proj"]]
    in_specs += [_layer_spec((1, D), layer), _layer_spec((D, 2 * F), layer),
                 _layer_spec((FFN_CONV_K, 2 * F), layer), _layer_spec((1, 2 * F), layer),
                 _layer_spec((F, D), layer), _layer_spec((1, D), layer),
                 _layer_spec((D, D), layer), _layer_spec((PLE_DIM, D), layer)]
    out_specs = [pl.BlockSpec((R, D), row), fs_spec]
    out_shape = [jax.ShapeDtypeStruct((n_rows, D), _F32),
                 jax.ShapeDtypeStruct(fs_shape, _F32)]
    if with_kvq:
        args += [w["g_kv"], w["w_k"], w["w_v"], w["g_k2"], w["g_q_in"], w["w_q"]]
        in_specs += [_const_spec((1, D)), _const_spec((D, KD)), _const_spec((D, KD)),
                     _const_spec((1, LANES)), _const_spec((1, D)), _const_spec((D, D))]
        out_specs += [pl.BlockSpec((R, KD), row), pl.BlockSpec((R, KD), row),
                      pl.BlockSpec((R, D), row)]
        out_shape += [jax.ShapeDtypeStruct((n_rows, KD), _F32),
                      jax.ShapeDtypeStruct((n_rows, KD), _F32),
                      jax.ShapeDtypeStruct((n_rows, D), _F32)]
        if prompt:
            out_specs += [pl.BlockSpec((R, 2 * KD), row), pl.BlockSpec((R, 2 * KD), row)]
            out_shape += [jax.ShapeDtypeStruct((n_rows, 2 * KD), _BF),
                          jax.ShapeDtypeStruct((n_rows, 2 * KD), _BF)]
    kern = functools.partial(_ffn_kernel, prompt=prompt, with_oproj=with_oproj,
                             with_kvq=with_kvq)
    name = "ffn_%s_%d" % ("prompt" if prompt else "sample", layer)
    return pl.pallas_call(
        kern, grid=grid, in_specs=in_specs, out_specs=out_specs, out_shape=out_shape,
        scratch_shapes=scratch, compiler_params=_params(len(grid)), name=name,
    )(*args)


def _t5_buckets(dist):
    max_exact = N_BUCKETS // 2
    d = np.maximum(dist, 0)
    df = np.maximum(d, 1).astype(np.float32)
    large = max_exact + (np.log(df / np.float32(max_exact))
                         / np.float32(math.log(MAX_DISTANCE / max_exact))
                         * np.float32(N_BUCKETS - max_exact)).astype(np.int32)
    large = np.minimum(large, N_BUCKETS - 1)
    return np.where(d < max_exact, d, large).astype(np.int32)


LOG2E = math.log2(math.e)
NEG2 = NEG * LOG2E


def _low_lanes(rows):
    return lax.broadcasted_iota(jnp.int32, (rows, LANES), 1) < HEAD_DIM


def _bias_rows(revb_ref, table_ref, hd, nq):
    revb = revb_ref[...]
    row = jnp.zeros(revb.shape, _F32)
    for bk in range(N_BUCKETS):
        row = jnp.where(revb == bk, table_ref[bk, hd] * LOG2E, row)
    base = jnp.concatenate([jnp.broadcast_to(row, (nq, LANES)),
                            jnp.full((nq, LANES), NEG2, _F32)], axis=1)
    return pltpu.roll(base, 1, 1, stride=1, stride_axis=0)


def _norm_heads(q, seg, gqs):
    n = q.shape[0]
    nc = D_MODEL // LANES
    sq = jnp.concatenate([q[:, c * LANES:(c + 1) * LANES] for c in range(nc)], axis=0)
    ss = jnp.dot((sq * sq).astype(_BF), seg, preferred_element_type=_F32)
    r = lax.rsqrt(ss * (1.0 / HEAD_DIM) + EPS)
    return [q[:, c * LANES:(c + 1) * LANES] * r[c * n:(c + 1) * n, :] * gqs for c in range(nc)]


def _sink_softmax2(logits2, sink2):
    m = jnp.maximum(jnp.max(logits2, axis=-1, keepdims=True), sink2)
    e = jnp.exp2(logits2 - m)
    denom = jnp.sum(e, axis=-1, keepdims=True) + jnp.exp2(sink2 - m)
    return e * (1.0 / denom)


def _dot_t(a, b):
    return lax.dot_general(a, b, (((1,), (1,)), ((), ())), preferred_element_type=_F32)


def _attend_blocks(blocks, sinks_ref, seg, gqs):
    units = []
    for (q, kd, vd, bias_of) in blocks:
        nq = q.shape[0]
        low = _low_lanes(nq)
        qn = _norm_heads(q, seg, gqs)
        for g in range(N_KV_HEADS):
            qs = jnp.concatenate(
                [jnp.where(low if hl % 2 == 0 else jnp.logical_not(low), qn[2 * g + hl // 2], 0.0)
                 for hl in range(GROUP)], axis=0).astype(_BF)
            units.append((g, nq, bias_of, _dot_t(qs, kd[:, g * LANES:(g + 1) * LANES]),
                          vd[:, g * LANES:(g + 1) * LANES]))
    probs = []
    for (g, nq, bias_of, s, _) in units:
        probs.append(jnp.concatenate(
            [_sink_softmax2(s[hl * nq:(hl + 1) * nq, :] + bias_of(g * GROUP + hl),
                            sinks_ref[g * GROUP + hl] * LOG2E) for hl in range(GROUP)],
            axis=0).astype(_BF))
    res, outs = [], []
    for (g, nq, _, _, vg), p in zip(units, probs):
        low_k = _low_lanes(vg.shape[0])
        zero = jnp.zeros_like(vg)
        vlo, vhi = jnp.where(low_k, vg, zero), jnp.where(low_k, zero, vg)
        for pair in range(2):
            r0 = 2 * pair * nq
            outs.append(jnp.dot(p[r0:r0 + nq], vlo, preferred_element_type=_F32)
                        + jnp.dot(p[r0 + nq:r0 + 2 * nq], vhi, preferred_element_type=_F32))
        if g == N_KV_HEADS - 1:
            res.append(jnp.concatenate(outs, axis=1))
            outs = []
    return res


def _attn_prompt_kernel(q_ref, kp_ref, kc_ref, vp_ref, vc_ref, revb_ref, table_ref,
                        sinks_ref, gq_ref, seg_ref, o_ref, bias_ref):
    W = WINDOW
    i = pl.program_id(1)

    @pl.when(jnp.logical_and(pl.program_id(0) == 0, i == 0))
    def _():
        first_half = lax.broadcasted_iota(jnp.int32, (W, 2 * W), 1) < W
        for hd in range(N_HEADS):
            rows = _bias_rows(revb_ref, table_ref, hd, W)
            bias_ref[0, hd] = rows
            bias_ref[1, hd] = jnp.where(first_half, NEG2, rows)

    gqs = gq_ref[...] * (SCALE * LOG2E)
    kp, kc = kp_ref[...], kc_ref[...]
    vp, vc = vp_ref[...], vc_ref[...]
    first = jnp.where(i == 0, 1, 0)
    blocks = []
    for blk in range(ATT_BLOCKS):
        if blk == 0:
            kd = jnp.concatenate([kp, kc[0:W]], axis=0)
            vd = jnp.concatenate([vp, vc[0:W]], axis=0)
            bias_of = lambda hd: bias_ref[first, hd]
        else:
            kd, vd = kc[(blk - 1) * W:(blk + 1) * W], vc[(blk - 1) * W:(blk + 1) * W]
            bias_of = lambda hd: bias_ref[0, hd]
        blocks.append((q_ref[blk * W:(blk + 1) * W, :], kd, vd, bias_of))
    for blk, o in enumerate(_attend_blocks(blocks, sinks_ref, seg_ref[...], gqs)):
        o_ref[blk * W:(blk + 1) * W, :] = o.astype(o_ref.dtype)


def _attn_prompt(q2d, kd2d, vd2d, revb, table, sinks, gq2, seg):
    W = WINDOW
    R = ATT_BLOCKS * W
    ns = SEQ // R
    cur = lambda b, i: (b * ns + i, 0)
    prev = lambda b, i: (b * (SEQ // W) + jnp.maximum(ATT_BLOCKS * i - 1, 0), 0)
    smem = pl.BlockSpec(memory_space=pltpu.SMEM)
    return pl.pallas_call(
        _attn_prompt_kernel,
        grid=(BATCH, ns),
        in_specs=[pl.BlockSpec((R, D_MODEL), cur),
                  pl.BlockSpec((W, 2 * KD), prev), pl.BlockSpec((R, 2 * KD), cur),
                  pl.BlockSpec((W, 2 * KD), prev), pl.BlockSpec((R, 2 * KD), cur),
                  _const_spec((1, LANES)), smem, smem, _const_spec((1, LANES)),
                  _const_spec((LANES, LANES))],
        out_specs=pl.BlockSpec((R, D_MODEL), cur),
        out_shape=jax.ShapeDtypeStruct((BATCH * SEQ, D_MODEL), _BF),
        scratch_shapes=[pltpu.VMEM((2, N_HEADS, W, 2 * W), _F32)],
        compiler_params=_params(2),
        name="attn_prompt",
    )(q2d, kd2d, kd2d, vd2d, vd2d, revb, table, sinks, gq2, seg)


def _step_logits(q, kb, bias_all, gqs):
    n = DEC_SEQ
    low = _low_lanes(n)
    zero = jnp.zeros((n, LANES), _F32)
    rows = []
    for hd in range(N_HEADS):
        g = hd // GROUP
        x = q[:, (hd // 2) * LANES:(hd // 2 + 1) * LANES]
        if hd % 2 != g % 2:
            x = pltpu.roll(x, HEAD_DIM, 1)
        x = jnp.where(low if g % 2 == 0 else jnp.logical_not(low), x, 0.0)
        ss = jnp.sum(x * x, axis=-1, keepdims=True) * (1.0 / HEAD_DIM)
        x = x * lax.rsqrt(ss + EPS) * gqs
        rows.append(jnp.concatenate([x, zero] if g // 2 == 0 else [zero, x], axis=1))
    q_all = jnp.concatenate(rows, axis=0).astype(_BF)
    return _dot_t(q_all, kb) + bias_all


def _split_heads(o_all):
    n = DEC_SEQ
    low = _low_lanes(n)
    outs = []
    for c in range(N_HEADS // 2):
        halves = []
        for hf in range(2):
            hd = 2 * c + hf
            g = hd // GROUP
            y = o_all[hd * n:(hd + 1) * n, (g // 2) * LANES:(g // 2 + 1) * LANES]
            halves.append(y if g % 2 == hf else pltpu.roll(y, HEAD_DIM, 1))
        outs.append(jnp.where(low, halves[0], halves[1]))
    return jnp.concatenate(outs, axis=1)


def _attn_sample_kernel(q_ref, kn_ref, vn_ref, ck_ref, cv_ref, revb_ref, table_ref,
                        sinks_ref, gq_ref, o_ref, ko_ref, vo_ref, bias_ref, kbuf_ref, vbuf_ref):
    W = WINDOW
    L = 2 * W
    U = ATT_UNROLL

    @pl.when(pl.program_id(0) == 0)
    def _():
        for hd in range(N_HEADS):
            bias_ref[hd * DEC_SEQ:(hd + 1) * DEC_SEQ, :] = _bias_rows(revb_ref, table_ref, hd,
                                                                      DEC_SEQ)
        kbuf_ref[...] = jnp.zeros((U, L, KD), _F32)
        vbuf_ref[...] = jnp.zeros((U, L, KD), _F32)

    gqs = gq_ref[...] * (SCALE * LOG2E)
    bias_all = bias_ref[...]
    sink_col = jnp.concatenate(
        [jnp.full((DEC_SEQ, 1), sinks_ref[hd] * LOG2E, _F32) for hd in range(N_HEADS)], axis=0)

    def body(it, carry):
        logits = []
        for u in range(U):
            s = it * U + u
            kbuf_ref[u, 0:W, :] = ck_ref[s]
            vbuf_ref[u, 0:W, :] = cv_ref[s]
            kbuf_ref[u, W:W + DEC_SEQ, :] = kn_ref[s]
            vbuf_ref[u, W:W + DEC_SEQ, :] = vn_ref[s]
            ko_ref[s] = kbuf_ref[u, DEC_SEQ:DEC_SEQ + W, :]
            vo_ref[s] = vbuf_ref[u, DEC_SEQ:DEC_SEQ + W, :]
            logits.append(_step_logits(q_ref[s], kbuf_ref[u].astype(_BF), bias_all, gqs))
        probs = [_sink_softmax2(l, sink_col).astype(_BF) for l in logits]
        for u in range(U):
            o_all = jnp.dot(probs[u], vbuf_ref[u].astype(_BF), preferred_element_type=_F32)
            o_ref[it * U + u] = _split_heads(o_all)
        return carry

    lax.fori_loop(0, ATT_SEQS // U, body, 0)


def _attn_sample(q3d, kn3d, vn3d, ck, cv, revb, table, sinks, gq2):
    W = WINDOW
    A = ATT_SEQS
    blk = lambda r, c: pl.BlockSpec((A, r, c), lambda i: (i, 0, 0))
    smem = pl.BlockSpec(memory_space=pltpu.SMEM)
    return pl.pallas_call(
        _attn_sample_kernel,
        grid=(DEC_BATCH // A,),
        in_specs=[blk(DEC_SEQ, D_MODEL), blk(DEC_SEQ, KD), blk(DEC_SEQ, KD),
                  blk(W, KD), blk(W, KD),
                  _const_spec((1, LANES)), smem, smem, _const_spec((1, LANES))],
        out_specs=[blk(DEC_SEQ, D_MODEL), blk(W, KD), blk(W, KD)],
        out_shape=[jax.ShapeDtypeStruct((DEC_BATCH, DEC_SEQ, D_MODEL), _F32),
                   jax.ShapeDtypeStruct((DEC_BATCH, W, KD), _F32),
                   jax.ShapeDtypeStruct((DEC_BATCH, W, KD), _F32)],
        scratch_shapes=[pltpu.VMEM((N_HEADS * DEC_SEQ, 2 * W), _F32),
                        pltpu.VMEM((ATT_UNROLL, 2 * W, KD), _F32),
                        pltpu.VMEM((ATT_UNROLL, 2 * W, KD), _F32)],
        compiler_params=_params(1),
        name="attn_sample",
    )(q3d, kn3d, vn3d, ck, cv, revb, table, sinks, gq2)


def kernel(x_prompt, x_sample, state_conv, state_ffn, cache_k, cache_v, p_prompt, p_sample, g_mix, cm_w_pw1, cm_b_pw1, cm_w_dw, cm_b_dw, cm_ln_g, cm_ln_b, cm_w_pw2, cm_b_pw2, at_w_q, at_g_q, at_sinks, at_w_o, kv_g, kv_w_k, kv_w_v, kv_g_k, rel_bias, g_ffn, ffn_w_up, ffn_w_dw, ffn_b_dw, ffn_w_down, g_ple, ple_w_gate, ple_w_proj):
    D, F = D_MODEL, D_FF
    row = lambda a: a.reshape(1, -1)
    bf = lambda a: a.astype(_BF)

    mixer_w = [row(g_mix[0]), bf(cm_w_pw1[0]), row(cm_b_pw1[0]), cm_w_dw[0], row(cm_b_dw[0]),
               row(cm_ln_g[0]), row(cm_ln_b[0]), bf(cm_w_pw2[0]), row(cm_b_pw2[0])]
    cast_srcs = [ffn_w_up.reshape(2 * D, 2 * F), ffn_w_down.reshape(2 * F, D),
                 ple_w_gate.reshape(2 * D, D), ple_w_proj.reshape(2 * PLE_DIM, D)]

    xp = x_prompt.reshape(BATCH * SEQ, D)
    pp = p_prompt.reshape(2, BATCH * SEQ, PLE_DIM)
    h1, conv_p, w_up_bf, w_down_bf, w_gate_bf, w_proj_bf = _mixer_prompt(xp, mixer_w, cast_srcs)

    stacked = dict(
        g_ffn=g_ffn[:, None, :], w_up=w_up_bf.reshape(2, D, 2 * F), w_dw=ffn_w_dw,
        b_dw=ffn_b_dw[:, None, :], w_down=w_down_bf.reshape(2, F, D), g_ple=g_ple[:, None, :],
        w_gate=w_gate_bf.reshape(2, D, D), w_proj=w_proj_bf.reshape(2, PLE_DIM, D))
    ffn_w = [dict(stacked), dict(stacked)]
    ffn_w[0].update(g_kv=row(kv_g), w_k=bf(kv_w_k), w_v=bf(kv_w_v),
                    g_k2=row(jnp.tile(kv_g_k, 2)), g_q_in=row(g_mix[1]), w_q=bf(at_w_q[0]))
    ffn_w[1].update(wo=bf(at_w_o[0]))
    gq2 = row(jnp.tile(at_g_q[0], 2))
    sinks = at_sinks[0]

    W = WINDOW
    revb = jnp.asarray(_t5_buckets(W - 1 - np.arange(W))[None, :])
    half = np.arange(LANES) // HEAD_DIM
    seg = jnp.asarray(half[:, None] == half[None, :], dtype=_BF)

    h3, ffn_p0, k_p, v_p, q_p, kd_p, vd_p = _ffn_call(
        h1, pp, None, None, ffn_w[0], layer=0, prompt=True, with_oproj=False, with_kvq=True)
    o_p = _attn_prompt(q_p, kd_p, vd_p, revb, rel_bias, sinks, gq2, seg)
    y_p, ffn_p = _ffn_call(h3, pp, None, o_p, ffn_w[1], layer=1, prompt=True,
                           with_oproj=True, with_kvq=False, prev_state=ffn_p0)

    ps = p_sample.reshape(2, DEC_BATCH * DEC_SEQ, PLE_DIM)
    g1_tm, conv_s_tm = _mixer_sample(jnp.transpose(x_sample, (1, 0, 2)),
                                     jnp.transpose(state_conv[0], (1, 0, 2)), mixer_w)
    g1 = jnp.transpose(g1_tm, (1, 0, 2)).reshape(DEC_BATCH * DEC_SEQ, D)
    conv_s = jnp.transpose(conv_s_tm, (1, 0, 2))
    g3, ffn_s0, k_s, v_s, q_s = _ffn_call(g1, ps, state_ffn, None, ffn_w[0], layer=0,
                                          prompt=False, with_oproj=False, with_kvq=True)
    o_s, k_out, v_out = _attn_sample(
        q_s.reshape(DEC_BATCH, DEC_SEQ, D), k_s.reshape(DEC_BATCH, DEC_SEQ, KD),
        v_s.reshape(DEC_BATCH, DEC_SEQ, KD), cache_k.reshape(DEC_BATCH, W, KD),
        cache_v.reshape(DEC_BATCH, W, KD), revb, rel_bias, sinks, gq2)
    y_s, ffn_s = _ffn_call(g3, ps, state_ffn, o_s.reshape(DEC_BATCH * DEC_SEQ, D),
                           ffn_w[1], layer=1, prompt=False, with_oproj=True, with_kvq=False,
                           prev_state=ffn_s0)

    kv_shape_p = (BATCH, W, N_KV_HEADS, HEAD_DIM)
    kv_shape_s = (DEC_BATCH, W, N_KV_HEADS, HEAD_DIM)
    return (y_p.reshape(BATCH, SEQ, D), y_s.reshape(DEC_BATCH, DEC_SEQ, D),
            conv_p[None], conv_s[None],
            ffn_p, ffn_s,
            k_p.reshape(BATCH, SEQ, KD)[:, SEQ - W:].reshape(kv_shape_p),
            k_out.reshape(kv_shape_s),
            v_p.reshape(BATCH, SEQ, KD)[:, SEQ - W:].reshape(kv_shape_p),
            v_out.reshape(kv_shape_s))
```

```python
import functools
import math

import jax
import jax.numpy as jnp
import numpy as np
from jax import lax
from jax.experimental import pallas as pl
from jax.experimental.pallas import tpu as pltpu

D_MODEL = 1024
BATCH = 8
SEQ = 2048
DEC_BATCH = 128
DEC_SEQ = 8
CONV_K = 31
FFN_CONV_K = 3
D_FF = 2816
N_HEADS = 16
N_KV_HEADS = 4
HEAD_DIM = 64
GROUP = N_HEADS // N_KV_HEADS
WINDOW = 128
N_BUCKETS = 32
MAX_DISTANCE = 128
PLE_DIM = 256
EPS = 1e-6
SCALE = HEAD_DIM ** -0.5
NEG = -1e30
KD = N_KV_HEADS * HEAD_DIM

LANES = 128
SUBLANES = 8
VMEM_LIMIT = 60 * 1024 * 1024

T_TILE = 512
MIX_TILE = 1024
S_TILE = 32
CONV_HALO = 32
FFN_HALO = 8
F_CHUNK = 2816
D_SLABS = D_MODEL // LANES
F_SLABS = D_FF // LANES
ATT_SEQS = 16
ATT_UNROLL = 8
ATT_BLOCKS = 2

_BF = jnp.bfloat16
_F32 = jnp.float32


def _bdot(a, w):
    return jnp.dot(a.astype(_BF), w, preferred_element_type=_F32)


def _rms(x, g):
    return x * lax.rsqrt(jnp.mean(x * x, axis=-1, keepdims=True) + EPS) * g


def _sigmoid(x):
    return 1.0 / (1.0 + jnp.exp(-x))


def _const_spec(shape):
    nd = len(shape)
    return pl.BlockSpec(shape, lambda *_: (0,) * nd, pipeline_mode=pl.Buffered(1))


def _layer_spec(shape, layer):
    nd = len(shape)
    return pl.BlockSpec((None,) + tuple(shape), lambda *_: (layer,) + (0,) * nd,
                        pipeline_mode=pl.Buffered(1))


def _params(n_grid):
    return pltpu.CompilerParams(dimension_semantics=("arbitrary",) * n_grid,
                                vmem_limit_bytes=VMEM_LIMIT)


def _mixer_tail(x, c, bdw, lng, lnb, w2, b2):
    c = c + bdw
    mu = jnp.mean(c, axis=-1, keepdims=True)
    cz = c - mu
    var = jnp.mean(cz * cz, axis=-1, keepdims=True)
    y = cz * lax.rsqrt(var + EPS) * lng + lnb
    s = y * _sigmoid(y)
    return x + _bdot(s, w2) + b2


def _mixer_prompt_kernel(x_ref, g_ref, w1_ref, b1_ref, wdw_ref, bdw_ref, lng_ref, lnb_ref,
                         w2_ref, b2_ref, *rest):
    n_cast = len(_CAST_ROWS)
    cast_in, rest = rest[:n_cast], rest[n_cast:]
    h_ref, cs_ref = rest[0], rest[1]
    cast_out, (slab_ref, oslab_ref) = rest[2:2 + n_cast], rest[2 + n_cast:]
    T = MIX_TILE
    H = T // 2
    j = pl.program_id(1)

    for src, dst in zip(cast_in, cast_out):
        dst[...] = src[...].astype(_BF)

    @pl.when(j == 0)
    def _():
        slab_ref[:, 0:CONV_HALO, :] = jnp.zeros((D_SLABS, CONV_HALO, LANES), _F32)

    x = x_ref[...]
    u = _bdot(_rms(x, g_ref[...]), w1_ref[...]) + b1_ref[...]
    glu = u[:, :D_MODEL] * _sigmoid(u[:, D_MODEL:])
    for c in range(D_SLABS):
        slab_ref[c, CONV_HALO:CONV_HALO + T, :] = glu[:, c * LANES:(c + 1) * LANES]
    cs_ref[...] = glu[T - (CONV_K - 1):, :]

    base = CONV_HALO - (CONV_K - 1)
    for c in range(D_SLABS):
        wc = wdw_ref[:, c * LANES:(c + 1) * LANES]
        for p in range(2):
            acc = slab_ref[c, pl.ds(base + p, H, stride=2), :] * wc[0:1, :]
            for k in range(1, CONV_K):
                acc = acc + slab_ref[c, pl.ds(base + k + p, H, stride=2), :] * wc[k:k + 1, :]
            oslab_ref[c, pl.ds(p, H, stride=2), :] = acc
    conv = jnp.concatenate([oslab_ref[c] for c in range(D_SLABS)], axis=1)
    h_ref[...] = _mixer_tail(x, conv, bdw_ref[...], lng_ref[...], lnb_ref[...],
                             w2_ref[...], b2_ref[...])
    for c in range(D_SLABS):
        slab_ref[c, 0:CONV_HALO, :] = slab_ref[c, T:T + CONV_HALO, :]


def _mixer_sample_kernel(x_ref, st_ref, g_ref, w1_ref, b1_ref, wdw_ref, bdw_ref, lng_ref,
                         lnb_ref, w2_ref, b2_ref, h_ref, cs_ref):
    S = S_TILE
    KH = CONV_K - 1
    x = x_ref[...]
    u = _bdot(_rms(x, g_ref[...]), w1_ref[...]) + b1_ref[...]
    glu = u[:, :D_MODEL] * _sigmoid(u[:, D_MODEL:])
    glu_tm = jnp.swapaxes(glu.reshape(S, DEC_SEQ, D_MODEL), 0, 1)
    cs_ref[0:KH - DEC_SEQ] = st_ref[DEC_SEQ:KH]
    cs_ref[KH - DEC_SEQ:KH] = glu_tm

    def plane(j):
        return st_ref[j] if j < KH else glu_tm[j - KH]

    outs = []
    for t in range(DEC_SEQ):
        acc = plane(t) * wdw_ref[0:1, :]
        for k in range(1, CONV_K):
            acc = acc + plane(t + k) * wdw_ref[k:k + 1, :]
        outs.append(acc)
    conv = jnp.swapaxes(jnp.stack(outs, axis=0), 0, 1).reshape(S * DEC_SEQ, D_MODEL)
    h_ref[...] = _mixer_tail(x, conv, bdw_ref[...], lng_ref[...], lnb_ref[...],
                             w2_ref[...], b2_ref[...])


def _mixer_weight_specs():
    D = D_MODEL
    return [_const_spec((1, D)), _const_spec((D, 2 * D)), _const_spec((1, 2 * D)),
            _const_spec((CONV_K, D)), _const_spec((1, D)), _const_spec((1, D)),
            _const_spec((1, D)), _const_spec((D, D)), _const_spec((1, D))]


_CAST_ROWS = (2 * D_MODEL, 2 * D_FF, 2 * D_MODEL, 2 * PLE_DIM)


def _mixer_prompt(x2d, wts, cast_srcs):
    T = MIX_TILE
    nt = SEQ // T
    steps = BATCH * nt
    row = lambda b, j: (b * nt + j, 0)
    cast_specs, cast_shapes = [], []
    for src, rows in zip(cast_srcs, _CAST_ROWS):
        assert src.shape[0] == rows and rows % steps == 0
        cast_specs.append(pl.BlockSpec((rows // steps, src.shape[1]), row))
        cast_shapes.append(jax.ShapeDtypeStruct(src.shape, _BF))
    return pl.pallas_call(
        _mixer_prompt_kernel,
        grid=(BATCH, nt),
        in_specs=[pl.BlockSpec((T, D_MODEL), row)] + _mixer_weight_specs() + cast_specs,
        out_specs=[pl.BlockSpec((T, D_MODEL), row),
                   pl.BlockSpec((None, CONV_K - 1, D_MODEL), lambda b, j: (b, 0, 0))] + cast_specs,
        out_shape=[jax.ShapeDtypeStruct((BATCH * SEQ, D_MODEL), _F32),
                   jax.ShapeDtypeStruct((BATCH, CONV_K - 1, D_MODEL), _F32)] + cast_shapes,
        scratch_shapes=[pltpu.VMEM((D_SLABS, CONV_HALO + T, LANES), _F32),
                        pltpu.VMEM((D_SLABS, T, LANES), _F32)],
        compiler_params=_params(2),
        name="mixer_prompt",
    )(x2d, *wts, *cast_srcs)


def _mixer_sample(x2d, state_tm, wts):
    S = S_TILE
    R = S * DEC_SEQ
    rows = pl.BlockSpec((R, D_MODEL), lambda i: (i, 0))
    planes = pl.BlockSpec((CONV_K - 1, S, D_MODEL), lambda i: (0, i, 0))
    return pl.pallas_call(
        _mixer_sample_kernel,
        grid=(DEC_BATCH // S,),
        in_specs=[rows, planes] + _mixer_weight_specs(),
        out_specs=[rows, planes],
        out_shape=[jax.ShapeDtypeStruct((DEC_BATCH * DEC_SEQ, D_MODEL), _F32),
                   jax.ShapeDtypeStruct((CONV_K - 1, DEC_BATCH, D_MODEL), _F32)],
        compiler_params=_params(1),
        name="mixer_sample",
    )(x2d, state_tm, *wts)


def _ffn_kernel(*refs, prompt, with_oproj, with_kvq):
    it = iter(refs)
    h_ref = next(it)
    p_ref = next(it)
    st_ref = None if prompt else next(it)
    if with_oproj:
        o_ref = next(it)
        wo_ref = next(it)
        fs0_ref = next(it)
    gf_ref, wup_ref, wdw_ref, bdw_ref, wdn_ref = (next(it) for _ in range(5))
    gp_ref, wg_ref, wp_ref = (next(it) for _ in range(3))
    if with_kvq:
        gkv_ref, wk_ref, wv_ref, gk_ref, gq_ref, wq_ref = (next(it) for _ in range(6))
    hout_ref = next(it)
    fs_ref = next(it)
    dup_kv = with_kvq and prompt
    if with_kvq:
        k_ref, v_ref, q_ref = (next(it) for _ in range(3))
    if dup_kv:
        kd_ref, vd_ref = next(it), next(it)
    if prompt:
        slab_ref, oslab_ref = next(it), next(it)
    else:
        win_ref = next(it)

    F = D_FF
    R = T_TILE if prompt else S_TILE * DEC_SEQ
    H = R // 2
    KH = FFN_CONV_K - 1

    if prompt:
        @pl.when(pl.program_id(1) == 0)
        def _():
            slab_ref[:, 0:FFN_HALO, :] = jnp.zeros((2 * F_SLABS, FFN_HALO, LANES), _F32)
    else:
        win_ref[:, FFN_HALO - KH:FFN_HALO, :] = st_ref[...]

    h = h_ref[...]
    if with_oproj:
        h = h + _bdot(o_ref[...], wo_ref[...])
    xn = _rms(h, gf_ref[...]).astype(_BF)

    n_chunks = F // F_CHUNK
    spc = F_CHUNK // LANES
    acc = None
    for ci in range(n_chunks):
        halves = []
        for half in range(2):
            c0 = half * F + ci * F_CHUNK
            uc = jnp.dot(xn, wup_ref[:, c0:c0 + F_CHUNK], preferred_element_type=_F32)
            w = wdw_ref[:, c0:c0 + F_CHUNK]
            b = bdw_ref[:, c0:c0 + F_CHUNK]
            if prompt:
                cols = []
                for s in range(spc):
                    sl = half * F_SLABS + ci * spc + s
                    slab_ref[sl, FFN_HALO:FFN_HALO + R, :] = uc[:, s * LANES:(s + 1) * LANES]
                    ws = w[:, s * LANES:(s + 1) * LANES]
                    ph = []
                    for p in range(2):
                        cv = b[:, s * LANES:(s + 1) * LANES]
                        for k in range(FFN_CONV_K):
                            cv = cv + (slab_ref[sl, pl.ds(FFN_HALO - KH + k + p, H, stride=2), :]
                                       * ws[k:k + 1, :])
                        ph.append(cv)
                    cols.append(jnp.concatenate(ph, axis=0))
                halves.append(jnp.concatenate(cols, axis=1))
            else:
                win_ref[:, FFN_HALO:FFN_HALO + DEC_SEQ, c0:c0 + F_CHUNK] = uc.reshape(
                    S_TILE, DEC_SEQ, F_CHUNK)
                cv = b
                for k in range(FFN_CONV_K):
                    o = FFN_HALO - KH + k
                    cv = cv + win_ref[:, o:o + DEC_SEQ, c0:c0 + F_CHUNK] * w[k:k + 1, :]
                halves.append(cv.reshape(R, F_CHUNK))
        gate, val = halves
        act = (gate * _sigmoid(gate) * val).astype(_BF)
        part = jnp.dot(act, wdn_ref[ci * F_CHUNK:(ci + 1) * F_CHUNK, :],
                       preferred_element_type=_F32)
        acc = part if acc is None else acc + part

    if prompt:
        for c in range(D_SLABS):
            oslab_ref[c, pl.ds(0, H, stride=2), :] = acc[:H, c * LANES:(c + 1) * LANES]
            oslab_ref[c, pl.ds(1, H, stride=2), :] = acc[H:, c * LANES:(c + 1) * LANES]
        f = jnp.concatenate([oslab_ref[c] for c in range(D_SLABS)], axis=1)
        new_state = jnp.concatenate(
            [slab_ref[s, FFN_HALO + R - SUBLANES:FFN_HALO + R, :] for s in range(2 * F_SLABS)],
            axis=1)[SUBLANES - KH:, :]
        for s in range(2 * F_SLABS):
            slab_ref[s, 0:FFN_HALO, :] = slab_ref[s, R:R + FFN_HALO, :]
    else:
        f = acc
        new_state = win_ref[:, FFN_HALO + DEC_SEQ - KH:FFN_HALO + DEC_SEQ, :]
    if with_oproj:
        fs_ref[0] = fs0_ref[...]
        fs_ref[1] = new_state
    else:
        fs_ref[...] = new_state

    h2 = h + f
    gate = _sigmoid(_bdot(_rms(h2, gp_ref[...]), wg_ref[...]))
    h3 = h2 + gate * _bdot(p_ref[...], wp_ref[...])
    hout_ref[...] = h3

    if with_kvq:
        hk = _rms(h3, gkv_ref[...]).astype(_BF)
        kk = jnp.dot(hk, wk_ref[...], preferred_element_type=_F32)
        vv = jnp.dot(hk, wv_ref[...], preferred_element_type=_F32)
        v_ref[...] = vv
        low = lax.broadcasted_iota(jnp.int32, (R, LANES), 1) < HEAD_DIM

        def dup_heads(x):
            out = []
            for c in range(KD // LANES):
                xc = x[:, c * LANES:(c + 1) * LANES]
                xr = pltpu.roll(xc, HEAD_DIM, 1)
                out += [jnp.where(low, xc, xr), jnp.where(low, xr, xc)]
            return out

        kn = []
        for kg in dup_heads(kk):
            r = lax.rsqrt(jnp.mean(kg * kg, axis=-1, keepdims=True) + EPS)
            kn.append(kg * r * gk_ref[...])
        k_ref[...] = jnp.concatenate(
            [jnp.where(low, kn[2 * c], kn[2 * c + 1]) for c in range(KD // LANES)], axis=1)
        if dup_kv:
            kd_ref[...] = jnp.concatenate(kn, axis=1).astype(_BF)
            vd_ref[...] = jnp.concatenate(dup_heads(vv), axis=1).astype(_BF)
        q_ref[...] = _bdot(_rms(h3, gq_ref[...]), wq_ref[...])


def _ffn_call(h2d, p3d, state4d, o2d, w, *, layer, prompt, with_oproj, with_kvq,
              prev_state=None):
    D, F = D_MODEL, D_FF
    if prompt:
        R = T_TILE
        nt = SEQ // R
        grid = (BATCH, nt)
        row = lambda b, j: (b * nt + j, 0)
        p_spec = pl.BlockSpec((None, R, PLE_DIM), lambda b, j: (layer, b * nt + j, 0))
        n_rows, n_seq = BATCH * SEQ, BATCH
        fs_spec = pl.BlockSpec((None, FFN_CONV_K - 1, 2 * F), lambda b, j: (b, 0, 0))
        scratch = [pltpu.VMEM((2 * F_SLABS, FFN_HALO + R, LANES), _F32),
                   pltpu.VMEM((D_SLABS, R, LANES), _F32)]
    else:
        R = S_TILE * DEC_SEQ
        grid = (DEC_BATCH // S_TILE,)
        row = lambda i: (i, 0)
        p_spec = pl.BlockSpec((None, R, PLE_DIM), lambda i: (layer, i, 0))
        n_rows, n_seq = DEC_BATCH * DEC_SEQ, DEC_BATCH
        fs_spec = pl.BlockSpec((S_TILE, FFN_CONV_K - 1, 2 * F), lambda i: (i, 0, 0))
        scratch = [pltpu.VMEM((S_TILE, FFN_HALO + DEC_SEQ, 2 * F), _F32)]

    args = [h2d, p3d]
    in_specs = [pl.BlockSpec((R, D), row), p_spec]
    if not prompt:
        args.append(state4d)
        in_specs.append(pl.BlockSpec((None, S_TILE, FFN_CONV_K - 1, 2 * F),
                                     lambda i: (layer, i, 0, 0)))
    fs_shape = (n_seq, FFN_CONV_K - 1, 2 * F)
    if with_oproj:
        args += [o2d, w["wo"], prev_state]
        in_specs += [pl.BlockSpec((R, D), row), _const_spec((D, D)), fs_spec]
        blk, imap = fs_spec.block_shape, fs_spec.index_map
        fs_spec = pl.BlockSpec((2,) + tuple(blk), lambda *g: (0,) + tuple(imap(*g)))
        fs_shape = (2,) + fs_shape
    args += [w["g_ffn"], w["w_up"], w["w_dw"], w["b_dw"], w["w_down"],
             w["g_ple"], w["w_gate"], w["w_proj"]]
    in_specs += [_layer_spec((1, D), layer), _layer_spec((D, 2 * F), layer),
                 _layer_spec((FFN_CONV_K, 2 * F), layer), _layer_spec((1, 2 * F), layer),
                 _layer_spec((F, D), layer), _layer_spec((1, D), layer),
                 _layer_spec((D, D), layer), _layer_spec((PLE_DIM, D), layer)]
    out_specs = [pl.BlockSpec((R, D), row), fs_spec]
    out_shape = [jax.ShapeDtypeStruct((n_rows, D), _F32),
                 jax.ShapeDtypeStruct(fs_shape, _F32)]
    if with_kvq:
        args += [w["g_kv"], w["w_k"], w["w_v"], w["g_k2"], w["g_q_in"], w["w_q"]]
        in_specs += [_const_spec((1, D)), _const_spec((D, KD)), _const_spec((D, KD)),
                     _const_spec((1, LANES)), _const_spec((1, D)), _const_spec((D, D))]
        out_specs += [pl.BlockSpec((R, KD), row), pl.BlockSpec((R, KD), row),
                      pl.BlockSpec((R, D), row)]
        out_shape += [jax.ShapeDtypeStruct((n_rows, KD), _F32),
                      jax.ShapeDtypeStruct((n_rows, KD), _F32),
                      jax.ShapeDtypeStruct((n_rows, D), _F32)]
        if prompt:
            out_specs += [pl.BlockSpec((R, 2 * KD), row), pl.BlockSpec((R, 2 * KD), row)]
            out_shape += [jax.ShapeDtypeStruct((n_rows, 2 * KD), _BF),
                          jax.ShapeDtypeStruct((n_rows, 2 * KD), _BF)]
    kern = functools.partial(_ffn_kernel, prompt=prompt, with_oproj=with_oproj,
                             with_kvq=with_kvq)
    name = "ffn_%s_%d" % ("prompt" if prompt else "sample", layer)
    return pl.pallas_call(
        kern, grid=grid, in_specs=in_specs, out_specs=out_specs, out_shape=out_shape,
        scratch_shapes=scratch, compiler_params=_params(len(grid)), name=name,
    )(*args)


def _t5_buckets(dist):
    max_exact = N_BUCKETS // 2
    d = np.maximum(dist, 0)
    df = np.maximum(d, 1).astype(np.float32)
    large = max_exact + (np.log(df / np.float32(max_exact))
                         / np.float32(math.log(MAX_DISTANCE / max_exact))
                         * np.float32(N_BUCKETS - max_exact)).astype(np.int32)
    large = np.minimum(large, N_BUCKETS - 1)
    return np.where(d < max_exact, d, large).astype(np.int32)


LOG2E = math.log2(math.e)
NEG2 = NEG * LOG2E


def _low_lanes(rows):
    return lax.broadcasted_iota(jnp.int32, (rows, LANES), 1) < HEAD_DIM


def _bias_rows(revb_ref, table_ref, hd, nq):
    revb = revb_ref[...]
    row = jnp.zeros(revb.shape, _F32)
    for bk in range(N_BUCKETS):
        row = jnp.where(revb == bk, table_ref[bk, hd] * LOG2E, row)
    base = jnp.concatenate([jnp.broadcast_to(row, (nq, LANES)),
                            jnp.full((nq, LANES), NEG2, _F32)], axis=1)
    return pltpu.roll(base, 1, 1, stride=1, stride_axis=0)


def _norm_heads(q, seg, gqs):
    n = q.shape[0]
    nc = D_MODEL // LANES
    sq = jnp.concatenate([q[:, c * LANES:(c + 1) * LANES] for c in range(nc)], axis=0)
    ss = jnp.dot((sq * sq).astype(_BF), seg, preferred_element_type=_F32)
    r = lax.rsqrt(ss * (1.0 / HEAD_DIM) + EPS)
    return [q[:, c * LANES:(c + 1) * LANES] * r[c * n:(c + 1) * n, :] * gqs for c in range(nc)]


def _sink_softmax2(logits2, sink2):
    m = jnp.maximum(jnp.max(logits2, axis=-1, keepdims=True), sink2)
    e = jnp.exp2(logits2 - m)
    denom = jnp.sum(e, axis=-1, keepdims=True) + jnp.exp2(sink2 - m)
    return e * (1.0 / denom)


def _dot_t(a, b):
    return lax.dot_general(a, b, (((1,), (1,)), ((), ())), preferred_element_type=_F32)


def _attend_blocks(blocks, sinks_ref, seg, gqs):
    units = []
    for (q, kd, vd, bias_of) in blocks:
        nq = q.shape[0]
        low = _low_lanes(nq)
        qn = _norm_heads(q, seg, gqs)
        for g in range(N_KV_HEADS):
            qs = jnp.concatenate(
                [jnp.where(low if hl % 2 == 0 else jnp.logical_not(low), qn[2 * g + hl // 2], 0.0)
                 for hl in range(GROUP)], axis=0).astype(_BF)
            units.append((g, nq, bias_of, _dot_t(qs, kd[:, g * LANES:(g + 1) * LANES]),
                          vd[:, g * LANES:(g + 1) * LANES]))
    probs = []
    for (g, nq, bias_of, s, _) in units:
        probs.append(jnp.concatenate(
            [_sink_softmax2(s[hl * nq:(hl + 1) * nq, :] + bias_of(g * GROUP + hl),
                            sinks_ref[g * GROUP + hl] * LOG2E) for hl in range(GROUP)],
            axis=0).astype(_BF))
    res, outs = [], []
    for (g, nq, _, _, vg), p in zip(units, probs):
        low_k = _low_lanes(vg.shape[0])
        zero = jnp.zeros_like(vg)
        vlo, vhi = jnp.where(low_k, vg, zero), jnp.where(low_k, zero, vg)
        for pair in range(2):
            r0 = 2 * pair * nq
            outs.append(jnp.dot(p[r0:r0 + nq], vlo, preferred_element_type=_F32)
                        + jnp.dot(p[r0 + nq:r0 + 2 * nq], vhi, preferred_element_type=_F32))
        if g == N_KV_HEADS - 1:
            res.append(jnp.concatenate(outs, axis=1))
            outs = []
    return res


def _attn_prompt_kernel(q_ref, kp_ref, kc_ref, vp_ref, vc_ref, revb_ref, table_ref,
                        sinks_ref, gq_ref, seg_ref, o_ref, bias_ref):
    W = WINDOW
    i = pl.program_id(1)

    @pl.when(jnp.logical_and(pl.program_id(0) == 0, i == 0))
    def _():
        first_half = lax.broadcasted_iota(jnp.int32, (W, 2 * W), 1) < W
        for hd in range(N_HEADS):
            rows = _bias_rows(revb_ref, table_ref, hd, W)
            bias_ref[0, hd] = rows
            bias_ref[1, hd] = jnp.where(first_half, NEG2, rows)

    gqs = gq_ref[...] * (SCALE * LOG2E)
    kp, kc = kp_ref[...], kc_ref[...]
    vp, vc = vp_ref[...], vc_ref[...]
    first = jnp.where(i == 0, 1, 0)
    blocks = []
    for blk in range(ATT_BLOCKS):
        if blk == 0:
            kd = jnp.concatenate([kp, kc[0:W]], axis=0)
            vd = jnp.concatenate([vp, vc[0:W]], axis=0)
            bias_of = lambda hd: bias_ref[first, hd]
        else:
            kd, vd = kc[(blk - 1) * W:(blk + 1) * W], vc[(blk - 1) * W:(blk + 1) * W]
            bias_of = lambda hd: bias_ref[0, hd]
        blocks.append((q_ref[blk * W:(blk + 1) * W, :], kd, vd, bias_of))
    for blk, o in enumerate(_attend_blocks(blocks, sinks_ref, seg_ref[...], gqs)):
        o_ref[blk * W:(blk + 1) * W, :] = o.astype(o_ref.dtype)


def _attn_prompt(q2d, kd2d, vd2d, revb, table, sinks, gq2, seg):
    W = WINDOW
    R = ATT_BLOCKS * W
    ns = SEQ // R
    cur = lambda b, i: (b * ns + i, 0)
    prev = lambda b, i: (b * (SEQ // W) + jnp.maximum(ATT_BLOCKS * i - 1, 0), 0)
    smem = pl.BlockSpec(memory_space=pltpu.SMEM)
    return pl.pallas_call(
        _attn_prompt_kernel,
        grid=(BATCH, ns),
        in_specs=[pl.BlockSpec((R, D_MODEL), cur),
                  pl.BlockSpec((W, 2 * KD), prev), pl.BlockSpec((R, 2 * KD), cur),
                  pl.BlockSpec((W, 2 * KD), prev), pl.BlockSpec((R, 2 * KD), cur),
                  _const_spec((1, LANES)), smem, smem, _const_spec((1, LANES)),
                  _const_spec((LANES, LANES))],
        out_specs=pl.BlockSpec((R, D_MODEL), cur),
        out_shape=jax.ShapeDtypeStruct((BATCH * SEQ, D_MODEL), _BF),
        scratch_shapes=[pltpu.VMEM((2, N_HEADS, W, 2 * W), _F32)],
        compiler_params=_params(2),
        name="attn_prompt",
    )(q2d, kd2d, kd2d, vd2d, vd2d, revb, table, sinks, gq2, seg)


def _step_logits(q, kb, bias_all, gqs):
    n = DEC_SEQ
    low = _low_lanes(n)
    zero = jnp.zeros((n, LANES), _F32)
    rows = []
    for hd in range(N_HEADS):
        g = hd // GROUP
        x = q[:, (hd // 2) * LANES:(hd // 2 + 1) * LANES]
        if hd % 2 != g % 2:
            x = pltpu.roll(x, HEAD_DIM, 1)
        x = jnp.where(low if g % 2 == 0 else jnp.logical_not(low), x, 0.0)
        ss = jnp.sum(x * x, axis=-1, keepdims=True) * (1.0 / HEAD_DIM)
        x = x * lax.rsqrt(ss + EPS) * gqs
        rows.append(jnp.concatenate([x, zero] if g // 2 == 0 else [zero, x], axis=1))
    q_all = jnp.concatenate(rows, axis=0).astype(_BF)
    return _dot_t(q_all, kb) + bias_all


def _split_heads(o_all):
    n = DEC_SEQ
    low = _low_lanes(n)
    outs = []
    for c in range(N_HEADS // 2):
        halves = []
        for hf in range(2):
            hd = 2 * c + hf
            g = hd // GROUP
            y = o_all[hd * n:(hd + 1) * n, (g // 2) * LANES:(g // 2 + 1) * LANES]
            halves.append(y if g % 2 == hf else pltpu.roll(y, HEAD_DIM, 1))
        outs.append(jnp.where(low, halves[0], halves[1]))
    return jnp.concatenate(outs, axis=1)


def _attn_sample_kernel(q_ref, kn_ref, vn_ref, ck_ref, cv_ref, revb_ref, table_ref,
                        sinks_ref, gq_ref, o_ref, ko_ref, vo_ref, bias_ref, kbuf_ref, vbuf_ref):
    W = WINDOW
    L = 2 * W
    U = ATT_UNROLL

    @pl.when(pl.program_id(0) == 0)
    def _():
        for hd in range(N_HEADS):
            bias_ref[hd * DEC_SEQ:(hd + 1) * DEC_SEQ, :] = _bias_rows(revb_ref, table_ref, hd,
                                                                      DEC_SEQ)
        kbuf_ref[...] = jnp.zeros((U, L, KD), _F32)
        vbuf_ref[...] = jnp.zeros((U, L, KD), _F32)

    gqs = gq_ref[...] * (SCALE * LOG2E)
    bias_all = bias_ref[...]
    sink_col = jnp.concatenate(
        [jnp.full((DEC_SEQ, 1), sinks_ref[hd] * LOG2E, _F32) for hd in range(N_HEADS)], axis=0)

    def body(it, carry):
        logits = []
        for u in range(U):
            s = it * U + u
            kbuf_ref[u, 0:W, :] = ck_ref[s]
            vbuf_ref[u, 0:W, :] = cv_ref[s]
            kbuf_ref[u, W:W + DEC_SEQ, :] = kn_ref[s]
            vbuf_ref[u, W:W + DEC_SEQ, :] = vn_ref[s]
            ko_ref[s] = kbuf_ref[u, DEC_SEQ:DEC_SEQ + W, :]
            vo_ref[s] = vbuf_ref[u, DEC_SEQ:DEC_SEQ + W, :]
            logits.append(_step_logits(q_ref[s], kbuf_ref[u].astype(_BF), bias_all, gqs))
        probs = [_sink_softmax2(l, sink_col).astype(_BF) for l in logits]
        for u in range(U):
            o_all = jnp.dot(probs[u], vbuf_ref[u].astype(_BF), preferred_element_type=_F32)
            o_ref[it * U + u] = _split_heads(o_all)
        return carry

    lax.fori_loop(0, ATT_SEQS // U, body, 0)


def _attn_sample(q3d, kn3d, vn3d, ck, cv, revb, table, sinks, gq2):
    W = WINDOW
    A = ATT_SEQS
    blk = lambda r, c: pl.BlockSpec((A, r, c), lambda i: (i, 0, 0))
    smem = pl.BlockSpec(memory_space=pltpu.SMEM)
    return pl.pallas_call(
        _attn_sample_kernel,
        grid=(DEC_BATCH // A,),
        in_specs=[blk(DEC_SEQ, D_MODEL), blk(DEC_SEQ, KD), blk(DEC_SEQ, KD),
                  blk(W, KD), blk(W, KD),
                  _const_spec((1, LANES)), smem, smem, _const_spec((1, LANES))],
        out_specs=[blk(DEC_SEQ, D_MODEL), blk(W, KD), blk(W, KD)],
        out_shape=[jax.ShapeDtypeStruct((DEC_BATCH, DEC_SEQ, D_MODEL), _F32),
                   jax.ShapeDtypeStruct((DEC_BATCH, W, KD), _F32),
                   jax.ShapeDtypeStruct((DEC_BATCH, W, KD), _F32)],
        scratch_shapes=[pltpu.VMEM((N_HEADS * DEC_SEQ, 2 * W), _F32),
                        pltpu.VMEM((ATT_UNROLL, 2 * W, KD), _F32),
                        pltpu.VMEM((ATT_UNROLL, 2 * W, KD), _F32)],
        compiler_params=_params(1),
        name="attn_sample",
    )(q3d, kn3d, vn3d, ck, cv, revb, table, sinks, gq2)


def kernel(x_prompt, x_sample, state_conv, state_ffn, cache_k, cache_v, p_prompt, p_sample, g_mix, cm_w_pw1, cm_b_pw1, cm_w_dw, cm_b_dw, cm_ln_g, cm_ln_b, cm_w_pw2, cm_b_pw2, at_w_q, at_g_q, at_sinks, at_w_o, kv_g, kv_w_k, kv_w_v, kv_g_k, rel_bias, g_ffn, ffn_w_up, ffn_w_dw, ffn_b_dw, ffn_w_down, g_ple, ple_w_gate, ple_w_proj):
    D, F = D_MODEL, D_FF
    row = lambda a: a.reshape(1, -1)
    bf = lambda a: a.astype(_BF)

    mixer_w = [row(g_mix[0]), bf(cm_w_pw1[0]), row(cm_b_pw1[0]), cm_w_dw[0], row(cm_b_dw[0]),
               row(cm_ln_g[0]), row(cm_ln_b[0]), bf(cm_w_pw2[0]), row(cm_b_pw2[0])]
    cast_srcs = [ffn_w_up.reshape(2 * D, 2 * F), ffn_w_down.reshape(2 * F, D),
                 ple_w_gate.reshape(2 * D, D), ple_w_proj.reshape(2 * PLE_DIM, D)]

    xp = x_prompt.reshape(BATCH * SEQ, D)
    pp = p_prompt.reshape(2, BATCH * SEQ, PLE_DIM)
    h1, conv_p, w_up_bf, w_down_bf, w_gate_bf, w_proj_bf = _mixer_prompt(xp, mixer_w, cast_srcs)

    stacked = dict(
        g_ffn=g_ffn[:, None, :], w_up=w_up_bf.reshape(2, D, 2 * F), w_dw=ffn_w_dw,
        b_dw=ffn_b_dw[:, None, :], w_down=w_down_bf.reshape(2, F, D), g_ple=g_ple[:, None, :],
        w_gate=w_gate_bf.reshape(2, D, D), w_proj=w_proj_bf.reshape(2, PLE_DIM, D))
    ffn_w = [dict(stacked), dict(stacked)]
    ffn_w[0].update(g_kv=row(kv_g), w_k=bf(kv_w_k), w_v=bf(kv_w_v),
                    g_k2=row(jnp.tile(kv_g_k, 2)), g_q_in=row(g_mix[1]), w_q=bf(at_w_q[0]))
    ffn_w[1].update(wo=bf(at_w_o[0]))
    gq2 = row(jnp.tile(at_g_q[0], 2))
    sinks = at_sinks[0]

    W = WINDOW
    revb = jnp.asarray(_t5_buckets(W - 1 - np.arange(W))[None, :])
    half = np.arange(LANES) // HEAD_DIM
    seg = jnp.asarray(half[:, None] == half[None, :], dtype=_BF)

    h3, ffn_p0, k_p, v_p, q_p, kd_p, vd_p = _ffn_call(
        h1, pp, None, None, ffn_w[0], layer=0, prompt=True, with_oproj=False, with_kvq=True)
    o_p = _attn_prompt(q_p, kd_p, vd_p, revb, rel_bias, sinks, gq2, seg)
    y_p, ffn_p = _ffn_call(h3, pp, None, o_p, ffn_w[1], layer=1, prompt=True,
                           with_oproj=True, with_kvq=False, prev_state=ffn_p0)

    ps = p_sample.reshape(2, DEC_BATCH * DEC_SEQ, PLE_DIM)
    g1, conv_s_tm = _mixer_sample(x_sample.reshape(DEC_BATCH * DEC_SEQ, D),
                                  jnp.transpose(state_conv[0], (1, 0, 2)), mixer_w)
    conv_s = jnp.transpose(conv_s_tm, (1, 0, 2))
    g3, ffn_s0, k_s, v_s, q_s = _ffn_call(g1, ps, state_ffn, None, ffn_w[0], layer=0,
                                          prompt=False, with_oproj=False, with_kvq=True)
    o_s, k_out, v_out = _attn_sample(
        q_s.reshape(DEC_BATCH, DEC_SEQ, D), k_s.reshape(DEC_BATCH, DEC_SEQ, KD),
        v_s.reshape(DEC_BATCH, DEC_SEQ, KD), cache_k.reshape(DEC_BATCH, W, KD),
        cache_v.reshape(DEC_BATCH, W, KD), revb, rel_bias, sinks, gq2)
    y_s, ffn_s = _ffn_call(g3, ps, state_ffn, o_s.reshape(DEC_BATCH * DEC_SEQ, D),
                           ffn_w[1], layer=1, prompt=False, with_oproj=True, with_kvq=False,
                           prev_state=ffn_s0)

    kv_shape_p = (BATCH, W, N_KV_HEADS, HEAD_DIM)
    kv_shape_s = (DEC_BATCH, W, N_KV_HEADS, HEAD_DIM)
    return (y_p.reshape(BATCH, SEQ, D), y_s.reshape(DEC_BATCH, DEC_SEQ, D),
            conv_p[None], conv_s[None],
            ffn_p, ffn_s,
            k_p.reshape(BATCH, SEQ, KD)[:, SEQ - W:].reshape(kv_shape_p),
            k_out.reshape(kv_shape_s),
            v_p.reshape(BATCH, SEQ, KD)[:, SEQ - W:].reshape(kv_shape_p),
            v_out.reshape(kv_shape_s))
```

```python
import functools
import math

import jax
import jax.numpy as jnp
import numpy as np
from jax import lax
from jax.experimental import pallas as pl
from jax.experimental.pallas import tpu as pltpu

D_MODEL = 1024
BATCH = 8
SEQ = 2048
DEC_BATCH = 128
DEC_SEQ = 8
CONV_K = 31
FFN_CONV_K = 3
D_FF = 2816
N_HEADS = 16
N_KV_HEADS = 4
HEAD_DIM = 64
GROUP = N_HEADS // N_KV_HEADS
WINDOW = 128
N_BUCKETS = 32
MAX_DISTANCE = 128
PLE_DIM = 256
EPS = 1e-6
SCALE = HEAD_DIM ** -0.5
NEG = -1e30
KD = N_KV_HEADS * HEAD_DIM

LANES = 128
SUBLANES = 8
VMEM_LIMIT = 60 * 1024 * 1024

T_TILE = 512
MIX_TILE = 1024
S_TILE = 32
CONV_HALO = 32
FFN_HALO = 8
F_CHUNK = 2816
D_SLABS = D_MODEL // LANES
F_SLABS = D_FF // LANES
ATT_SEQS = 16
ATT_UNROLL = 8
ATT_BLOCKS = 2

_BF = jnp.bfloat16
_F32 = jnp.float32


def _bdot(a, w):
    return jnp.dot(a.astype(_BF), w, preferred_element_type=_F32)


def _rms(x, g):
    return x * lax.rsqrt(jnp.mean(x * x, axis=-1, keepdims=True) + EPS) * g


def _sigmoid(x):
    return 1.0 / (1.0 + jnp.exp(-x))


def _const_spec(shape):
    nd = len(shape)
    return pl.BlockSpec(shape, lambda *_: (0,) * nd, pipeline_mode=pl.Buffered(1))


def _layer_spec(shape, layer):
    nd = len(shape)
    return pl.BlockSpec((None,) + tuple(shape), lambda *_: (layer,) + (0,) * nd,
                        pipeline_mode=pl.Buffered(1))


def _params(n_grid):
    return pltpu.CompilerParams(dimension_semantics=("arbitrary",) * n_grid,
                                vmem_limit_bytes=VMEM_LIMIT)


def _mixer_tail(x, c, bdw, lng, lnb, w2, b2):
    c = c + bdw
    mu = jnp.mean(c, axis=-1, keepdims=True)
    cz = c - mu
    var = jnp.mean(cz * cz, axis=-1, keepdims=True)
    y = cz * lax.rsqrt(var + EPS) * lng + lnb
    s = y * _sigmoid(y)
    return x + _bdot(s, w2) + b2


def _mixer_prompt_kernel(x_ref, g_ref, w1_ref, b1_ref, wdw_ref, bdw_ref, lng_ref, lnb_ref,
                         w2_ref, b2_ref, *rest):
    n_cast = len(_CAST_ROWS)
    cast_in, rest = rest[:n_cast], rest[n_cast:]
    h_ref, cs_ref = rest[0], rest[1]
    cast_out, (slab_ref, oslab_ref) = rest[2:2 + n_cast], rest[2 + n_cast:]
    T = MIX_TILE
    H = T // 2
    j = pl.program_id(1)

    for src, dst in zip(cast_in, cast_out):
        dst[...] = src[...].astype(_BF)

    @pl.when(j == 0)
    def _():
        slab_ref[:, 0:CONV_HALO, :] = jnp.zeros((D_SLABS, CONV_HALO, LANES), _F32)

    x = x_ref[...]
    u = _bdot(_rms(x, g_ref[...]), w1_ref[...]) + b1_ref[...]
    glu = u[:, :D_MODEL] * _sigmoid(u[:, D_MODEL:])
    for c in range(D_SLABS):
        slab_ref[c, CONV_HALO:CONV_HALO + T, :] = glu[:, c * LANES:(c + 1) * LANES]
    cs_ref[...] = glu[T - (CONV_K - 1):, :]

    base = CONV_HALO - (CONV_K - 1)
    for c in range(D_SLABS):
        wc = wdw_ref[:, c * LANES:(c + 1) * LANES]
        for p in range(2):
            acc = slab_ref[c, pl.ds(base + p, H, stride=2), :] * wc[0:1, :]
            for k in range(1, CONV_K):
                acc = acc + slab_ref[c, pl.ds(base + k + p, H, stride=2), :] * wc[k:k + 1, :]
            oslab_ref[c, pl.ds(p, H, stride=2), :] = acc
    conv = jnp.concatenate([oslab_ref[c] for c in range(D_SLABS)], axis=1)
    h_ref[...] = _mixer_tail(x, conv, bdw_ref[...], lng_ref[...], lnb_ref[...],
                             w2_ref[...], b2_ref[...])
    for c in range(D_SLABS):
        slab_ref[c, 0:CONV_HALO, :] = slab_ref[c, T:T + CONV_HALO, :]


def _mixer_sample_kernel(x_ref, st_ref, g_ref, w1f_ref, b1_ref, wdw_ref, bdw_ref, lng_ref,
                         lnb_ref, w2f_ref, b2_ref, h_ref, cs_ref, w1_ref, w2_ref):
    S = S_TILE
    KH = CONV_K - 1

    @pl.when(pl.program_id(0) == 0)
    def _():
        w1_ref[...] = w1f_ref[...].astype(_BF)
        w2_ref[...] = w2f_ref[...].astype(_BF)

    x = x_ref[...]
    u = _bdot(_rms(x, g_ref[...]), w1_ref[...]) + b1_ref[...]
    glu = u[:, :D_MODEL] * _sigmoid(u[:, D_MODEL:])
    glu_tm = jnp.swapaxes(glu.reshape(S, DEC_SEQ, D_MODEL), 0, 1)
    cs_ref[0:KH - DEC_SEQ] = st_ref[DEC_SEQ:KH]
    cs_ref[KH - DEC_SEQ:KH] = glu_tm

    def plane(j):
        return st_ref[j] if j < KH else glu_tm[j - KH]

    outs = []
    for t in range(DEC_SEQ):
        acc = plane(t) * wdw_ref[0:1, :]
        for k in range(1, CONV_K):
            acc = acc + plane(t + k) * wdw_ref[k:k + 1, :]
        outs.append(acc)
    conv = jnp.swapaxes(jnp.stack(outs, axis=0), 0, 1).reshape(S * DEC_SEQ, D_MODEL)
    h_ref[...] = _mixer_tail(x, conv, bdw_ref[...], lng_ref[...], lnb_ref[...],
                             w2_ref[...], b2_ref[...])


def _mixer_weight_specs():
    D = D_MODEL
    return [_const_spec((1, D)), _const_spec((D, 2 * D)), _const_spec((1, 2 * D)),
            _const_spec((CONV_K, D)), _const_spec((1, D)), _const_spec((1, D)),
            _const_spec((1, D)), _const_spec((D, D)), _const_spec((1, D))]


_CAST_ROWS = (2 * D_MODEL, 2 * D_FF, 2 * D_MODEL, 2 * PLE_DIM) + (D_MODEL,) * 4


def _mixer_prompt(x2d, wts, cast_srcs):
    T = MIX_TILE
    nt = SEQ // T
    steps = BATCH * nt
    row = lambda b, j: (b * nt + j, 0)
    cast_specs, cast_shapes = [], []
    for src, rows in zip(cast_srcs, _CAST_ROWS):
        assert src.shape[0] == rows and rows % steps == 0
        cast_specs.append(pl.BlockSpec((rows // steps, src.shape[1]), row))
        cast_shapes.append(jax.ShapeDtypeStruct(src.shape, _BF))
    return pl.pallas_call(
        _mixer_prompt_kernel,
        grid=(BATCH, nt),
        in_specs=[pl.BlockSpec((T, D_MODEL), row)] + _mixer_weight_specs() + cast_specs,
        out_specs=[pl.BlockSpec((T, D_MODEL), row),
                   pl.BlockSpec((None, CONV_K - 1, D_MODEL), lambda b, j: (b, 0, 0))] + cast_specs,
        out_shape=[jax.ShapeDtypeStruct((BATCH * SEQ, D_MODEL), _F32),
                   jax.ShapeDtypeStruct((BATCH, CONV_K - 1, D_MODEL), _F32)] + cast_shapes,
        scratch_shapes=[pltpu.VMEM((D_SLABS, CONV_HALO + T, LANES), _F32),
                        pltpu.VMEM((D_SLABS, T, LANES), _F32)],
        compiler_params=_params(2),
        name="mixer_prompt",
    )(x2d, *wts, *cast_srcs)


def _mixer_sample(x2d, state_tm, wts):
    S = S_TILE
    R = S * DEC_SEQ
    rows = pl.BlockSpec((R, D_MODEL), lambda i: (i, 0))
    planes = pl.BlockSpec((CONV_K - 1, S, D_MODEL), lambda i: (0, i, 0))
    w1_shape, w2_shape = (D_MODEL, 2 * D_MODEL), (D_MODEL, D_MODEL)
    whole = lambda shape: pl.BlockSpec(shape, lambda i: (0, 0))
    return pl.pallas_call(
        _mixer_sample_kernel,
        grid=(DEC_BATCH // S,),
        in_specs=[rows, planes] + _mixer_weight_specs(),
        out_specs=[rows, planes, whole(w1_shape), whole(w2_shape)],
        out_shape=[jax.ShapeDtypeStruct((DEC_BATCH * DEC_SEQ, D_MODEL), _F32),
                   jax.ShapeDtypeStruct((CONV_K - 1, DEC_BATCH, D_MODEL), _F32),
                   jax.ShapeDtypeStruct(w1_shape, _BF), jax.ShapeDtypeStruct(w2_shape, _BF)],
        compiler_params=_params(1),
        name="mixer_sample",
    )(x2d, state_tm, *wts)


def _ffn_kernel(*refs, prompt, with_oproj, with_kvq):
    it = iter(refs)
    h_ref = next(it)
    p_ref = next(it)
    st_ref = None if prompt else next(it)
    if with_oproj:
        o_ref = next(it)
        wo_ref = next(it)
        fs0_ref = next(it)
    gf_ref, wup_ref, wdw_ref, bdw_ref, wdn_ref = (next(it) for _ in range(5))
    gp_ref, wg_ref, wp_ref = (next(it) for _ in range(3))
    if with_kvq:
        gkv_ref, wk_ref, wv_ref, gk_ref, gq_ref, wq_ref = (next(it) for _ in range(6))
    hout_ref = next(it)
    fs_ref = next(it)
    dup_kv = with_kvq and prompt
    if with_kvq:
        k_ref, v_ref, q_ref = (next(it) for _ in range(3))
    if dup_kv:
        kd_ref, vd_ref = next(it), next(it)
    if prompt:
        slab_ref, oslab_ref = next(it), next(it)
    else:
        win_ref = next(it)

    F = D_FF
    R = T_TILE if prompt else S_TILE * DEC_SEQ
    H = R // 2
    KH = FFN_CONV_K - 1

    if prompt:
        @pl.when(pl.program_id(1) == 0)
        def _():
            slab_ref[:, 0:FFN_HALO, :] = jnp.zeros((2 * F_SLABS, FFN_HALO, LANES), _F32)
    else:
        win_ref[:, FFN_HALO - KH:FFN_HALO, :] = st_ref[...]

    h = h_ref[...]
    if with_oproj:
        h = h + _bdot(o_ref[...], wo_ref[...])
    xn = _rms(h, gf_ref[...]).astype(_BF)

    n_chunks = F // F_CHUNK
    spc = F_CHUNK // LANES
    acc = None
    for ci in range(n_chunks):
        halves = []
        for half in range(2):
            c0 = half * F + ci * F_CHUNK
            uc = jnp.dot(xn, wup_ref[:, c0:c0 + F_CHUNK], preferred_element_type=_F32)
            w = wdw_ref[:, c0:c0 + F_CHUNK]
            b = bdw_ref[:, c0:c0 + F_CHUNK]
            if prompt:
                cols = []
                for s in range(spc):
                    sl = half * F_SLABS + ci * spc + s
                    slab_ref[sl, FFN_HALO:FFN_HALO + R, :] = uc[:, s * LANES:(s + 1) * LANES]
                    ws = w[:, s * LANES:(s + 1) * LANES]
                    ph = []
                    for p in range(2):
                        cv = b[:, s * LANES:(s + 1) * LANES]
                        for k in range(FFN_CONV_K):
                            cv = cv + (slab_ref[sl, pl.ds(FFN_HALO - KH + k + p, H, stride=2), :]
                                       * ws[k:k + 1, :])
                        ph.append(cv)
                    cols.append(jnp.concatenate(ph, axis=0))
                halves.append(jnp.concatenate(cols, axis=1))
            else:
                win_ref[:, FFN_HALO:FFN_HALO + DEC_SEQ, c0:c0 + F_CHUNK] = uc.reshape(
                    S_TILE, DEC_SEQ, F_CHUNK)
                cv = b
                for k in range(FFN_CONV_K):
                    o = FFN_HALO - KH + k
                    cv = cv + win_ref[:, o:o + DEC_SEQ, c0:c0 + F_CHUNK] * w[k:k + 1, :]
                halves.append(cv.reshape(R, F_CHUNK))
        gate, val = halves
        act = (gate * _sigmoid(gate) * val).astype(_BF)
        part = jnp.dot(act, wdn_ref[ci * F_CHUNK:(ci + 1) * F_CHUNK, :],
                       preferred_element_type=_F32)
        acc = part if acc is None else acc + part

    if prompt:
        for c in range(D_SLABS):
            oslab_ref[c, pl.ds(0, H, stride=2), :] = acc[:H, c * LANES:(c + 1) * LANES]
            oslab_ref[c, pl.ds(1, H, stride=2), :] = acc[H:, c * LANES:(c + 1) * LANES]
        f = jnp.concatenate([oslab_ref[c] for c in range(D_SLABS)], axis=1)
        new_state = jnp.concatenate(
            [slab_ref[s, FFN_HALO + R - SUBLANES:FFN_HALO + R, :] for s in range(2 * F_SLABS)],
            axis=1)[SUBLANES - KH:, :]
        for s in range(2 * F_SLABS):
            slab_ref[s, 0:FFN_HALO, :] = slab_ref[s, R:R + FFN_HALO, :]
    else:
        f = acc
        new_state = win_ref[:, FFN_HALO + DEC_SEQ - KH:FFN_HALO + DEC_SEQ, :]
    if with_oproj:
        fs_ref[0] = fs0_ref[...]
        fs_ref[1] = new_state
    else:
        fs_ref[...] = new_state

    h2 = h + f
    gate = _sigmoid(_bdot(_rms(h2, gp_ref[...]), wg_ref[...]))
    h3 = h2 + gate * _bdot(p_ref[...], wp_ref[...])
    hout_ref[...] = h3

    if with_kvq:
        hk = _rms(h3, gkv_ref[...]).astype(_BF)
        kk = jnp.dot(hk, wk_ref[...], preferred_element_type=_F32)
        vv = jnp.dot(hk, wv_ref[...], preferred_element_type=_F32)
        v_ref[...] = vv
        low = lax.broadcasted_iota(jnp.int32, (R, LANES), 1) < HEAD_DIM

        def dup_heads(x):
            out = []
            for c in range(KD // LANES):
                xc = x[:, c * LANES:(c + 1) * LANES]
                xr = pltpu.roll(xc, HEAD_DIM, 1)
                out += [jnp.where(low, xc, xr), jnp.where(low, xr, xc)]
            return out

        kn = []
        for kg in dup_heads(kk):
            r = lax.rsqrt(jnp.mean(kg * kg, axis=-1, keepdims=True) + EPS)
            kn.append(kg * r * gk_ref[...])
        k_ref[...] = jnp.concatenate(
            [jnp.where(low, kn[2 * c], kn[2 * c + 1]) for c in range(KD // LANES)], axis=1)
        if dup_kv:
            kd_ref[...] = jnp.concatenate(kn, axis=1).astype(_BF)
            vd_ref[...] = jnp.concatenate(dup_heads(vv), axis=1).astype(_BF)
        q_ref[...] = _bdot(_rms(h3, gq_ref[...]), wq_ref[...])


def _ffn_call(h2d, p3d, state4d, o2d, w, *, layer, prompt, with_oproj, with_kvq,
              prev_state=None):
    D, F = D_MODEL, D_FF
    if prompt:
        R = T_TILE
        nt = SEQ // R
        grid = (BATCH, nt)
        row = lambda b, j: (b * nt + j, 0)
        p_spec = pl.BlockSpec((None, R, PLE_DIM), lambda b, j: (layer, b * nt + j, 0))
        n_rows, n_seq = BATCH * SEQ, BATCH
        fs_spec = pl.BlockSpec((None, FFN_CONV_K - 1, 2 * F), lambda b, j: (b, 0, 0))
        scratch = [pltpu.VMEM((2 * F_SLABS, FFN_HALO + R, LANES), _F32),
                   pltpu.VMEM((D_SLABS, R, LANES), _F32)]
    else:
        R = S_TILE * DEC_SEQ
        grid = (DEC_BATCH // S_TILE,)
        row = lambda i: (i, 0)
        p_spec = pl.BlockSpec((None, R, PLE_DIM), lambda i: (layer, i, 0))
        n_rows, n_seq = DEC_BATCH * DEC_SEQ, DEC_BATCH
        fs_spec = pl.BlockSpec((S_TILE, FFN_CONV_K - 1, 2 * F), lambda i: (i, 0, 0))
        scratch = [pltpu.VMEM((S_TILE, FFN_HALO + DEC_SEQ, 2 * F), _F32)]

    args = [h2d, p3d]
    in_specs = [pl.BlockSpec((R, D), row), p_spec]
    if not prompt:
        args.append(state4d)
        in_specs.append(pl.BlockSpec((None, S_TILE, FFN_CONV_K - 1, 2 * F),
                                     lambda i: (layer, i, 0, 0)))
    fs_shape = (n_seq, FFN_CONV_K - 1, 2 * F)
    if with_oproj:
        args += [o2d, w["wo"], prev_state]
        in_specs += [pl.BlockSpec((R, D), row), _const_spec((D, D)), fs_spec]
        blk, imap = fs_spec.block_shape, fs_spec.index_map
        fs_spec = pl.BlockSpec((2,) + tuple(blk), lambda *g: (0,) + tuple(imap(*g)))
        fs_shape = (2,) + fs_shape
    args += [w["g_ffn"], w["w_up"], w["w_dw"], w["b_dw"], w["w_down"],
             w["g_ple"], w["w_gate"], w["w_proj"]]
    in_specs += [_layer_spec((1, D), layer), _layer_spec((D, 2 * F), layer),
                 _layer_spec((FFN_CONV_K, 2 * F), layer), _layer_spec((1, 2 * F), layer),
                 _layer_spec((F, D), layer), _layer_spec((1, D), layer),
                 _layer_spec((D, D), layer), _layer_spec((PLE_DIM, D), layer)]
    out_specs = [pl.BlockSpec((R, D), row), fs_spec]
    out_shape = [jax.ShapeDtypeStruct((n_rows, D), _F32),
                 jax.ShapeDtypeStruct(fs_shape, _F32)]
    if with_kvq:
        args += [w["g_kv"], w["w_k"], w["w_v"], w["g_k2"], w["g_q_in"], w["w_q"]]
        in_specs += [_const_spec((1, D)), _const_spec((D, KD)), _const_spec((D, KD)),
                     _const_spec((1, LANES)), _const_spec((1, D)), _const_spec((D, D))]
        out_specs += [pl.BlockSpec((R, KD), row), pl.BlockSpec((R, KD), row),
                      pl.BlockSpec((R, D), row)]
        out_shape += [jax.ShapeDtypeStruct((n_rows, KD), _F32),
                      jax.ShapeDtypeStruct((n_rows, KD), _F32),
                      jax.ShapeDtypeStruct((n_rows, D), _F32)]
        if prompt:
            out_specs += [pl.BlockSpec((R, 2 * KD), row), pl.BlockSpec((R, 2 * KD), row)]
            out_shape += [jax.ShapeDtypeStruct((n_rows, 2 * KD), _BF),
                          jax.ShapeDtypeStruct((n_rows, 2 * KD), _BF)]
    kern = functools.partial(_ffn_kernel, prompt=prompt, with_oproj=with_oproj,
                             with_kvq=with_kvq)
    name = "ffn_%s_%d" % ("prompt" if prompt else "sample", layer)
    return pl.pallas_call(
        kern, grid=grid, in_specs=in_specs, out_specs=out_specs, out_shape=out_shape,
        scratch_shapes=scratch, compiler_params=_params(len(grid)), name=name,
    )(*args)


def _t5_buckets(dist):
    max_exact = N_BUCKETS // 2
    d = np.maximum(dist, 0)
    df = np.maximum(d, 1).astype(np.float32)
    large = max_exact + (np.log(df / np.float32(max_exact))
                         / np.float32(math.log(MAX_DISTANCE / max_exact))
                         * np.float32(N_BUCKETS - max_exact)).astype(np.int32)
    large = np.minimum(large, N_BUCKETS - 1)
    return np.where(d < max_exact, d, large).astype(np.int32)


LOG2E = math.log2(math.e)
NEG2 = NEG * LOG2E


def _low_lanes(rows):
    return lax.broadcasted_iota(jnp.int32, (rows, LANES), 1) < HEAD_DIM


def _bias_rows(revb_ref, table_ref, hd, nq):
    revb = revb_ref[...]
    row = jnp.zeros(revb.shape, _F32)
    for bk in range(N_BUCKETS):
        row = jnp.where(revb == bk, table_ref[bk, hd] * LOG2E, row)
    base = jnp.concatenate([jnp.broadcast_to(row, (nq, LANES)),
                            jnp.full((nq, LANES), NEG2, _F32)], axis=1)
    return pltpu.roll(base, 1, 1, stride=1, stride_axis=0)


def _norm_heads(q, seg, gqs):
    n = q.shape[0]
    nc = D_MODEL // LANES
    sq = jnp.concatenate([q[:, c * LANES:(c + 1) * LANES] for c in range(nc)], axis=0)
    ss = jnp.dot((sq * sq).astype(_BF), seg, preferred_element_type=_F32)
    r = lax.rsqrt(ss * (1.0 / HEAD_DIM) + EPS)
    return [q[:, c * LANES:(c + 1) * LANES] * r[c * n:(c + 1) * n, :] * gqs for c in range(nc)]


def _sink_softmax2(logits2, sink2):
    m = jnp.maximum(jnp.max(logits2, axis=-1, keepdims=True), sink2)
    e = jnp.exp2(logits2 - m)
    denom = jnp.sum(e, axis=-1, keepdims=True) + jnp.exp2(sink2 - m)
    return e * (1.0 / denom)


def _dot_t(a, b):
    return lax.dot_general(a, b, (((1,), (1,)), ((), ())), preferred_element_type=_F32)


def _attend_blocks(blocks, sinks_ref, seg, gqs):
    units = []
    for (q, kd, vd, bias_of) in blocks:
        nq = q.shape[0]
        low = _low_lanes(nq)
        qn = _norm_heads(q, seg, gqs)
        for g in range(N_KV_HEADS):
            qs = jnp.concatenate(
                [jnp.where(low if hl % 2 == 0 else jnp.logical_not(low), qn[2 * g + hl // 2], 0.0)
                 for hl in range(GROUP)], axis=0).astype(_BF)
            units.append((g, nq, bias_of, _dot_t(qs, kd[:, g * LANES:(g + 1) * LANES]),
                          vd[:, g * LANES:(g + 1) * LANES]))
    probs = []
    for (g, nq, bias_of, s, _) in units:
        probs.append(jnp.concatenate(
            [_sink_softmax2(s[hl * nq:(hl + 1) * nq, :] + bias_of(g * GROUP + hl),
                            sinks_ref[g * GROUP + hl] * LOG2E) for hl in range(GROUP)],
            axis=0).astype(_BF))
    res, outs = [], []
    for (g, nq, _, _, vg), p in zip(units, probs):
        low_k = _low_lanes(vg.shape[0])
        zero = jnp.zeros_like(vg)
        vlo, vhi = jnp.where(low_k, vg, zero), jnp.where(low_k, zero, vg)
        for pair in range(2):
            r0 = 2 * pair * nq
            outs.append(jnp.dot(p[r0:r0 + nq], vlo, preferred_element_type=_F32)
                        + jnp.dot(p[r0 + nq:r0 + 2 * nq], vhi, preferred_element_type=_F32))
        if g == N_KV_HEADS - 1:
            res.append(jnp.concatenate(outs, axis=1))
            outs = []
    return res


def _attn_prompt_kernel(q_ref, kp_ref, kc_ref, vp_ref, vc_ref, revb_ref, table_ref,
                        sinks_ref, gq_ref, seg_ref, o_ref, bias_ref):
    W = WINDOW
    i = pl.program_id(1)

    @pl.when(jnp.logical_and(pl.program_id(0) == 0, i == 0))
    def _():
        first_half = lax.broadcasted_iota(jnp.int32, (W, 2 * W), 1) < W
        for hd in range(N_HEADS):
            rows = _bias_rows(revb_ref, table_ref, hd, W)
            bias_ref[0, hd] = rows
            bias_ref[1, hd] = jnp.where(first_half, NEG2, rows)

    gqs = gq_ref[...] * (SCALE * LOG2E)
    kp, kc = kp_ref[...], kc_ref[...]
    vp, vc = vp_ref[...], vc_ref[...]
    first = jnp.where(i == 0, 1, 0)
    blocks = []
    for blk in range(ATT_BLOCKS):
        if blk == 0:
            kd = jnp.concatenate([kp, kc[0:W]], axis=0)
            vd = jnp.concatenate([vp, vc[0:W]], axis=0)
            bias_of = lambda hd: bias_ref[first, hd]
        else:
            kd, vd = kc[(blk - 1) * W:(blk + 1) * W], vc[(blk - 1) * W:(blk + 1) * W]
            bias_of = lambda hd: bias_ref[0, hd]
        blocks.append((q_ref[blk * W:(blk + 1) * W, :], kd, vd, bias_of))
    for blk, o in enumerate(_attend_blocks(blocks, sinks_ref, seg_ref[...], gqs)):
        o_ref[blk * W:(blk + 1) * W, :] = o.astype(o_ref.dtype)


def _attn_prompt(q2d, kd2d, vd2d, revb, table, sinks, gq2, seg):
    W = WINDOW
    R = ATT_BLOCKS * W
    ns = SEQ // R
    cur = lambda b, i: (b * ns + i, 0)
    prev = lambda b, i: (b * (SEQ // W) + jnp.maximum(ATT_BLOCKS * i - 1, 0), 0)
    smem = pl.BlockSpec(memory_space=pltpu.SMEM)
    return pl.pallas_call(
        _attn_prompt_kernel,
        grid=(BATCH, ns),
        in_specs=[pl.BlockSpec((R, D_MODEL), cur),
                  pl.BlockSpec((W, 2 * KD), prev), pl.BlockSpec((R, 2 * KD), cur),
                  pl.BlockSpec((W, 2 * KD), prev), pl.BlockSpec((R, 2 * KD), cur),
                  _const_spec((1, LANES)), smem, smem, _const_spec((1, LANES)),
                  _const_spec((LANES, LANES))],
        out_specs=pl.BlockSpec((R, D_MODEL), cur),
        out_shape=jax.ShapeDtypeStruct((BATCH * SEQ, D_MODEL), _BF),
        scratch_shapes=[pltpu.VMEM((2, N_HEADS, W, 2 * W), _F32)],
        compiler_params=_params(2),
        name="attn_prompt",
    )(q2d, kd2d, kd2d, vd2d, vd2d, revb, table, sinks, gq2, seg)


def _step_logits(q, kb, bias_all, gqs):
    n = DEC_SEQ
    low = _low_lanes(n)
    zero = jnp.zeros((n, LANES), _F32)
    rows = []
    for hd in range(N_HEADS):
        g = hd // GROUP
        x = q[:, (hd // 2) * LANES:(hd // 2 + 1) * LANES]
        if hd % 2 != g % 2:
            x = pltpu.roll(x, HEAD_DIM, 1)
        x = jnp.where(low if g % 2 == 0 else jnp.logical_not(low), x, 0.0)
        ss = jnp.sum(x * x, axis=-1, keepdims=True) * (1.0 / HEAD_DIM)
        x = x * lax.rsqrt(ss + EPS) * gqs
        rows.append(jnp.concatenate([x, zero] if g // 2 == 0 else [zero, x], axis=1))
    q_all = jnp.concatenate(rows, axis=0).astype(_BF)
    return _dot_t(q_all, kb) + bias_all


def _split_heads(o_all):
    n = DEC_SEQ
    low = _low_lanes(n)
    outs = []
    for c in range(N_HEADS // 2):
        halves = []
        for hf in range(2):
            hd = 2 * c + hf
            g = hd // GROUP
            y = o_all[hd * n:(hd + 1) * n, (g // 2) * LANES:(g // 2 + 1) * LANES]
            halves.append(y if g % 2 == hf else pltpu.roll(y, HEAD_DIM, 1))
        outs.append(jnp.where(low, halves[0], halves[1]))
    return jnp.concatenate(outs, axis=1)


def _attn_sample_kernel(q_ref, kn_ref, vn_ref, ck_ref, cv_ref, revb_ref, table_ref,
                        sinks_ref, gq_ref, o_ref, ko_ref, vo_ref, bias_ref, kbuf_ref, vbuf_ref):
    W = WINDOW
    L = 2 * W
    U = ATT_UNROLL

    @pl.when(pl.program_id(0) == 0)
    def _():
        for hd in range(N_HEADS):
            bias_ref[hd * DEC_SEQ:(hd + 1) * DEC_SEQ, :] = _bias_rows(revb_ref, table_ref, hd,
                                                                      DEC_SEQ)
        kbuf_ref[...] = jnp.zeros((U, L, KD), _F32)
        vbuf_ref[...] = jnp.zeros((U, L, KD), _F32)

    gqs = gq_ref[...] * (SCALE * LOG2E)
    bias_all = bias_ref[...]
    sink_col = jnp.concatenate(
        [jnp.full((DEC_SEQ, 1), sinks_ref[hd] * LOG2E, _F32) for hd in range(N_HEADS)], axis=0)

    def body(it, carry):
        logits = []
        for u in range(U):
            s = it * U + u
            kbuf_ref[u, 0:W, :] = ck_ref[s]
            vbuf_ref[u, 0:W, :] = cv_ref[s]
            kbuf_ref[u, W:W + DEC_SEQ, :] = kn_ref[s]
            vbuf_ref[u, W:W + DEC_SEQ, :] = vn_ref[s]
            ko_ref[s] = kbuf_ref[u, DEC_SEQ:DEC_SEQ + W, :]
            vo_ref[s] = vbuf_ref[u, DEC_SEQ:DEC_SEQ + W, :]
            logits.append(_step_logits(q_ref[s], kbuf_ref[u].astype(_BF), bias_all, gqs))
        probs = [_sink_softmax2(l, sink_col).astype(_BF) for l in logits]
        for u in range(U):
            o_all = jnp.dot(probs[u], vbuf_ref[u].astype(_BF), preferred_element_type=_F32)
            o_ref[it * U + u] = _split_heads(o_all)
        return carry

    lax.fori_loop(0, ATT_SEQS // U, body, 0)


def _attn_sample(q3d, kn3d, vn3d, ck, cv, revb, table, sinks, gq2):
    W = WINDOW
    A = ATT_SEQS
    blk = lambda r, c: pl.BlockSpec((A, r, c), lambda i: (i, 0, 0))
    smem = pl.BlockSpec(memory_space=pltpu.SMEM)
    return pl.pallas_call(
        _attn_sample_kernel,
        grid=(DEC_BATCH // A,),
        in_specs=[blk(DEC_SEQ, D_MODEL), blk(DEC_SEQ, KD), blk(DEC_SEQ, KD),
                  blk(W, KD), blk(W, KD),
                  _const_spec((1, LANES)), smem, smem, _const_spec((1, LANES))],
        out_specs=[blk(DEC_SEQ, D_MODEL), blk(W, KD), blk(W, KD)],
        out_shape=[jax.ShapeDtypeStruct((DEC_BATCH, DEC_SEQ, D_MODEL), _F32),
                   jax.ShapeDtypeStruct((DEC_BATCH, W, KD), _F32),
                   jax.ShapeDtypeStruct((DEC_BATCH, W, KD), _F32)],
        scratch_shapes=[pltpu.VMEM((N_HEADS * DEC_SEQ, 2 * W), _F32),
                        pltpu.VMEM((ATT_UNROLL, 2 * W, KD), _F32),
                        pltpu.VMEM((ATT_UNROLL, 2 * W, KD), _F32)],
        compiler_params=_params(1),
        name="attn_sample",
    )(q3d, kn3d, vn3d, ck, cv, revb, table, sinks, gq2)


def kernel(x_prompt, x_sample, state_conv, state_ffn, cache_k, cache_v, p_prompt, p_sample, g_mix, cm_w_pw1, cm_b_pw1, cm_w_dw, cm_b_dw, cm_ln_g, cm_ln_b, cm_w_pw2, cm_b_pw2, at_w_q, at_g_q, at_sinks, at_w_o, kv_g, kv_w_k, kv_w_v, kv_g_k, rel_bias, g_ffn, ffn_w_up, ffn_w_dw, ffn_b_dw, ffn_w_down, g_ple, ple_w_gate, ple_w_proj):
    D, F = D_MODEL, D_FF
    row = lambda a: a.reshape(1, -1)
    bf = lambda a: a.astype(_BF)

    mixer_w = [row(g_mix[0]), cm_w_pw1[0], row(cm_b_pw1[0]), cm_w_dw[0], row(cm_b_dw[0]),
               row(cm_ln_g[0]), row(cm_ln_b[0]), cm_w_pw2[0], row(cm_b_pw2[0])]
    g1, conv_s_tm, w_pw1_bf, w_pw2_bf = _mixer_sample(
        x_sample.reshape(DEC_BATCH * DEC_SEQ, D), jnp.transpose(state_conv[0], (1, 0, 2)), mixer_w)
    conv_s = jnp.transpose(conv_s_tm, (1, 0, 2))
    mixer_w[1], mixer_w[7] = w_pw1_bf, w_pw2_bf
    cast_srcs = [ffn_w_up.reshape(2 * D, 2 * F), ffn_w_down.reshape(2 * F, D),
                 ple_w_gate.reshape(2 * D, D), ple_w_proj.reshape(2 * PLE_DIM, D),
                 at_w_q[0], at_w_o[0], kv_w_k, kv_w_v]

    xp = x_prompt.reshape(BATCH * SEQ, D)
    pp = p_prompt.reshape(2, BATCH * SEQ, PLE_DIM)
    (h1, conv_p, w_up_bf, w_down_bf, w_gate_bf, w_proj_bf, w_q_bf, w_o_bf, w_k_bf,
     w_v_bf) = _mixer_prompt(xp, mixer_w, cast_srcs)

    stacked = dict(
        g_ffn=g_ffn[:, None, :], w_up=w_up_bf.reshape(2, D, 2 * F), w_dw=ffn_w_dw,
        b_dw=ffn_b_dw[:, None, :], w_down=w_down_bf.reshape(2, F, D), g_ple=g_ple[:, None, :],
        w_gate=w_gate_bf.reshape(2, D, D), w_proj=w_proj_bf.reshape(2, PLE_DIM, D))
    ffn_w = [dict(stacked), dict(stacked)]
    ffn_w[0].update(g_kv=row(kv_g), w_k=w_k_bf, w_v=w_v_bf,
                    g_k2=row(jnp.tile(kv_g_k, 2)), g_q_in=row(g_mix[1]), w_q=w_q_bf)
    ffn_w[1].update(wo=w_o_bf)
    gq2 = row(jnp.tile(at_g_q[0], 2))
    sinks = at_sinks[0]

    W = WINDOW
    revb = jnp.asarray(_t5_buckets(W - 1 - np.arange(W))[None, :])
    half = np.arange(LANES) // HEAD_DIM
    seg = jnp.asarray(half[:, None] == half[None, :], dtype=_BF)

    h3, ffn_p0, k_p, v_p, q_p, kd_p, vd_p = _ffn_call(
        h1, pp, None, None, ffn_w[0], layer=0, prompt=True, with_oproj=False, with_kvq=True)
    o_p = _attn_prompt(q_p, kd_p, vd_p, revb, rel_bias, sinks, gq2, seg)
    y_p, ffn_p = _ffn_call(h3, pp, None, o_p, ffn_w[1], layer=1, prompt=True,
                           with_oproj=True, with_kvq=False, prev_state=ffn_p0)

    ps = p_sample.reshape(2, DEC_BATCH * DEC_SEQ, PLE_DIM)
    g3, ffn_s0, k_s, v_s, q_s = _ffn_call(g1, ps, state_ffn, None, ffn_w[0], layer=0,
                                          prompt=False, with_oproj=False, with_kvq=True)
    o_s, k_out, v_out = _attn_sample(
        q_s.reshape(DEC_BATCH, DEC_SEQ, D), k_s.reshape(DEC_BATCH, DEC_SEQ, KD),
        v_s.reshape(DEC_BATCH, DEC_SEQ, KD), cache_k.reshape(DEC_BATCH, W, KD),
        cache_v.reshape(DEC_BATCH, W, KD), revb, rel_bias, sinks, gq2)
    y_s, ffn_s = _ffn_call(g3, ps, state_ffn, o_s.reshape(DEC_BATCH * DEC_SEQ, D),
                           ffn_w[1], layer=1, prompt=False, with_oproj=True, with_kvq=False,
                           prev_state=ffn_s0)

    kv_shape_p = (BATCH, W, N_KV_HEADS, HEAD_DIM)
    kv_shape_s = (DEC_BATCH, W, N_KV_HEADS, HEAD_DIM)
    return (y_p.reshape(BATCH, SEQ, D), y_s.reshape(DEC_BATCH, DEC_SEQ, D),
            conv_p[None], conv_s[None],
            ffn_p, ffn_s,
            k_p.reshape(BATCH, SEQ, KD)[:, SEQ - W:].reshape(kv_shape_p),
            k_out.reshape(kv_shape_s),
            v_p.reshape(BATCH, SEQ, KD)[:, SEQ - W:].reshape(kv_shape_p),
            v_out.reshape(kv_shape_s))
```

```python
import functools
import math

import jax
import jax.numpy as jnp
import numpy as np
from jax import lax
from jax.experimental import pallas as pl
from jax.experimental.pallas import tpu as pltpu

D_MODEL = 1024
BATCH = 8
SEQ = 2048
DEC_BATCH = 128
DEC_SEQ = 8
CONV_K = 31
FFN_CONV_K = 3
D_FF = 2816
N_HEADS = 16
N_KV_HEADS = 4
HEAD_DIM = 64
GROUP = N_HEADS // N_KV_HEADS
WINDOW = 128
N_BUCKETS = 32
MAX_DISTANCE = 128
PLE_DIM = 256
EPS = 1e-6
SCALE = HEAD_DIM ** -0.5
NEG = -1e30
KD = N_KV_HEADS * HEAD_DIM

LANES = 128
SUBLANES = 8
V7X_VMEM_BYTES = 64 * 1024 * 1024
VMEM_LIMIT = V7X_VMEM_BYTES - 4 * 1024 * 1024

T_TILE = 512
MIX_TILE = 1024
S_TILE = 32
FFN_S_TILE = 32
CONV_HALO = 32
FFN_HALO = 8
F_CHUNK = 2816
D_SLABS = D_MODEL // LANES
F_SLABS = D_FF // LANES
ATT_SEQS = 16
ATT_UNROLL = 8
ATT_BLOCKS = 2

_BF = jnp.bfloat16
_F32 = jnp.float32


def _bdot(a, w):
    return jnp.dot(a.astype(_BF), w, preferred_element_type=_F32)


def _rms(x, g):
    return x * lax.rsqrt(jnp.mean(x * x, axis=-1, keepdims=True) + EPS) * g


def _sigmoid(x):
    return 1.0 / (1.0 + jnp.exp(-x))


def _const_spec(shape):
    nd = len(shape)
    return pl.BlockSpec(shape, lambda *_: (0,) * nd, pipeline_mode=pl.Buffered(1))


def _layer_spec(shape, layer):
    nd = len(shape)
    return pl.BlockSpec((None,) + tuple(shape), lambda *_: (layer,) + (0,) * nd,
                        pipeline_mode=pl.Buffered(1))


def _params(n_grid):
    return pltpu.CompilerParams(dimension_semantics=("arbitrary",) * n_grid,
                                vmem_limit_bytes=VMEM_LIMIT)


def _mixer_tail(x, c, bdw, lng, lnb, w2, b2):
    c = c + bdw
    mu = jnp.mean(c, axis=-1, keepdims=True)
    cz = c - mu
    var = jnp.mean(cz * cz, axis=-1, keepdims=True)
    y = cz * lax.rsqrt(var + EPS) * lng + lnb
    s = y * _sigmoid(y)
    return x + _bdot(s, w2) + b2


def _mixer_prompt_kernel(x_ref, g_ref, w1_ref, b1_ref, wdw_ref, bdw_ref, lng_ref, lnb_ref,
                         w2_ref, b2_ref, *rest):
    n_cast = len(_CAST_ROWS)
    cast_in, rest = rest[:n_cast], rest[n_cast:]
    h_ref, cs_ref = rest[0], rest[1]
    cast_out, (slab_ref, oslab_ref) = rest[2:2 + n_cast], rest[2 + n_cast:]
    T = MIX_TILE
    H = T // 2
    j = pl.program_id(1)

    for src, dst in zip(cast_in, cast_out):
        dst[...] = src[...].astype(_BF)

    @pl.when(j == 0)
    def _():
        slab_ref[:, 0:CONV_HALO, :] = jnp.zeros((D_SLABS, CONV_HALO, LANES), _F32)

    x = x_ref[...]
    u = _bdot(_rms(x, g_ref[...]), w1_ref[...]) + b1_ref[...]
    glu = u[:, :D_MODEL] * _sigmoid(u[:, D_MODEL:])
    for c in range(D_SLABS):
        slab_ref[c, CONV_HALO:CONV_HALO + T, :] = glu[:, c * LANES:(c + 1) * LANES]
    cs_ref[...] = glu[T - (CONV_K - 1):, :]

    base = CONV_HALO - (CONV_K - 1)
    for c in range(D_SLABS):
        wc = wdw_ref[:, c * LANES:(c + 1) * LANES]
        for p in range(2):
            acc = slab_ref[c, pl.ds(base + p, H, stride=2), :] * wc[0:1, :]
            for k in range(1, CONV_K):
                acc = acc + slab_ref[c, pl.ds(base + k + p, H, stride=2), :] * wc[k:k + 1, :]
            oslab_ref[c, pl.ds(p, H, stride=2), :] = acc
    conv = jnp.concatenate([oslab_ref[c] for c in range(D_SLABS)], axis=1)
    h_ref[...] = _mixer_tail(x, conv, bdw_ref[...], lng_ref[...], lnb_ref[...],
                             w2_ref[...], b2_ref[...])
    for c in range(D_SLABS):
        slab_ref[c, 0:CONV_HALO, :] = slab_ref[c, T:T + CONV_HALO, :]


def _mixer_sample_kernel(x_ref, st_ref, g_ref, w1f_ref, b1_ref, wdw_ref, bdw_ref, lng_ref,
                         lnb_ref, w2f_ref, b2_ref, h_ref, cs_ref, w1_ref, w2_ref):
    S = S_TILE
    KH = CONV_K - 1

    @pl.when(pl.program_id(0) == 0)
    def _():
        w1_ref[...] = w1f_ref[...].astype(_BF)
        w2_ref[...] = w2f_ref[...].astype(_BF)

    x = x_ref[...]
    u = _bdot(_rms(x, g_ref[...]), w1_ref[...]) + b1_ref[...]
    glu = u[:, :D_MODEL] * _sigmoid(u[:, D_MODEL:])
    glu_tm = jnp.swapaxes(glu.reshape(S, DEC_SEQ, D_MODEL), 0, 1)
    cs_ref[0:KH - DEC_SEQ] = st_ref[DEC_SEQ:KH]
    cs_ref[KH - DEC_SEQ:KH] = glu_tm

    def plane(j):
        return st_ref[j] if j < KH else glu_tm[j - KH]

    outs = []
    for t in range(DEC_SEQ):
        acc = plane(t) * wdw_ref[0:1, :]
        for k in range(1, CONV_K):
            acc = acc + plane(t + k) * wdw_ref[k:k + 1, :]
        outs.append(acc)
    conv = jnp.swapaxes(jnp.stack(outs, axis=0), 0, 1).reshape(S * DEC_SEQ, D_MODEL)
    h_ref[...] = _mixer_tail(x, conv, bdw_ref[...], lng_ref[...], lnb_ref[...],
                             w2_ref[...], b2_ref[...])


def _mixer_weight_specs():
    D = D_MODEL
    return [_const_spec((1, D)), _const_spec((D, 2 * D)), _const_spec((1, 2 * D)),
            _const_spec((CONV_K, D)), _const_spec((1, D)), _const_spec((1, D)),
            _const_spec((1, D)), _const_spec((D, D)), _const_spec((1, D))]


_CAST_ROWS = (2 * D_MODEL, 2 * D_FF, 2 * D_MODEL, 2 * PLE_DIM) + (D_MODEL,) * 4


def _mixer_prompt(x2d, wts, cast_srcs):
    T = MIX_TILE
    nt = SEQ // T
    steps = BATCH * nt
    row = lambda b, j: (b * nt + j, 0)
    cast_specs, cast_shapes = [], []
    for src, rows in zip(cast_srcs, _CAST_ROWS):
        assert src.shape[0] == rows and rows % steps == 0
        cast_specs.append(pl.BlockSpec((rows // steps, src.shape[1]), row))
        cast_shapes.append(jax.ShapeDtypeStruct(src.shape, _BF))
    return pl.pallas_call(
        _mixer_prompt_kernel,
        grid=(BATCH, nt),
        in_specs=[pl.BlockSpec((T, D_MODEL), row)] + _mixer_weight_specs() + cast_specs,
        out_specs=[pl.BlockSpec((T, D_MODEL), row),
                   pl.BlockSpec((None, CONV_K - 1, D_MODEL), lambda b, j: (b, 0, 0))] + cast_specs,
        out_shape=[jax.ShapeDtypeStruct((BATCH * SEQ, D_MODEL), _F32),
                   jax.ShapeDtypeStruct((BATCH, CONV_K - 1, D_MODEL), _F32)] + cast_shapes,
        scratch_shapes=[pltpu.VMEM((D_SLABS, CONV_HALO + T, LANES), _F32),
                        pltpu.VMEM((D_SLABS, T, LANES), _F32)],
        compiler_params=_params(2),
        name="mixer_prompt",
    )(x2d, *wts, *cast_srcs)


def _mixer_sample(x2d, state_tm, wts):
    S = S_TILE
    R = S * DEC_SEQ
    rows = pl.BlockSpec((R, D_MODEL), lambda i: (i, 0))
    planes = pl.BlockSpec((CONV_K - 1, S, D_MODEL), lambda i: (0, i, 0))
    w1_shape, w2_shape = (D_MODEL, 2 * D_MODEL), (D_MODEL, D_MODEL)
    whole = lambda shape: pl.BlockSpec(shape, lambda i: (0, 0))
    return pl.pallas_call(
        _mixer_sample_kernel,
        grid=(DEC_BATCH // S,),
        in_specs=[rows, planes] + _mixer_weight_specs(),
        out_specs=[rows, planes, whole(w1_shape), whole(w2_shape)],
        out_shape=[jax.ShapeDtypeStruct((DEC_BATCH * DEC_SEQ, D_MODEL), _F32),
                   jax.ShapeDtypeStruct((CONV_K - 1, DEC_BATCH, D_MODEL), _F32),
                   jax.ShapeDtypeStruct(w1_shape, _BF), jax.ShapeDtypeStruct(w2_shape, _BF)],
        compiler_params=_params(1),
        name="mixer_sample",
    )(x2d, state_tm, *wts)


def _ffn_kernel(*refs, prompt, with_oproj, with_kvq):
    it = iter(refs)
    h_ref = next(it)
    p_ref = next(it)
    st_ref = None if prompt else next(it)
    if with_oproj:
        o_ref = next(it)
        wo_ref = next(it)
        fs0_ref = next(it)
    gf_ref, wup_ref, wdw_ref, bdw_ref, wdn_ref = (next(it) for _ in range(5))
    gp_ref, wg_ref, wp_ref = (next(it) for _ in range(3))
    if with_kvq:
        gkv_ref, wk_ref, wv_ref, gk_ref, gq_ref, wq_ref = (next(it) for _ in range(6))
    hout_ref = next(it)
    fs_ref = next(it)
    dup_kv = with_kvq and prompt
    if with_kvq:
        k_ref, v_ref, q_ref = (next(it) for _ in range(3))
    if dup_kv:
        kd_ref, vd_ref = next(it), next(it)
    if prompt:
        slab_ref, oslab_ref = next(it), next(it)

    F = D_FF
    R = T_TILE if prompt else FFN_S_TILE * DEC_SEQ
    H = R // 2
    KH = FFN_CONV_K - 1
    assert F_CHUNK == F or prompt

    if prompt:
        @pl.when(pl.program_id(1) == 0)
        def _():
            slab_ref[:, 0:FFN_HALO, :] = jnp.zeros((2 * F_SLABS, FFN_HALO, LANES), _F32)

    h = h_ref[...]
    if with_oproj:
        h = h + _bdot(o_ref[...], wo_ref[...])
    xn = _rms(h, gf_ref[...]).astype(_BF)

    n_chunks = F // F_CHUNK
    spc = F_CHUNK // LANES
    acc = None
    tails = []
    for ci in range(n_chunks):
        halves = []
        for half in range(2):
            c0 = half * F + ci * F_CHUNK
            uc = jnp.dot(xn, wup_ref[:, c0:c0 + F_CHUNK], preferred_element_type=_F32)
            w = wdw_ref[:, c0:c0 + F_CHUNK]
            b = bdw_ref[:, c0:c0 + F_CHUNK]
            if prompt:
                cols = []
                for s in range(spc):
                    sl = half * F_SLABS + ci * spc + s
                    slab_ref[sl, FFN_HALO:FFN_HALO + R, :] = uc[:, s * LANES:(s + 1) * LANES]
                    ws = w[:, s * LANES:(s + 1) * LANES]
                    ph = []
                    for p in range(2):
                        cv = b[:, s * LANES:(s + 1) * LANES]
                        for k in range(FFN_CONV_K):
                            cv = cv + (slab_ref[sl, pl.ds(FFN_HALO - KH + k + p, H, stride=2), :]
                                       * ws[k:k + 1, :])
                        ph.append(cv)
                    cols.append(jnp.concatenate(ph, axis=0))
                halves.append(jnp.concatenate(cols, axis=1))
            else:
                u3 = uc.reshape(FFN_S_TILE, DEC_SEQ, F_CHUNK)
                hist = st_ref[:, :, c0:c0 + F_CHUNK]
                h0 = jnp.broadcast_to(hist[:, 0:1, :], u3.shape)
                h1 = jnp.broadcast_to(hist[:, 1:2, :], u3.shape)
                step = lax.broadcasted_iota(jnp.int32, u3.shape, 1)
                um1 = jnp.where(step == 0, h1, pltpu.roll(u3, 1, 1))
                um2 = jnp.where(step == 0, h0, jnp.where(step == 1, h1, pltpu.roll(u3, 2, 1)))
                cv = b + um2 * w[0:1, :] + um1 * w[1:2, :] + u3 * w[2:3, :]
                tails.append(u3[:, DEC_SEQ - KH:, :])
                halves.append(cv.reshape(R, F_CHUNK))
        gate, val = halves
        act = (gate * _sigmoid(gate) * val).astype(_BF)
        part = jnp.dot(act, wdn_ref[ci * F_CHUNK:(ci + 1) * F_CHUNK, :],
                       preferred_element_type=_F32)
        acc = part if acc is None else acc + part

    if prompt:
        for c in range(D_SLABS):
            oslab_ref[c, pl.ds(0, H, stride=2), :] = acc[:H, c * LANES:(c + 1) * LANES]
            oslab_ref[c, pl.ds(1, H, stride=2), :] = acc[H:, c * LANES:(c + 1) * LANES]
        f = jnp.concatenate([oslab_ref[c] for c in range(D_SLABS)], axis=1)
        new_state = jnp.concatenate(
            [slab_ref[s, FFN_HALO + R - SUBLANES:FFN_HALO + R, :] for s in range(2 * F_SLABS)],
            axis=1)[SUBLANES - KH:, :]
        for s in range(2 * F_SLABS):
            slab_ref[s, 0:FFN_HALO, :] = slab_ref[s, R:R + FFN_HALO, :]
    else:
        f = acc
        new_state = jnp.concatenate(tails, axis=2)
    if with_oproj:
        fs_ref[0] = fs0_ref[...]
        fs_ref[1] = new_state
    else:
        fs_ref[...] = new_state

    h2 = h + f
    gate = _sigmoid(_bdot(_rms(h2, gp_ref[...]), wg_ref[...]))
    h3 = h2 + gate * _bdot(p_ref[...], wp_ref[...])
    hout_ref[...] = h3

    if with_kvq:
        hk = _rms(h3, gkv_ref[...]).astype(_BF)
        kk = jnp.dot(hk, wk_ref[...], preferred_element_type=_F32)
        vv = jnp.dot(hk, wv_ref[...], preferred_element_type=_F32)
        v_ref[...] = vv
        low = lax.broadcasted_iota(jnp.int32, (R, LANES), 1) < HEAD_DIM

        def dup_heads(x):
            out = []
            for c in range(KD // LANES):
                xc = x[:, c * LANES:(c + 1) * LANES]
                xr = pltpu.roll(xc, HEAD_DIM, 1)
                out += [jnp.where(low, xc, xr), jnp.where(low, xr, xc)]
            return out

        kn = []
        for kg in dup_heads(kk):
            r = lax.rsqrt(jnp.mean(kg * kg, axis=-1, keepdims=True) + EPS)
            kn.append(kg * r * gk_ref[...])
        k_ref[...] = jnp.concatenate(
            [jnp.where(low, kn[2 * c], kn[2 * c + 1]) for c in range(KD // LANES)], axis=1)
        if dup_kv:
            kd_ref[...] = jnp.concatenate(kn, axis=1).astype(_BF)
            vd_ref[...] = jnp.concatenate(dup_heads(vv), axis=1).astype(_BF)
        q_ref[...] = _bdot(_rms(h3, gq_ref[...]), wq_ref[...])


def _ffn_call(h2d, p3d, state4d, o2d, w, *, layer, prompt, with_oproj, with_kvq,
              prev_state=None):
    D, F = D_MODEL, D_FF
    if prompt:
        R = T_TILE
        nt = SEQ // R
        grid = (BATCH, nt)
        row = lambda b, j: (b * nt + j, 0)
        p_spec = pl.BlockSpec((None, R, PLE_DIM), lambda b, j: (layer, b * nt + j, 0))
        n_rows, n_seq = BATCH * SEQ, BATCH
        fs_spec = pl.BlockSpec((None, FFN_CONV_K - 1, 2 * F), lambda b, j: (b, 0, 0))
        scratch = [pltpu.VMEM((2 * F_SLABS, FFN_HALO + R, LANES), _F32),
                   pltpu.VMEM((D_SLABS, R, LANES), _F32)]
    else:
        R = FFN_S_TILE * DEC_SEQ
        grid = (DEC_BATCH // FFN_S_TILE,)
        row = lambda i: (i, 0)
        p_spec = pl.BlockSpec((None, R, PLE_DIM), lambda i: (layer, i, 0))
        n_rows, n_seq = DEC_BATCH * DEC_SEQ, DEC_BATCH
        fs_spec = pl.BlockSpec((FFN_S_TILE, FFN_CONV_K - 1, 2 * F), lambda i: (i, 0, 0))
        scratch = []

    args = [h2d, p3d]
    in_specs = [pl.BlockSpec((R, D), row), p_spec]
    if not prompt:
        args.append(state4d)
        in_specs.append(pl.BlockSpec((None, FFN_S_TILE, FFN_CONV_K - 1, 2 * F),
                                     lambda i: (layer, i, 0, 0)))
    fs_shape = (n_seq, FFN_CONV_K - 1, 2 * F)
    if with_oproj:
        args += [o2d, w["wo"], prev_state]
        in_specs += [pl.BlockSpec((R, D), row), _const_spec((D, D)), fs_spec]
        blk, imap = fs_spec.block_shape, fs_spec.index_map
        fs_spec = pl.BlockSpec((2,) + tuple(blk), lambda *g: (0,) + tuple(imap(*g)))
        fs_shape = (2,) + fs_shape
    args += [w["g_ffn"], w["w_up"], w["w_dw"], w["b_dw"], w["w_down"],
             w["g_ple"], w["w_gate"], w["w_proj"]]
    in_specs += [_layer_spec((1, D), layer), _layer_spec((D, 2 * F), layer),
                 _layer_spec((FFN_CONV_K, 2 * F), layer), _layer_spec((1, 2 * F), layer),
                 _layer_spec((F, D), layer), _layer_spec((1, D), layer),
                 _layer_spec((D, D), layer), _layer_spec((PLE_DIM, D), layer)]
    out_specs = [pl.BlockSpec((R, D), row), fs_spec]
    out_shape = [jax.ShapeDtypeStruct((n_rows, D), _F32),
                 jax.ShapeDtypeStruct(fs_shape, _F32)]
    if with_kvq:
        args += [w["g_kv"], w["w_k"], w["w_v"], w["g_k2"], w["g_q_in"], w["w_q"]]
        in_specs += [_const_spec((1, D)), _const_spec((D, KD)), _const_spec((D, KD)),
                     _const_spec((1, LANES)), _const_spec((1, D)), _const_spec((D, D))]
        out_specs += [pl.BlockSpec((R, KD), row), pl.BlockSpec((R, KD), row),
                      pl.BlockSpec((R, D), row)]
        out_shape += [jax.ShapeDtypeStruct((n_rows, KD), _F32),
                      jax.ShapeDtypeStruct((n_rows, KD), _F32),
                      jax.ShapeDtypeStruct((n_rows, D), _F32)]
        if prompt:
            out_specs += [pl.BlockSpec((R, 2 * KD), row), pl.BlockSpec((R, 2 * KD), row)]
            out_shape += [jax.ShapeDtypeStruct((n_rows, 2 * KD), _BF),
                          jax.ShapeDtypeStruct((n_rows, 2 * KD), _BF)]
    kern = functools.partial(_ffn_kernel, prompt=prompt, with_oproj=with_oproj,
                             with_kvq=with_kvq)
    name = "ffn_%s_%d" % ("prompt" if prompt else "sample", layer)
    return pl.pallas_call(
        kern, grid=grid, in_specs=in_specs, out_specs=out_specs, out_shape=out_shape,
        scratch_shapes=scratch, compiler_params=_params(len(grid)), name=name,
    )(*args)


def _t5_buckets(dist):
    max_exact = N_BUCKETS // 2
    d = np.maximum(dist, 0)
    df = np.maximum(d, 1).astype(np.float32)
    large = max_exact + (np.log(df / np.float32(max_exact))
                         / np.float32(math.log(MAX_DISTANCE / max_exact))
                         * np.float32(N_BUCKETS - max_exact)).astype(np.int32)
    large = np.minimum(large, N_BUCKETS - 1)
    return np.where(d < max_exact, d, large).astype(np.int32)


LOG2E = math.log2(math.e)
NEG2 = NEG * LOG2E


def _low_lanes(rows):
    return lax.broadcasted_iota(jnp.int32, (rows, LANES), 1) < HEAD_DIM


def _bias_rows(revb_ref, table_ref, hd, nq):
    revb = revb_ref[...]
    row = jnp.zeros(revb.shape, _F32)
    for bk in range(N_BUCKETS):
        row = jnp.where(revb == bk, table_ref[bk, hd] * LOG2E, row)
    base = jnp.concatenate([jnp.broadcast_to(row, (nq, LANES)),
                            jnp.full((nq, LANES), NEG2, _F32)], axis=1)
    return pltpu.roll(base, 1, 1, stride=1, stride_axis=0)


def _norm_heads(q, seg, gqs):
    n = q.shape[0]
    nc = D_MODEL // LANES
    sq = jnp.concatenate([q[:, c * LANES:(c + 1) * LANES] for c in range(nc)], axis=0)
    ss = jnp.dot((sq * sq).astype(_BF), seg, preferred_element_type=_F32)
    r = lax.rsqrt(ss * (1.0 / HEAD_DIM) + EPS)
    return [q[:, c * LANES:(c + 1) * LANES] * r[c * n:(c + 1) * n, :] * gqs for c in range(nc)]


def _sink_softmax2(logits2, sink2):
    m = jnp.maximum(jnp.max(logits2, axis=-1, keepdims=True), sink2)
    e = jnp.exp2(logits2 - m)
    denom = jnp.sum(e, axis=-1, keepdims=True) + jnp.exp2(sink2 - m)
    return e * (1.0 / denom)


def _dot_t(a, b):
    return lax.dot_general(a, b, (((1,), (1,)), ((), ())), preferred_element_type=_F32)


def _attend_blocks(blocks, sinks_ref, seg, gqs):
    units = []
    for (q, kd, vd, bias_of) in blocks:
        nq = q.shape[0]
        low = _low_lanes(nq)
        qn = _norm_heads(q, seg, gqs)
        for g in range(N_KV_HEADS):
            qs = jnp.concatenate(
                [jnp.where(low if hl % 2 == 0 else jnp.logical_not(low), qn[2 * g + hl // 2], 0.0)
                 for hl in range(GROUP)], axis=0).astype(_BF)
            units.append((g, nq, bias_of, _dot_t(qs, kd[:, g * LANES:(g + 1) * LANES]),
                          vd[:, g * LANES:(g + 1) * LANES]))
    probs = []
    for (g, nq, bias_of, s, _) in units:
        probs.append(jnp.concatenate(
            [_sink_softmax2(s[hl * nq:(hl + 1) * nq, :] + bias_of(g * GROUP + hl),
                            sinks_ref[g * GROUP + hl] * LOG2E) for hl in range(GROUP)],
            axis=0).astype(_BF))
    res, outs = [], []
    for (g, nq, _, _, vg), p in zip(units, probs):
        low_k = _low_lanes(vg.shape[0])
        zero = jnp.zeros_like(vg)
        vlo, vhi = jnp.where(low_k, vg, zero), jnp.where(low_k, zero, vg)
        for pair in range(2):
            r0 = 2 * pair * nq
            outs.append(jnp.dot(p[r0:r0 + nq], vlo, preferred_element_type=_F32)
                        + jnp.dot(p[r0 + nq:r0 + 2 * nq], vhi, preferred_element_type=_F32))
        if g == N_KV_HEADS - 1:
            res.append(jnp.concatenate(outs, axis=1))
            outs = []
    return res


def _attn_prompt_kernel(q_ref, kp_ref, kc_ref, vp_ref, vc_ref, revb_ref, table_ref,
                        sinks_ref, gq_ref, seg_ref, o_ref, bias_ref):
    W = WINDOW
    i = pl.program_id(1)

    @pl.when(jnp.logical_and(pl.program_id(0) == 0, i == 0))
    def _():
        first_half = lax.broadcasted_iota(jnp.int32, (W, 2 * W), 1) < W
        for hd in range(N_HEADS):
            rows = _bias_rows(revb_ref, table_ref, hd, W)
            bias_ref[0, hd] = rows
            bias_ref[1, hd] = jnp.where(first_half, NEG2, rows)

    gqs = gq_ref[...] * (SCALE * LOG2E)
    kp, kc = kp_ref[...], kc_ref[...]
    vp, vc = vp_ref[...], vc_ref[...]
    first = jnp.where(i == 0, 1, 0)
    blocks = []
    for blk in range(ATT_BLOCKS):
        if blk == 0:
            kd = jnp.concatenate([kp, kc[0:W]], axis=0)
            vd = jnp.concatenate([vp, vc[0:W]], axis=0)
            bias_of = lambda hd: bias_ref[first, hd]
        else:
            kd, vd = kc[(blk - 1) * W:(blk + 1) * W], vc[(blk - 1) * W:(blk + 1) * W]
            bias_of = lambda hd: bias_ref[0, hd]
        blocks.append((q_ref[blk * W:(blk + 1) * W, :], kd, vd, bias_of))
    for blk, o in enumerate(_attend_blocks(blocks, sinks_ref, seg_ref[...], gqs)):
        o_ref[blk * W:(blk + 1) * W, :] = o.astype(o_ref.dtype)


def _attn_prompt(q2d, kd2d, vd2d, revb, table, sinks, gq2, seg):
    W = WINDOW
    R = ATT_BLOCKS * W
    ns = SEQ // R
    cur = lambda b, i: (b * ns + i, 0)
    prev = lambda b, i: (b * (SEQ // W) + jnp.maximum(ATT_BLOCKS * i - 1, 0), 0)
    smem = pl.BlockSpec(memory_space=pltpu.SMEM)
    return pl.pallas_call(
        _attn_prompt_kernel,
        grid=(BATCH, ns),
        in_specs=[pl.BlockSpec((R, D_MODEL), cur),
                  pl.BlockSpec((W, 2 * KD), prev), pl.BlockSpec((R, 2 * KD), cur),
                  pl.BlockSpec((W, 2 * KD), prev), pl.BlockSpec((R, 2 * KD), cur),
                  _const_spec((1, LANES)), smem, smem, _const_spec((1, LANES)),
                  _const_spec((LANES, LANES))],
        out_specs=pl.BlockSpec((R, D_MODEL), cur),
        out_shape=jax.ShapeDtypeStruct((BATCH * SEQ, D_MODEL), _BF),
        scratch_shapes=[pltpu.VMEM((2, N_HEADS, W, 2 * W), _F32)],
        compiler_params=_params(2),
        name="attn_prompt",
    )(q2d, kd2d, kd2d, vd2d, vd2d, revb, table, sinks, gq2, seg)


def _step_logits(q, kb, bias_all, gqs):
    n = DEC_SEQ
    low = _low_lanes(n)
    zero = jnp.zeros((n, LANES), _F32)
    rows = []
    for hd in range(N_HEADS):
        g = hd // GROUP
        x = q[:, (hd // 2) * LANES:(hd // 2 + 1) * LANES]
        if hd % 2 != g % 2:
            x = pltpu.roll(x, HEAD_DIM, 1)
        x = jnp.where(low if g % 2 == 0 else jnp.logical_not(low), x, 0.0)
        ss = jnp.sum(x * x, axis=-1, keepdims=True) * (1.0 / HEAD_DIM)
        x = x * lax.rsqrt(ss + EPS) * gqs
        rows.append(jnp.concatenate([x, zero] if g // 2 == 0 else [zero, x], axis=1))
    q_all = jnp.concatenate(rows, axis=0).astype(_BF)
    return _dot_t(q_all, kb) + bias_all


def _split_heads(o_all):
    n = DEC_SEQ
    low = _low_lanes(n)
    outs = []
    for c in range(N_HEADS // 2):
        halves = []
        for hf in range(2):
            hd = 2 * c + hf
            g = hd // GROUP
            y = o_all[hd * n:(hd + 1) * n, (g // 2) * LANES:(g // 2 + 1) * LANES]
            halves.append(y if g % 2 == hf else pltpu.roll(y, HEAD_DIM, 1))
        outs.append(jnp.where(low, halves[0], halves[1]))
    return jnp.concatenate(outs, axis=1)


def _attn_sample_kernel(q_ref, kn_ref, vn_ref, ck_ref, cv_ref, revb_ref, table_ref,
                        sinks_ref, gq_ref, o_ref, ko_ref, vo_ref, bias_ref, kbuf_ref, vbuf_ref):
    W = WINDOW
    L = 2 * W
    U = ATT_UNROLL

    @pl.when(pl.program_id(0) == 0)
    def _():
        for hd in range(N_HEADS):
            bias_ref[hd * DEC_SEQ:(hd + 1) * DEC_SEQ, :] = _bias_rows(revb_ref, table_ref, hd,
                                                                      DEC_SEQ)
        kbuf_ref[...] = jnp.zeros((U, L, KD), _F32)
        vbuf_ref[...] = jnp.zeros((U, L, KD), _F32)

    gqs = gq_ref[...] * (SCALE * LOG2E)
    bias_all = bias_ref[...]
    sink_col = jnp.concatenate(
        [jnp.full((DEC_SEQ, 1), sinks_ref[hd] * LOG2E, _F32) for hd in range(N_HEADS)], axis=0)

    def body(it, carry):
        logits = []
        for u in range(U):
            s = it * U + u
            kbuf_ref[u, 0:W, :] = ck_ref[s]
            vbuf_ref[u, 0:W, :] = cv_ref[s]
            kbuf_ref[u, W:W + DEC_SEQ, :] = kn_ref[s]
            vbuf_ref[u, W:W + DEC_SEQ, :] = vn_ref[s]
            ko_ref[s] = kbuf_ref[u, DEC_SEQ:DEC_SEQ + W, :]
            vo_ref[s] = vbuf_ref[u, DEC_SEQ:DEC_SEQ + W, :]
            logits.append(_step_logits(q_ref[s], kbuf_ref[u].astype(_BF), bias_all, gqs))
        probs = [_sink_softmax2(l, sink_col).astype(_BF) for l in logits]
        for u in range(U):
            o_all = jnp.dot(probs[u], vbuf_ref[u].astype(_BF), preferred_element_type=_F32)
            o_ref[it * U + u] = _split_heads(o_all)
        return carry

    lax.fori_loop(0, ATT_SEQS // U, body, 0)


def _attn_sample(q3d, kn3d, vn3d, ck, cv, revb, table, sinks, gq2):
    W = WINDOW
    A = ATT_SEQS
    blk = lambda r, c: pl.BlockSpec((A, r, c), lambda i: (i, 0, 0))
    smem = pl.BlockSpec(memory_space=pltpu.SMEM)
    return pl.pallas_call(
        _attn_sample_kernel,
        grid=(DEC_BATCH // A,),
        in_specs=[blk(DEC_SEQ, D_MODEL), blk(DEC_SEQ, KD), blk(DEC_SEQ, KD),
                  blk(W, KD), blk(W, KD),
                  _const_spec((1, LANES)), smem, smem, _const_spec((1, LANES))],
        out_specs=[blk(DEC_SEQ, D_MODEL), blk(W, KD), blk(W, KD)],
        out_shape=[jax.ShapeDtypeStruct((DEC_BATCH, DEC_SEQ, D_MODEL), _F32),
                   jax.ShapeDtypeStruct((DEC_BATCH, W, KD), _F32),
                   jax.ShapeDtypeStruct((DEC_BATCH, W, KD), _F32)],
        scratch_shapes=[pltpu.VMEM((N_HEADS * DEC_SEQ, 2 * W), _F32),
                        pltpu.VMEM((ATT_UNROLL, 2 * W, KD), _F32),
                        pltpu.VMEM((ATT_UNROLL, 2 * W, KD), _F32)],
        compiler_params=_params(1),
        name="attn_sample",
    )(q3d, kn3d, vn3d, ck, cv, revb, table, sinks, gq2)


def kernel(x_prompt, x_sample, state_conv, state_ffn, cache_k, cache_v, p_prompt, p_sample, g_mix, cm_w_pw1, cm_b_pw1, cm_w_dw, cm_b_dw, cm_ln_g, cm_ln_b, cm_w_pw2, cm_b_pw2, at_w_q, at_g_q, at_sinks, at_w_o, kv_g, kv_w_k, kv_w_v, kv_g_k, rel_bias, g_ffn, ffn_w_up, ffn_w_dw, ffn_b_dw, ffn_w_down, g_ple, ple_w_gate, ple_w_proj):
    D, F = D_MODEL, D_FF
    row = lambda a: a.reshape(1, -1)

    mixer_w = [row(g_mix[0]), cm_w_pw1[0], row(cm_b_pw1[0]), cm_w_dw[0], row(cm_b_dw[0]),
               row(cm_ln_g[0]), row(cm_ln_b[0]), cm_w_pw2[0], row(cm_b_pw2[0])]
    g1, conv_s_tm, w_pw1_bf, w_pw2_bf = _mixer_sample(
        x_sample.reshape(DEC_BATCH * DEC_SEQ, D), jnp.transpose(state_conv[0], (1, 0, 2)), mixer_w)
    conv_s = jnp.transpose(conv_s_tm, (1, 0, 2))
    mixer_w[1], mixer_w[7] = w_pw1_bf, w_pw2_bf
    cast_srcs = [ffn_w_up.reshape(2 * D, 2 * F), ffn_w_down.reshape(2 * F, D),
                 ple_w_gate.reshape(2 * D, D), ple_w_proj.reshape(2 * PLE_DIM, D),
                 at_w_q[0], at_w_o[0], kv_w_k, kv_w_v]

    xp = x_prompt.reshape(BATCH * SEQ, D)
    pp = p_prompt.reshape(2, BATCH * SEQ, PLE_DIM)
    (h1, conv_p, w_up_bf, w_down_bf, w_gate_bf, w_proj_bf, w_q_bf, w_o_bf, w_k_bf,
     w_v_bf) = _mixer_prompt(xp, mixer_w, cast_srcs)

    stacked = dict(
        g_ffn=g_ffn[:, None, :], w_up=w_up_bf.reshape(2, D, 2 * F), w_dw=ffn_w_dw,
        b_dw=ffn_b_dw[:, None, :], w_down=w_down_bf.reshape(2, F, D), g_ple=g_ple[:, None, :],
        w_gate=w_gate_bf.reshape(2, D, D), w_proj=w_proj_bf.reshape(2, PLE_DIM, D))
    ffn_w = [dict(stacked), dict(stacked)]
    ffn_w[0].update(g_kv=row(kv_g), w_k=w_k_bf, w_v=w_v_bf,
                    g_k2=row(jnp.tile(kv_g_k, 2)), g_q_in=row(g_mix[1]), w_q=w_q_bf)
    ffn_w[1].update(wo=w_o_bf)
    gq2 = row(jnp.tile(at_g_q[0], 2))
    sinks = at_sinks[0]

    W = WINDOW
    revb = jnp.asarray(_t5_buckets(W - 1 - np.arange(W))[None, :])
    half = np.arange(LANES) // HEAD_DIM
    seg = jnp.asarray(half[:, None] == half[None, :], dtype=_BF)

    h3, ffn_p0, k_p, v_p, q_p, kd_p, vd_p = _ffn_call(
        h1, pp, None, None, ffn_w[0], layer=0, prompt=True, with_oproj=False, with_kvq=True)
    o_p = _attn_prompt(q_p, kd_p, vd_p, revb, rel_bias, sinks, gq2, seg)
    y_p, ffn_p = _ffn_call(h3, pp, None, o_p, ffn_w[1], layer=1, prompt=True,
                           with_oproj=True, with_kvq=False, prev_state=ffn_p0)

    ps = p_sample.reshape(2, DEC_BATCH * DEC_SEQ, PLE_DIM)
    g3, ffn_s0, k_s, v_s, q_s = _ffn_call(g1, ps, state_ffn, None, ffn_w[0], layer=0,
                                          prompt=False, with_oproj=False, with_kvq=True)
    o_s, k_out, v_out = _attn_sample(
        q_s.reshape(DEC_BATCH, DEC_SEQ, D), k_s.reshape(DEC_BATCH, DEC_SEQ, KD),
        v_s.reshape(DEC_BATCH, DEC_SEQ, KD), cache_k.reshape(DEC_BATCH, W, KD),
        cache_v.reshape(DEC_BATCH, W, KD), revb, rel_bias, sinks, gq2)
    y_s, ffn_s = _ffn_call(g3, ps, state_ffn, o_s.reshape(DEC_BATCH * DEC_SEQ, D),
                           ffn_w[1], layer=1, prompt=False, with_oproj=True, with_kvq=False,
                           prev_state=ffn_s0)

    kv_shape_p = (BATCH, W, N_KV_HEADS, HEAD_DIM)
    kv_shape_s = (DEC_BATCH, W, N_KV_HEADS, HEAD_DIM)
    return (y_p.reshape(BATCH, SEQ, D), y_s.reshape(DEC_BATCH, DEC_SEQ, D),
            conv_p[None], conv_s[None],
            ffn_p, ffn_s,
            k_p.reshape(BATCH, SEQ, KD)[:, SEQ - W:].reshape(kv_shape_p),
            k_out.reshape(kv_shape_s),
            v_p.reshape(BATCH, SEQ, KD)[:, SEQ - W:].reshape(kv_shape_p),
            v_out.reshape(kv_shape_s))
```

```python
import functools
import math

import jax
import jax.numpy as jnp
import numpy as np
from jax import lax
from jax.experimental import pallas as pl
from jax.experimental.pallas import tpu as pltpu

D_MODEL = 1024
BATCH = 8
SEQ = 2048
DEC_BATCH = 128
DEC_SEQ = 8
CONV_K = 31
FFN_CONV_K = 3
D_FF = 2816
N_HEADS = 16
N_KV_HEADS = 4
HEAD_DIM = 64
GROUP = N_HEADS // N_KV_HEADS
WINDOW = 128
N_BUCKETS = 32
MAX_DISTANCE = 128
PLE_DIM = 256
EPS = 1e-6
SCALE = HEAD_DIM ** -0.5
NEG = -1e30
KD = N_KV_HEADS * HEAD_DIM

LANES = 128
SUBLANES = 8
V7X_VMEM_BYTES = 64 * 1024 * 1024
VMEM_LIMIT = V7X_VMEM_BYTES - 4 * 1024 * 1024

T_TILE = 512
MIX_TILE = 1024
S_TILE = 32
FFN_S_TILE = 32
CONV_HALO = 32
FFN_HALO = 8
F_CHUNK = 2816
D_SLABS = D_MODEL // LANES
F_SLABS = D_FF // LANES
ATT_SEQS = 16
ATT_UNROLL = 8
ATT_BLOCKS = 2

_BF = jnp.bfloat16
_F32 = jnp.float32


def _bdot(a, w):
    return jnp.dot(a.astype(_BF), w, preferred_element_type=_F32)


def _rms(x, g):
    return x * lax.rsqrt(jnp.mean(x * x, axis=-1, keepdims=True) + EPS) * g


def _sigmoid(x):
    return 0.5 * jnp.tanh(0.5 * x) + 0.5


def _const_spec(shape):
    nd = len(shape)
    return pl.BlockSpec(shape, lambda *_: (0,) * nd, pipeline_mode=pl.Buffered(1))


def _layer_spec(shape, layer):
    nd = len(shape)
    return pl.BlockSpec((None,) + tuple(shape), lambda *_: (layer,) + (0,) * nd,
                        pipeline_mode=pl.Buffered(1))


def _params(n_grid):
    return pltpu.CompilerParams(dimension_semantics=("arbitrary",) * n_grid,
                                vmem_limit_bytes=VMEM_LIMIT)


def _mixer_tail(x, c, bdw, lng, lnb, w2, b2):
    c = c + bdw
    mu = jnp.mean(c, axis=-1, keepdims=True)
    cz = c - mu
    var = jnp.mean(cz * cz, axis=-1, keepdims=True)
    y = cz * lax.rsqrt(var + EPS) * lng + lnb
    s = y * _sigmoid(y)
    return x + _bdot(s, w2) + b2


def _mixer_prompt_kernel(x_ref, g_ref, w1_ref, b1_ref, wdw_ref, bdw_ref, lng_ref, lnb_ref,
                         w2_ref, b2_ref, *rest):
    n_cast = len(_CAST_ROWS)
    cast_in, rest = rest[:n_cast], rest[n_cast:]
    h_ref, cs_ref = rest[0], rest[1]
    cast_out, (slab_ref, oslab_ref) = rest[2:2 + n_cast], rest[2 + n_cast:]
    T = MIX_TILE
    H = T // 2
    j = pl.program_id(1)

    for src, dst in zip(cast_in, cast_out):
        dst[...] = src[...].astype(_BF)

    @pl.when(j == 0)
    def _():
        slab_ref[:, 0:CONV_HALO, :] = jnp.zeros((D_SLABS, CONV_HALO, LANES), _F32)

    x = x_ref[...]
    u = _bdot(_rms(x, g_ref[...]), w1_ref[...]) + b1_ref[...]
    glu = u[:, :D_MODEL] * _sigmoid(u[:, D_MODEL:])
    for c in range(D_SLABS):
        slab_ref[c, CONV_HALO:CONV_HALO + T, :] = glu[:, c * LANES:(c + 1) * LANES]
    cs_ref[...] = glu[T - (CONV_K - 1):, :]

    base = CONV_HALO - (CONV_K - 1)
    for c in range(D_SLABS):
        wc = wdw_ref[:, c * LANES:(c + 1) * LANES]
        for p in range(2):
            acc = slab_ref[c, pl.ds(base + p, H, stride=2), :] * wc[0:1, :]
            for k in range(1, CONV_K):
                acc = acc + slab_ref[c, pl.ds(base + k + p, H, stride=2), :] * wc[k:k + 1, :]
            oslab_ref[c, pl.ds(p, H, stride=2), :] = acc
    conv = jnp.concatenate([oslab_ref[c] for c in range(D_SLABS)], axis=1)
    h_ref[...] = _mixer_tail(x, conv, bdw_ref[...], lng_ref[...], lnb_ref[...],
                             w2_ref[...], b2_ref[...])
    for c in range(D_SLABS):
        slab_ref[c, 0:CONV_HALO, :] = slab_ref[c, T:T + CONV_HALO, :]


def _mixer_sample_kernel(x_ref, st_ref, g_ref, w1f_ref, b1_ref, wdw_ref, bdw_ref, lng_ref,
                         lnb_ref, w2f_ref, b2_ref, h_ref, cs_ref, w1_ref, w2_ref):
    S = S_TILE
    KH = CONV_K - 1

    @pl.when(pl.program_id(0) == 0)
    def _():
        w1_ref[...] = w1f_ref[...].astype(_BF)
        w2_ref[...] = w2f_ref[...].astype(_BF)

    x = x_ref[...]
    u = _bdot(_rms(x, g_ref[...]), w1_ref[...]) + b1_ref[...]
    glu = u[:, :D_MODEL] * _sigmoid(u[:, D_MODEL:])
    glu_tm = jnp.swapaxes(glu.reshape(S, DEC_SEQ, D_MODEL), 0, 1)
    cs_ref[0:KH - DEC_SEQ] = st_ref[DEC_SEQ:KH]
    cs_ref[KH - DEC_SEQ:KH] = glu_tm

    def plane(j):
        return st_ref[j] if j < KH else glu_tm[j - KH]

    outs = []
    for t in range(DEC_SEQ):
        acc = plane(t) * wdw_ref[0:1, :]
        for k in range(1, CONV_K):
            acc = acc + plane(t + k) * wdw_ref[k:k + 1, :]
        outs.append(acc)
    conv = jnp.swapaxes(jnp.stack(outs, axis=0), 0, 1).reshape(S * DEC_SEQ, D_MODEL)
    h_ref[...] = _mixer_tail(x, conv, bdw_ref[...], lng_ref[...], lnb_ref[...],
                             w2_ref[...], b2_ref[...])


def _mixer_weight_specs():
    D = D_MODEL
    return [_const_spec((1, D)), _const_spec((D, 2 * D)), _const_spec((1, 2 * D)),
            _const_spec((CONV_K, D)), _const_spec((1, D)), _const_spec((1, D)),
            _const_spec((1, D)), _const_spec((D, D)), _const_spec((1, D))]


_CAST_ROWS = (2 * D_MODEL, 2 * D_FF, 2 * D_MODEL, 2 * PLE_DIM) + (D_MODEL,) * 4


def _mixer_prompt(x2d, wts, cast_srcs):
    T = MIX_TILE
    nt = SEQ // T
    steps = BATCH * nt
    row = lambda b, j: (b * nt + j, 0)
    cast_specs, cast_shapes = [], []
    for src, rows in zip(cast_srcs, _CAST_ROWS):
        assert src.shape[0] == rows and rows % steps == 0
        cast_specs.append(pl.BlockSpec((rows // steps, src.shape[1]), row))
        cast_shapes.append(jax.ShapeDtypeStruct(src.shape, _BF))
    return pl.pallas_call(
        _mixer_prompt_kernel,
        grid=(BATCH, nt),
        in_specs=[pl.BlockSpec((T, D_MODEL), row)] + _mixer_weight_specs() + cast_specs,
        out_specs=[pl.BlockSpec((T, D_MODEL), row),
                   pl.BlockSpec((None, CONV_K - 1, D_MODEL), lambda b, j: (b, 0, 0))] + cast_specs,
        out_shape=[jax.ShapeDtypeStruct((BATCH * SEQ, D_MODEL), _F32),
                   jax.ShapeDtypeStruct((BATCH, CONV_K - 1, D_MODEL), _F32)] + cast_shapes,
        scratch_shapes=[pltpu.VMEM((D_SLABS, CONV_HALO + T, LANES), _F32),
                        pltpu.VMEM((D_SLABS, T, LANES), _F32)],
        compiler_params=_params(2),
        name="mixer_prompt",
    )(x2d, *wts, *cast_srcs)


def _mixer_sample(x2d, state_tm, wts):
    S = S_TILE
    R = S * DEC_SEQ
    rows = pl.BlockSpec((R, D_MODEL), lambda i: (i, 0))
    planes = pl.BlockSpec((CONV_K - 1, S, D_MODEL), lambda i: (0, i, 0))
    w1_shape, w2_shape = (D_MODEL, 2 * D_MODEL), (D_MODEL, D_MODEL)
    whole = lambda shape: pl.BlockSpec(shape, lambda i: (0, 0))
    return pl.pallas_call(
        _mixer_sample_kernel,
        grid=(DEC_BATCH // S,),
        in_specs=[rows, planes] + _mixer_weight_specs(),
        out_specs=[rows, planes, whole(w1_shape), whole(w2_shape)],
        out_shape=[jax.ShapeDtypeStruct((DEC_BATCH * DEC_SEQ, D_MODEL), _F32),
                   jax.ShapeDtypeStruct((CONV_K - 1, DEC_BATCH, D_MODEL), _F32),
                   jax.ShapeDtypeStruct(w1_shape, _BF), jax.ShapeDtypeStruct(w2_shape, _BF)],
        compiler_params=_params(1),
        name="mixer_sample",
    )(x2d, state_tm, *wts)


def _ffn_kernel(*refs, prompt, with_oproj, with_kvq):
    it = iter(refs)
    h_ref = next(it)
    p_ref = next(it)
    st_ref = None if prompt else next(it)
    if with_oproj:
        o_ref = next(it)
        wo_ref = next(it)
        fs0_ref = next(it)
    gf_ref, wup_ref, wdw_ref, bdw_ref, wdn_ref = (next(it) for _ in range(5))
    gp_ref, wg_ref, wp_ref = (next(it) for _ in range(3))
    if with_kvq:
        gkv_ref, wk_ref, wv_ref, gk_ref, gq_ref, wq_ref = (next(it) for _ in range(6))
    hout_ref = next(it)
    fs_ref = next(it)
    dup_kv = with_kvq and prompt
    if with_kvq:
        k_ref, v_ref, q_ref = (next(it) for _ in range(3))
    if dup_kv:
        kd_ref, vd_ref = next(it), next(it)
    if prompt:
        slab_ref, oslab_ref = next(it), next(it)

    F = D_FF
    R = T_TILE if prompt else FFN_S_TILE * DEC_SEQ
    H = R // 2
    KH = FFN_CONV_K - 1
    assert F_CHUNK == F or prompt

    if prompt:
        @pl.when(pl.program_id(1) == 0)
        def _():
            slab_ref[:, 0:FFN_HALO, :] = jnp.zeros((2 * F_SLABS, FFN_HALO, LANES), _F32)

    h = h_ref[...]
    if with_oproj:
        h = h + _bdot(o_ref[...], wo_ref[...])
    xn = _rms(h, gf_ref[...]).astype(_BF)

    n_chunks = F // F_CHUNK
    spc = F_CHUNK // LANES
    acc = None
    tails = []
    for ci in range(n_chunks):
        halves = []
        for half in range(2):
            c0 = half * F + ci * F_CHUNK
            uc = jnp.dot(xn, wup_ref[:, c0:c0 + F_CHUNK], preferred_element_type=_F32)
            w = wdw_ref[:, c0:c0 + F_CHUNK]
            b = bdw_ref[:, c0:c0 + F_CHUNK]
            if prompt:
                cols = []
                for s in range(spc):
                    sl = half * F_SLABS + ci * spc + s
                    slab_ref[sl, FFN_HALO:FFN_HALO + R, :] = uc[:, s * LANES:(s + 1) * LANES]
                    ws = w[:, s * LANES:(s + 1) * LANES]
                    ph = []
                    for p in range(2):
                        cv = b[:, s * LANES:(s + 1) * LANES]
                        for k in range(FFN_CONV_K):
                            cv = cv + (slab_ref[sl, pl.ds(FFN_HALO - KH + k + p, H, stride=2), :]
                                       * ws[k:k + 1, :])
                        ph.append(cv)
                    cols.append(jnp.concatenate(ph, axis=0))
                halves.append(jnp.concatenate(cols, axis=1))
            else:
                u3 = uc.reshape(FFN_S_TILE, DEC_SEQ, F_CHUNK)
                hist = st_ref[:, :, c0:c0 + F_CHUNK]
                h0 = jnp.broadcast_to(hist[:, 0:1, :], u3.shape)
                h1 = jnp.broadcast_to(hist[:, 1:2, :], u3.shape)
                step = lax.broadcasted_iota(jnp.int32, u3.shape, 1)
                um1 = jnp.where(step == 0, h1, pltpu.roll(u3, 1, 1))
                um2 = jnp.where(step == 0, h0, jnp.where(step == 1, h1, pltpu.roll(u3, 2, 1)))
                cv = b + um2 * w[0:1, :] + um1 * w[1:2, :] + u3 * w[2:3, :]
                tails.append(u3[:, DEC_SEQ - KH:, :])
                halves.append(cv.reshape(R, F_CHUNK))
        gate, val = halves
        act = (gate * _sigmoid(gate) * val).astype(_BF)
        part = jnp.dot(act, wdn_ref[ci * F_CHUNK:(ci + 1) * F_CHUNK, :],
                       preferred_element_type=_F32)
        acc = part if acc is None else acc + part

    if prompt:
        for c in range(D_SLABS):
            oslab_ref[c, pl.ds(0, H, stride=2), :] = acc[:H, c * LANES:(c + 1) * LANES]
            oslab_ref[c, pl.ds(1, H, stride=2), :] = acc[H:, c * LANES:(c + 1) * LANES]
        f = jnp.concatenate([oslab_ref[c] for c in range(D_SLABS)], axis=1)
        new_state = jnp.concatenate(
            [slab_ref[s, FFN_HALO + R - SUBLANES:FFN_HALO + R, :] for s in range(2 * F_SLABS)],
            axis=1)[SUBLANES - KH:, :]
        for s in range(2 * F_SLABS):
            slab_ref[s, 0:FFN_HALO, :] = slab_ref[s, R:R + FFN_HALO, :]
    else:
        f = acc
        new_state = jnp.concatenate(tails, axis=2)
    if with_oproj:
        fs_ref[0] = fs0_ref[...]
        fs_ref[1] = new_state
    else:
        fs_ref[...] = new_state

    h2 = h + f
    gate = _sigmoid(_bdot(_rms(h2, gp_ref[...]), wg_ref[...]))
    h3 = h2 + gate * _bdot(p_ref[...], wp_ref[...])
    hout_ref[...] = h3

    if with_kvq:
        hk = _rms(h3, gkv_ref[...]).astype(_BF)
        kk = jnp.dot(hk, wk_ref[...], preferred_element_type=_F32)
        vv = jnp.dot(hk, wv_ref[...], preferred_element_type=_F32)
        v_ref[...] = vv
        low = lax.broadcasted_iota(jnp.int32, (R, LANES), 1) < HEAD_DIM

        def dup_heads(x):
            out = []
            for c in range(KD // LANES):
                xc = x[:, c * LANES:(c + 1) * LANES]
                xr = pltpu.roll(xc, HEAD_DIM, 1)
                out += [jnp.where(low, xc, xr), jnp.where(low, xr, xc)]
            return out

        kn = []
        for kg in dup_heads(kk):
            r = lax.rsqrt(jnp.mean(kg * kg, axis=-1, keepdims=True) + EPS)
            kn.append(kg * r * gk_ref[...])
        k_ref[...] = jnp.concatenate(
            [jnp.where(low, kn[2 * c], kn[2 * c + 1]) for c in range(KD // LANES)], axis=1)
        if dup_kv:
            kd_ref[...] = jnp.concatenate(kn, axis=1).astype(_BF)
            vd_ref[...] = jnp.concatenate(dup_heads(vv), axis=1).astype(_BF)
        q_ref[...] = _bdot(_rms(h3, gq_ref[...]), wq_ref[...])


def _ffn_call(h2d, p3d, state4d, o2d, w, *, layer, prompt, with_oproj, with_kvq,
              prev_state=None):
    D, F = D_MODEL, D_FF
    if prompt:
        R = T_TILE
        nt = SEQ // R
        grid = (BATCH, nt)
        row = lambda b, j: (b * nt + j, 0)
        p_spec = pl.BlockSpec((None, R, PLE_DIM), lambda b, j: (layer, b * nt + j, 0))
        n_rows, n_seq = BATCH * SEQ, BATCH
        fs_spec = pl.BlockSpec((None, FFN_CONV_K - 1, 2 * F), lambda b, j: (b, 0, 0))
        scratch = [pltpu.VMEM((2 * F_SLABS, FFN_HALO + R, LANES), _F32),
                   pltpu.VMEM((D_SLABS, R, LANES), _F32)]
    else:
        R = FFN_S_TILE * DEC_SEQ
        grid = (DEC_BATCH // FFN_S_TILE,)
        row = lambda i: (i, 0)
        p_spec = pl.BlockSpec((None, R, PLE_DIM), lambda i: (layer, i, 0))
        n_rows, n_seq = DEC_BATCH * DEC_SEQ, DEC_BATCH
        fs_spec = pl.BlockSpec((FFN_S_TILE, FFN_CONV_K - 1, 2 * F), lambda i: (i, 0, 0))
        scratch = []

    args = [h2d, p3d]
    in_specs = [pl.BlockSpec((R, D), row), p_spec]
    if not prompt:
        args.append(state4d)
        in_specs.append(pl.BlockSpec((None, FFN_S_TILE, FFN_CONV_K - 1, 2 * F),
                                     lambda i: (layer, i, 0, 0)))
    fs_shape = (n_seq, FFN_CONV_K - 1, 2 * F)
    if with_oproj:
        args += [o2d, w["wo"], prev_state]
        in_specs += [pl.BlockSpec((R, D), row), _const_spec((D, D)), fs_spec]
        blk, imap = fs_spec.block_shape, fs_spec.index_map
        fs_spec = pl.BlockSpec((2,) + tuple(blk), lambda *g: (0,) + tuple(imap(*g)))
        fs_shape = (2,) + fs_shape
    args += [w["g_ffn"], w["w_up"], w["w_dw"], w["b_dw"], w["w_down"],
             w["g_ple"], w["w_gate"], w["w_proj"]]
    in_specs += [_layer_spec((1, D), layer), _layer_spec((D, 2 * F), layer),
                 _layer_spec((FFN_CONV_K, 2 * F), layer), _layer_spec((1, 2 * F), layer),
                 _layer_spec((F, D), layer), _layer_spec((1, D), layer),
                 _layer_spec((D, D), layer), _layer_spec((PLE_DIM, D), layer)]
    out_specs = [pl.BlockSpec((R, D), row), fs_spec]
    out_shape = [jax.ShapeDtypeStruct((n_rows, D), _F32),
                 jax.ShapeDtypeStruct(fs_shape, _F32)]
    if with_kvq:
        args += [w["g_kv"], w["w_k"], w["w_v"], w["g_k2"], w["g_q_in"], w["w_q"]]
        in_specs += [_const_spec((1, D)), _const_spec((D, KD)), _const_spec((D, KD)),
                     _const_spec((1, LANES)), _const_spec((1, D)), _const_spec((D, D))]
        out_specs += [pl.BlockSpec((R, KD), row), pl.BlockSpec((R, KD), row),
                      pl.BlockSpec((R, D), row)]
        out_shape += [jax.ShapeDtypeStruct((n_rows, KD), _F32),
                      jax.ShapeDtypeStruct((n_rows, KD), _F32),
                      jax.ShapeDtypeStruct((n_rows, D), _F32)]
        if prompt:
            out_specs += [pl.BlockSpec((R, 2 * KD), row), pl.BlockSpec((R, 2 * KD), row)]
            out_shape += [jax.ShapeDtypeStruct((n_rows, 2 * KD), _BF),
                          jax.ShapeDtypeStruct((n_rows, 2 * KD), _BF)]
    kern = functools.partial(_ffn_kernel, prompt=prompt, with_oproj=with_oproj,
                             with_kvq=with_kvq)
    name = "ffn_%s_%d" % ("prompt" if prompt else "sample", layer)
    return pl.pallas_call(
        kern, grid=grid, in_specs=in_specs, out_specs=out_specs, out_shape=out_shape,
        scratch_shapes=scratch, compiler_params=_params(len(grid)), name=name,
    )(*args)


def _t5_buckets(dist):
    max_exact = N_BUCKETS // 2
    d = np.maximum(dist, 0)
    df = np.maximum(d, 1).astype(np.float32)
    large = max_exact + (np.log(df / np.float32(max_exact))
                         / np.float32(math.log(MAX_DISTANCE / max_exact))
                         * np.float32(N_BUCKETS - max_exact)).astype(np.int32)
    large = np.minimum(large, N_BUCKETS - 1)
    return np.where(d < max_exact, d, large).astype(np.int32)


LOG2E = math.log2(math.e)
NEG2 = NEG * LOG2E


def _low_lanes(rows):
    return lax.broadcasted_iota(jnp.int32, (rows, LANES), 1) < HEAD_DIM


def _bias_rows(revb_ref, table_ref, hd, nq):
    revb = revb_ref[...]
    row = jnp.zeros(revb.shape, _F32)
    for bk in range(N_BUCKETS):
        row = jnp.where(revb == bk, table_ref[bk, hd] * LOG2E, row)
    base = jnp.concatenate([jnp.broadcast_to(row, (nq, LANES)),
                            jnp.full((nq, LANES), NEG2, _F32)], axis=1)
    return pltpu.roll(base, 1, 1, stride=1, stride_axis=0)


def _norm_heads(q, seg, gqs):
    n = q.shape[0]
    nc = D_MODEL // LANES
    sq = jnp.concatenate([q[:, c * LANES:(c + 1) * LANES] for c in range(nc)], axis=0)
    ss = jnp.dot((sq * sq).astype(_BF), seg, preferred_element_type=_F32)
    r = lax.rsqrt(ss * (1.0 / HEAD_DIM) + EPS)
    return [q[:, c * LANES:(c + 1) * LANES] * r[c * n:(c + 1) * n, :] * gqs for c in range(nc)]


def _sink_softmax2(logits2, sink2):
    m = jnp.maximum(jnp.max(logits2, axis=-1, keepdims=True), sink2)
    e = jnp.exp2(logits2 - m)
    denom = jnp.sum(e, axis=-1, keepdims=True) + jnp.exp2(sink2 - m)
    return e * (1.0 / denom)


def _dot_t(a, b):
    return lax.dot_general(a, b, (((1,), (1,)), ((), ())), preferred_element_type=_F32)


def _attend_blocks(blocks, sinks_ref, seg, gqs):
    units = []
    for (q, kd, vd, bias_of) in blocks:
        nq = q.shape[0]
        low = _low_lanes(nq)
        qn = _norm_heads(q, seg, gqs)
        for g in range(N_KV_HEADS):
            qs = jnp.concatenate(
                [jnp.where(low if hl % 2 == 0 else jnp.logical_not(low), qn[2 * g + hl // 2], 0.0)
                 for hl in range(GROUP)], axis=0).astype(_BF)
            units.append((g, nq, bias_of, _dot_t(qs, kd[:, g * LANES:(g + 1) * LANES]),
                          vd[:, g * LANES:(g + 1) * LANES]))
    probs = []
    for (g, nq, bias_of, s, _) in units:
        probs.append(jnp.concatenate(
            [_sink_softmax2(s[hl * nq:(hl + 1) * nq, :] + bias_of(g * GROUP + hl),
                            sinks_ref[g * GROUP + hl] * LOG2E) for hl in range(GROUP)],
            axis=0).astype(_BF))
    res, outs = [], []
    for (g, nq, _, _, vg), p in zip(units, probs):
        low_k = _low_lanes(vg.shape[0])
        zero = jnp.zeros_like(vg)
        vlo, vhi = jnp.where(low_k, vg, zero), jnp.where(low_k, zero, vg)
        for pair in range(2):
            r0 = 2 * pair * nq
            outs.append(jnp.dot(p[r0:r0 + nq], vlo, preferred_element_type=_F32)
                        + jnp.dot(p[r0 + nq:r0 + 2 * nq], vhi, preferred_element_type=_F32))
        if g == N_KV_HEADS - 1:
            res.append(jnp.concatenate(outs, axis=1))
            outs = []
    return res


def _attn_prompt_kernel(q_ref, kp_ref, kc_ref, vp_ref, vc_ref, revb_ref, table_ref,
                        sinks_ref, gq_ref, seg_ref, o_ref, bias_ref):
    W = WINDOW
    i = pl.program_id(1)

    @pl.when(jnp.logical_and(pl.program_id(0) == 0, i == 0))
    def _():
        first_half = lax.broadcasted_iota(jnp.int32, (W, 2 * W), 1) < W
        for hd in range(N_HEADS):
            rows = _bias_rows(revb_ref, table_ref, hd, W)
            bias_ref[0, hd] = rows
            bias_ref[1, hd] = jnp.where(first_half, NEG2, rows)

    gqs = gq_ref[...] * (SCALE * LOG2E)
    kp, kc = kp_ref[...], kc_ref[...]
    vp, vc = vp_ref[...], vc_ref[...]
    first = jnp.where(i == 0, 1, 0)
    blocks = []
    for blk in range(ATT_BLOCKS):
        if blk == 0:
            kd = jnp.concatenate([kp, kc[0:W]], axis=0)
            vd = jnp.concatenate([vp, vc[0:W]], axis=0)
            bias_of = lambda hd: bias_ref[first, hd]
        else:
            kd, vd = kc[(blk - 1) * W:(blk + 1) * W], vc[(blk - 1) * W:(blk + 1) * W]
            bias_of = lambda hd: bias_ref[0, hd]
        blocks.append((q_ref[blk * W:(blk + 1) * W, :], kd, vd, bias_of))
    for blk, o in enumerate(_attend_blocks(blocks, sinks_ref, seg_ref[...], gqs)):
        o_ref[blk * W:(blk + 1) * W, :] = o.astype(o_ref.dtype)


def _attn_prompt(q2d, kd2d, vd2d, revb, table, sinks, gq2, seg):
    W = WINDOW
    R = ATT_BLOCKS * W
    ns = SEQ // R
    cur = lambda b, i: (b * ns + i, 0)
    prev = lambda b, i: (b * (SEQ // W) + jnp.maximum(ATT_BLOCKS * i - 1, 0), 0)
    smem = pl.BlockSpec(memory_space=pltpu.SMEM)
    return pl.pallas_call(
        _attn_prompt_kernel,
        grid=(BATCH, ns),
        in_specs=[pl.BlockSpec((R, D_MODEL), cur),
                  pl.BlockSpec((W, 2 * KD), prev), pl.BlockSpec((R, 2 * KD), cur),
                  pl.BlockSpec((W, 2 * KD), prev), pl.BlockSpec((R, 2 * KD), cur),
                  _const_spec((1, LANES)), smem, smem, _const_spec((1, LANES)),
                  _const_spec((LANES, LANES))],
        out_specs=pl.BlockSpec((R, D_MODEL), cur),
        out_shape=jax.ShapeDtypeStruct((BATCH * SEQ, D_MODEL), _BF),
        scratch_shapes=[pltpu.VMEM((2, N_HEADS, W, 2 * W), _F32)],
        compiler_params=_params(2),
        name="attn_prompt",
    )(q2d, kd2d, kd2d, vd2d, vd2d, revb, table, sinks, gq2, seg)


def _step_logits(q, kb, bias_all, gqs):
    n = DEC_SEQ
    low = _low_lanes(n)
    zero = jnp.zeros((n, LANES), _F32)
    rows = []
    for hd in range(N_HEADS):
        g = hd // GROUP
        x = q[:, (hd // 2) * LANES:(hd // 2 + 1) * LANES]
        if hd % 2 != g % 2:
            x = pltpu.roll(x, HEAD_DIM, 1)
        x = jnp.where(low if g % 2 == 0 else jnp.logical_not(low), x, 0.0)
        ss = jnp.sum(x * x, axis=-1, keepdims=True) * (1.0 / HEAD_DIM)
        x = x * lax.rsqrt(ss + EPS) * gqs
        rows.append(jnp.concatenate([x, zero] if g // 2 == 0 else [zero, x], axis=1))
    q_all = jnp.concatenate(rows, axis=0).astype(_BF)
    return _dot_t(q_all, kb) + bias_all


def _split_heads(o_all):
    n = DEC_SEQ
    low = _low_lanes(n)
    outs = []
    for c in range(N_HEADS // 2):
        halves = []
        for hf in range(2):
            hd = 2 * c + hf
            g = hd // GROUP
            y = o_all[hd * n:(hd + 1) * n, (g // 2) * LANES:(g // 2 + 1) * LANES]
            halves.append(y if g % 2 == hf else pltpu.roll(y, HEAD_DIM, 1))
        outs.append(jnp.where(low, halves[0], halves[1]))
    return jnp.concatenate(outs, axis=1)


def _attn_sample_kernel(q_ref, kn_ref, vn_ref, ck_ref, cv_ref, revb_ref, table_ref,
                        sinks_ref, gq_ref, o_ref, ko_ref, vo_ref, bias_ref, kbuf_ref, vbuf_ref):
    W = WINDOW
    L = 2 * W
    U = ATT_UNROLL

    @pl.when(pl.program_id(0) == 0)
    def _():
        for hd in range(N_HEADS):
            bias_ref[hd * DEC_SEQ:(hd + 1) * DEC_SEQ, :] = _bias_rows(revb_ref, table_ref, hd,
                                                                      DEC_SEQ)
        kbuf_ref[...] = jnp.zeros((U, L, KD), _F32)
        vbuf_ref[...] = jnp.zeros((U, L, KD), _F32)

    gqs = gq_ref[...] * (SCALE * LOG2E)
    bias_all = bias_ref[...]
    sink_col = jnp.concatenate(
        [jnp.full((DEC_SEQ, 1), sinks_ref[hd] * LOG2E, _F32) for hd in range(N_HEADS)], axis=0)

    def body(it, carry):
        logits = []
        for u in range(U):
            s = it * U + u
            kbuf_ref[u, 0:W, :] = ck_ref[s]
            vbuf_ref[u, 0:W, :] = cv_ref[s]
            kbuf_ref[u, W:W + DEC_SEQ, :] = kn_ref[s]
            vbuf_ref[u, W:W + DEC_SEQ, :] = vn_ref[s]
            ko_ref[s] = kbuf_ref[u, DEC_SEQ:DEC_SEQ + W, :]
            vo_ref[s] = vbuf_ref[u, DEC_SEQ:DEC_SEQ + W, :]
            logits.append(_step_logits(q_ref[s], kbuf_ref[u].astype(_BF), bias_all, gqs))
        probs = [_sink_softmax2(l, sink_col).astype(_BF) for l in logits]
        for u in range(U):
            o_all = jnp.dot(probs[u], vbuf_ref[u].astype(_BF), preferred_element_type=_F32)
            o_ref[it * U + u] = _split_heads(o_all)
        return carry

    lax.fori_loop(0, ATT_SEQS // U, body, 0)


def _attn_sample(q3d, kn3d, vn3d, ck, cv, revb, table, sinks, gq2):
    W = WINDOW
    A = ATT_SEQS
    blk = lambda r, c: pl.BlockSpec((A, r, c), lambda i: (i, 0, 0))
    smem = pl.BlockSpec(memory_space=pltpu.SMEM)
    return pl.pallas_call(
        _attn_sample_kernel,
        grid=(DEC_BATCH // A,),
        in_specs=[blk(DEC_SEQ, D_MODEL), blk(DEC_SEQ, KD), blk(DEC_SEQ, KD),
                  blk(W, KD), blk(W, KD),
                  _const_spec((1, LANES)), smem, smem, _const_spec((1, LANES))],
        out_specs=[blk(DEC_SEQ, D_MODEL), blk(W, KD), blk(W, KD)],
        out_shape=[jax.ShapeDtypeStruct((DEC_BATCH, DEC_SEQ, D_MODEL), _F32),
                   jax.ShapeDtypeStruct((DEC_BATCH, W, KD), _F32),
                   jax.ShapeDtypeStruct((DEC_BATCH, W, KD), _F32)],
        scratch_shapes=[pltpu.VMEM((N_HEADS * DEC_SEQ, 2 * W), _F32),
                        pltpu.VMEM((ATT_UNROLL, 2 * W, KD), _F32),
                        pltpu.VMEM((ATT_UNROLL, 2 * W, KD), _F32)],
        compiler_params=_params(1),
        name="attn_sample",
    )(q3d, kn3d, vn3d, ck, cv, revb, table, sinks, gq2)


def kernel(x_prompt, x_sample, state_conv, state_ffn, cache_k, cache_v, p_prompt, p_sample, g_mix, cm_w_pw1, cm_b_pw1, cm_w_dw, cm_b_dw, cm_ln_g, cm_ln_b, cm_w_pw2, cm_b_pw2, at_w_q, at_g_q, at_sinks, at_w_o, kv_g, kv_w_k, kv_w_v, kv_g_k, rel_bias, g_ffn, ffn_w_up, ffn_w_dw, ffn_b_dw, ffn_w_down, g_ple, ple_w_gate, ple_w_proj):
    D, F = D_MODEL, D_FF
    row = lambda a: a.reshape(1, -1)

    mixer_w = [row(g_mix[0]), cm_w_pw1[0], row(cm_b_pw1[0]), cm_w_dw[0], row(cm_b_dw[0]),
               row(cm_ln_g[0]), row(cm_ln_b[0]), cm_w_pw2[0], row(cm_b_pw2[0])]
    g1, conv_s_tm, w_pw1_bf, w_pw2_bf = _mixer_sample(
        x_sample.reshape(DEC_BATCH * DEC_SEQ, D), jnp.transpose(state_conv[0], (1, 0, 2)), mixer_w)
    conv_s = jnp.transpose(conv_s_tm, (1, 0, 2))
    mixer_w[1], mixer_w[7] = w_pw1_bf, w_pw2_bf
    cast_srcs = [ffn_w_up.reshape(2 * D, 2 * F), ffn_w_down.reshape(2 * F, D),
                 ple_w_gate.reshape(2 * D, D), ple_w_proj.reshape(2 * PLE_DIM, D),
                 at_w_q[0], at_w_o[0], kv_w_k, kv_w_v]

    xp = x_prompt.reshape(BATCH * SEQ, D)
    pp = p_prompt.reshape(2, BATCH * SEQ, PLE_DIM)
    (h1, conv_p, w_up_bf, w_down_bf, w_gate_bf, w_proj_bf, w_q_bf, w_o_bf, w_k_bf,
     w_v_bf) = _mixer_prompt(xp, mixer_w, cast_srcs)

    stacked = dict(
        g_ffn=g_ffn[:, None, :], w_up=w_up_bf.reshape(2, D, 2 * F), w_dw=ffn_w_dw,
        b_dw=ffn_b_dw[:, None, :], w_down=w_down_bf.reshape(2, F, D), g_ple=g_ple[:, None, :],
        w_gate=w_gate_bf.reshape(2, D, D), w_proj=w_proj_bf.reshape(2, PLE_DIM, D))
    ffn_w = [dict(stacked), dict(stacked)]
    ffn_w[0].update(g_kv=row(kv_g), w_k=w_k_bf, w_v=w_v_bf,
                    g_k2=row(jnp.tile(kv_g_k, 2)), g_q_in=row(g_mix[1]), w_q=w_q_bf)
    ffn_w[1].update(wo=w_o_bf)
    gq2 = row(jnp.tile(at_g_q[0], 2))
    sinks = at_sinks[0]

    W = WINDOW
    revb = jnp.asarray(_t5_buckets(W - 1 - np.arange(W))[None, :])
    half = np.arange(LANES) // HEAD_DIM
    seg = jnp.asarray(half[:, None] == half[None, :], dtype=_BF)

    h3, ffn_p0, k_p, v_p, q_p, kd_p, vd_p = _ffn_call(
        h1, pp, None, None, ffn_w[0], layer=0, prompt=True, with_oproj=False, with_kvq=True)
    o_p = _attn_prompt(q_p, kd_p, vd_p, revb, rel_bias, sinks, gq2, seg)
    y_p, ffn_p = _ffn_call(h3, pp, None, o_p, ffn_w[1], layer=1, prompt=True,
                           with_oproj=True, with_kvq=False, prev_state=ffn_p0)

    ps = p_sample.reshape(2, DEC_BATCH * DEC_SEQ, PLE_DIM)
    g3, ffn_s0, k_s, v_s, q_s = _ffn_call(g1, ps, state_ffn, None, ffn_w[0], layer=0,
                                          prompt=False, with_oproj=False, with_kvq=True)
    o_s, k_out, v_out = _attn_sample(
        q_s.reshape(DEC_BATCH, DEC_SEQ, D), k_s.reshape(DEC_BATCH, DEC_SEQ, KD),
        v_s.reshape(DEC_BATCH, DEC_SEQ, KD), cache_k.reshape(DEC_BATCH, W, KD),
        cache_v.reshape(DEC_BATCH, W, KD), revb, rel_bias, sinks, gq2)
    y_s, ffn_s = _ffn_call(g3, ps, state_ffn, o_s.reshape(DEC_BATCH * DEC_SEQ, D),
                           ffn_w[1], layer=1, prompt=False, with_oproj=True, with_kvq=False,
                           prev_state=ffn_s0)

    kv_shape_p = (BATCH, W, N_KV_HEADS, HEAD_DIM)
    kv_shape_s = (DEC_BATCH, W, N_KV_HEADS, HEAD_DIM)
    return (y_p.reshape(BATCH, SEQ, D), y_s.reshape(DEC_BATCH, DEC_SEQ, D),
            conv_p[None], conv_s[None],
            ffn_p, ffn_s,
            k_p.reshape(BATCH, SEQ, KD)[:, SEQ - W:].reshape(kv_shape_p),
            k_out.reshape(kv_shape_s),
            v_p.reshape(BATCH, SEQ, KD)[:, SEQ - W:].reshape(kv_shape_p),
            v_out.reshape(kv_shape_s))
```

```python
import functools
import math

import jax
import jax.numpy as jnp
import numpy as np
from jax import lax
from jax.experimental import pallas as pl
from jax.experimental.pallas import tpu as pltpu

D_MODEL = 1024
BATCH = 8
SEQ = 2048
DEC_BATCH = 128
DEC_SEQ = 8
CONV_K = 31
FFN_CONV_K = 3
D_FF = 2816
N_HEADS = 16
N_KV_HEADS = 4
HEAD_DIM = 64
GROUP = N_HEADS // N_KV_HEADS
WINDOW = 128
N_BUCKETS = 32
MAX_DISTANCE = 128
PLE_DIM = 256
EPS = 1e-6
SCALE = HEAD_DIM ** -0.5
NEG = -1e30
KD = N_KV_HEADS * HEAD_DIM

LANES = 128
SUBLANES = 8
V7X_VMEM_BYTES = 64 * 1024 * 1024
VMEM_LIMIT = V7X_VMEM_BYTES - 4 * 1024 * 1024

T_TILE = 512
MIX_TILE = 1024
S_TILE = 32
FFN_S_TILE = 32
CONV_HALO = 32
FFN_HALO = 8
F_CHUNK = 2816
D_SLABS = D_MODEL // LANES
F_SLABS = D_FF // LANES
ATT_SEQS = 16
ATT_UNROLL = 8
ATT_BLOCKS = 2

_BF = jnp.bfloat16
_F32 = jnp.float32


def _bdot(a, w):
    return jnp.dot(a.astype(_BF), w, preferred_element_type=_F32)


def _rms(x, g):
    return x * lax.rsqrt(jnp.mean(x * x, axis=-1, keepdims=True) + EPS) * g


HALF = 0.5


def _one_plus_tanh(half_x):
    return 1.0 + jnp.tanh(half_x)


def _const_spec(shape):
    nd = len(shape)
    return pl.BlockSpec(shape, lambda *_: (0,) * nd, pipeline_mode=pl.Buffered(1))


def _layer_spec(shape, layer):
    nd = len(shape)
    return pl.BlockSpec((None,) + tuple(shape), lambda *_: (layer,) + (0,) * nd,
                        pipeline_mode=pl.Buffered(1))


def _params(n_grid):
    return pltpu.CompilerParams(dimension_semantics=("arbitrary",) * n_grid,
                                vmem_limit_bytes=VMEM_LIMIT)


def _mixer_tail(x, c, bdw, lng, lnb, w2, b2):
    c = c + bdw
    mu = jnp.mean(c, axis=-1, keepdims=True)
    cz = c - mu
    var = jnp.mean(cz * cz, axis=-1, keepdims=True)
    hy = cz * lax.rsqrt(var + EPS) * (HALF * lng) + HALF * lnb
    s = hy * _one_plus_tanh(hy)
    return x + _bdot(s, w2) + b2


def _mixer_prompt_kernel(x_ref, g_ref, w1_ref, b1_ref, wdw_ref, bdw_ref, lng_ref, lnb_ref,
                         w2_ref, b2_ref, *rest):
    n_cast = len(_CAST_ROWS)
    cast_in, rest = rest[:n_cast], rest[n_cast:]
    h_ref, cs_ref = rest[0], rest[1]
    cast_out, (slab_ref, oslab_ref) = rest[2:2 + n_cast], rest[2 + n_cast:]
    T = MIX_TILE
    H = T // 2
    j = pl.program_id(1)

    for src, dst, scale in zip(cast_in, cast_out, _CAST_SCALE):
        dst[...] = (src[...] if scale == 1.0 else scale * src[...]).astype(_BF)

    @pl.when(j == 0)
    def _():
        slab_ref[:, 0:CONV_HALO, :] = jnp.zeros((D_SLABS, CONV_HALO, LANES), _F32)

    x = x_ref[...]
    u = _bdot(_rms(x, g_ref[...]), w1_ref[...]) + HALF * b1_ref[...]
    glu = u[:, :D_MODEL] * _one_plus_tanh(u[:, D_MODEL:])
    for c in range(D_SLABS):
        slab_ref[c, CONV_HALO:CONV_HALO + T, :] = glu[:, c * LANES:(c + 1) * LANES]
    cs_ref[...] = glu[T - (CONV_K - 1):, :]

    base = CONV_HALO - (CONV_K - 1)
    for c in range(D_SLABS):
        wc = wdw_ref[:, c * LANES:(c + 1) * LANES]
        for p in range(2):
            acc = slab_ref[c, pl.ds(base + p, H, stride=2), :] * wc[0:1, :]
            for k in range(1, CONV_K):
                acc = acc + slab_ref[c, pl.ds(base + k + p, H, stride=2), :] * wc[k:k + 1, :]
            oslab_ref[c, pl.ds(p, H, stride=2), :] = acc
    conv = jnp.concatenate([oslab_ref[c] for c in range(D_SLABS)], axis=1)
    h_ref[...] = _mixer_tail(x, conv, bdw_ref[...], lng_ref[...], lnb_ref[...],
                             w2_ref[...], b2_ref[...])
    for c in range(D_SLABS):
        slab_ref[c, 0:CONV_HALO, :] = slab_ref[c, T:T + CONV_HALO, :]


def _mixer_sample_kernel(x_ref, st_ref, g_ref, w1f_ref, b1_ref, wdw_ref, bdw_ref, lng_ref,
                         lnb_ref, w2f_ref, b2_ref, h_ref, cs_ref, w1_ref, w2_ref):
    S = S_TILE
    KH = CONV_K - 1

    @pl.when(pl.program_id(0) == 0)
    def _():
        w1_ref[...] = (HALF * w1f_ref[...]).astype(_BF)
        w2_ref[...] = w2f_ref[...].astype(_BF)

    x = x_ref[...]
    u = _bdot(_rms(x, g_ref[...]), w1_ref[...]) + HALF * b1_ref[...]
    glu = u[:, :D_MODEL] * _one_plus_tanh(u[:, D_MODEL:])
    glu_tm = jnp.swapaxes(glu.reshape(S, DEC_SEQ, D_MODEL), 0, 1)
    cs_ref[0:KH - DEC_SEQ] = st_ref[DEC_SEQ:KH]
    cs_ref[KH - DEC_SEQ:KH] = glu_tm

    def plane(j):
        return st_ref[j] if j < KH else glu_tm[j - KH]

    outs = []
    for t in range(DEC_SEQ):
        acc = plane(t) * wdw_ref[0:1, :]
        for k in range(1, CONV_K):
            acc = acc + plane(t + k) * wdw_ref[k:k + 1, :]
        outs.append(acc)
    conv = jnp.swapaxes(jnp.stack(outs, axis=0), 0, 1).reshape(S * DEC_SEQ, D_MODEL)
    h_ref[...] = _mixer_tail(x, conv, bdw_ref[...], lng_ref[...], lnb_ref[...],
                             w2_ref[...], b2_ref[...])


def _mixer_weight_specs():
    D = D_MODEL
    return [_const_spec((1, D)), _const_spec((D, 2 * D)), _const_spec((1, 2 * D)),
            _const_spec((CONV_K, D)), _const_spec((1, D)), _const_spec((1, D)),
            _const_spec((1, D)), _const_spec((D, D)), _const_spec((1, D))]


_CAST_ROWS = (2 * D_MODEL, 2 * D_FF, 2 * D_MODEL, 2 * PLE_DIM) + (D_MODEL,) * 4
_CAST_SCALE = (1.0, 1.0, HALF, HALF) + (1.0,) * 4


def _mixer_prompt(x2d, wts, cast_srcs):
    T = MIX_TILE
    nt = SEQ // T
    steps = BATCH * nt
    row = lambda b, j: (b * nt + j, 0)
    cast_specs, cast_shapes = [], []
    for src, rows in zip(cast_srcs, _CAST_ROWS):
        assert src.shape[0] == rows and rows % steps == 0
        cast_specs.append(pl.BlockSpec((rows // steps, src.shape[1]), row))
        cast_shapes.append(jax.ShapeDtypeStruct(src.shape, _BF))
    return pl.pallas_call(
        _mixer_prompt_kernel,
        grid=(BATCH, nt),
        in_specs=[pl.BlockSpec((T, D_MODEL), row)] + _mixer_weight_specs() + cast_specs,
        out_specs=[pl.BlockSpec((T, D_MODEL), row),
                   pl.BlockSpec((None, CONV_K - 1, D_MODEL), lambda b, j: (b, 0, 0))] + cast_specs,
        out_shape=[jax.ShapeDtypeStruct((BATCH * SEQ, D_MODEL), _F32),
                   jax.ShapeDtypeStruct((BATCH, CONV_K - 1, D_MODEL), _F32)] + cast_shapes,
        scratch_shapes=[pltpu.VMEM((D_SLABS, CONV_HALO + T, LANES), _F32),
                        pltpu.VMEM((D_SLABS, T, LANES), _F32)],
        compiler_params=_params(2),
        name="mixer_prompt",
    )(x2d, *wts, *cast_srcs)


def _mixer_sample(x2d, state_tm, wts):
    S = S_TILE
    R = S * DEC_SEQ
    rows = pl.BlockSpec((R, D_MODEL), lambda i: (i, 0))
    planes = pl.BlockSpec((CONV_K - 1, S, D_MODEL), lambda i: (0, i, 0))
    w1_shape, w2_shape = (D_MODEL, 2 * D_MODEL), (D_MODEL, D_MODEL)
    whole = lambda shape: pl.BlockSpec(shape, lambda i: (0, 0))
    return pl.pallas_call(
        _mixer_sample_kernel,
        grid=(DEC_BATCH // S,),
        in_specs=[rows, planes] + _mixer_weight_specs(),
        out_specs=[rows, planes, whole(w1_shape), whole(w2_shape)],
        out_shape=[jax.ShapeDtypeStruct((DEC_BATCH * DEC_SEQ, D_MODEL), _F32),
                   jax.ShapeDtypeStruct((CONV_K - 1, DEC_BATCH, D_MODEL), _F32),
                   jax.ShapeDtypeStruct(w1_shape, _BF), jax.ShapeDtypeStruct(w2_shape, _BF)],
        compiler_params=_params(1),
        name="mixer_sample",
    )(x2d, state_tm, *wts)


def _ffn_kernel(*refs, prompt, with_oproj, with_kvq):
    it = iter(refs)
    h_ref = next(it)
    p_ref = next(it)
    st_ref = None if prompt else next(it)
    if with_oproj:
        o_ref = next(it)
        wo_ref = next(it)
        fs0_ref = next(it)
    gf_ref, wup_ref, wdw_ref, bdw_ref, wdn_ref = (next(it) for _ in range(5))
    gp_ref, wg_ref, wp_ref = (next(it) for _ in range(3))
    if with_kvq:
        gkv_ref, wk_ref, wv_ref, gk_ref, gq_ref, wq_ref = (next(it) for _ in range(6))
    hout_ref = next(it)
    fs_ref = next(it)
    dup_kv = with_kvq and prompt
    if with_kvq:
        k_ref, v_ref, q_ref = (next(it) for _ in range(3))
    if dup_kv:
        kd_ref, vd_ref = next(it), next(it)
    if prompt:
        slab_ref, oslab_ref = next(it), next(it)

    F = D_FF
    R = T_TILE if prompt else FFN_S_TILE * DEC_SEQ
    H = R // 2
    KH = FFN_CONV_K - 1
    assert F_CHUNK == F or prompt

    if prompt:
        @pl.when(pl.program_id(1) == 0)
        def _():
            slab_ref[:, 0:FFN_HALO, :] = jnp.zeros((2 * F_SLABS, FFN_HALO, LANES), _F32)

    h = h_ref[...]
    if with_oproj:
        h = h + _bdot(o_ref[...], wo_ref[...])
    xn = _rms(h, gf_ref[...]).astype(_BF)

    n_chunks = F // F_CHUNK
    spc = F_CHUNK // LANES
    acc = None
    tails = []
    for ci in range(n_chunks):
        halves = []
        for half in range(2):
            c0 = half * F + ci * F_CHUNK
            uc = jnp.dot(xn, wup_ref[:, c0:c0 + F_CHUNK], preferred_element_type=_F32)
            w = wdw_ref[:, c0:c0 + F_CHUNK]
            b = bdw_ref[:, c0:c0 + F_CHUNK]
            if half == 0:
                w, b = HALF * w, HALF * b
            if prompt:
                cols = []
                for s in range(spc):
                    sl = half * F_SLABS + ci * spc + s
                    slab_ref[sl, FFN_HALO:FFN_HALO + R, :] = uc[:, s * LANES:(s + 1) * LANES]
                    ws = w[:, s * LANES:(s + 1) * LANES]
                    ph = []
                    for p in range(2):
                        cv = b[:, s * LANES:(s + 1) * LANES]
                        for k in range(FFN_CONV_K):
                            cv = cv + (slab_ref[sl, pl.ds(FFN_HALO - KH + k + p, H, stride=2), :]
                                       * ws[k:k + 1, :])
                        ph.append(cv)
                    cols.append(jnp.concatenate(ph, axis=0))
                halves.append(jnp.concatenate(cols, axis=1))
            else:
                u3 = uc.reshape(FFN_S_TILE, DEC_SEQ, F_CHUNK)
                hist = st_ref[:, :, c0:c0 + F_CHUNK]
                h0 = jnp.broadcast_to(hist[:, 0:1, :], u3.shape)
                h1 = jnp.broadcast_to(hist[:, 1:2, :], u3.shape)
                step = lax.broadcasted_iota(jnp.int32, u3.shape, 1)
                um1 = jnp.where(step == 0, h1, pltpu.roll(u3, 1, 1))
                um2 = jnp.where(step == 0, h0, jnp.where(step == 1, h1, pltpu.roll(u3, 2, 1)))
                cv = b + um2 * w[0:1, :] + um1 * w[1:2, :] + u3 * w[2:3, :]
                tails.append(u3[:, DEC_SEQ - KH:, :])
                halves.append(cv.reshape(R, F_CHUNK))
        half_gate, val = halves
        act = (half_gate * _one_plus_tanh(half_gate) * val).astype(_BF)
        part = jnp.dot(act, wdn_ref[ci * F_CHUNK:(ci + 1) * F_CHUNK, :],
                       preferred_element_type=_F32)
        acc = part if acc is None else acc + part

    if prompt:
        for c in range(D_SLABS):
            oslab_ref[c, pl.ds(0, H, stride=2), :] = acc[:H, c * LANES:(c + 1) * LANES]
            oslab_ref[c, pl.ds(1, H, stride=2), :] = acc[H:, c * LANES:(c + 1) * LANES]
        f = jnp.concatenate([oslab_ref[c] for c in range(D_SLABS)], axis=1)
        new_state = jnp.concatenate(
            [slab_ref[s, FFN_HALO + R - SUBLANES:FFN_HALO + R, :] for s in range(2 * F_SLABS)],
            axis=1)[SUBLANES - KH:, :]
        for s in range(2 * F_SLABS):
            slab_ref[s, 0:FFN_HALO, :] = slab_ref[s, R:R + FFN_HALO, :]
    else:
        f = acc
        new_state = jnp.concatenate(tails, axis=2)
    if with_oproj:
        fs_ref[0] = fs0_ref[...]
        fs_ref[1] = new_state
    else:
        fs_ref[...] = new_state

    h2 = h + f
    h3 = h2 + (_one_plus_tanh(_bdot(_rms(h2, gp_ref[...]), wg_ref[...]))
               * _bdot(p_ref[...], wp_ref[...]))
    hout_ref[...] = h3

    if with_kvq:
        hk = _rms(h3, gkv_ref[...]).astype(_BF)
        kk = jnp.dot(hk, wk_ref[...], preferred_element_type=_F32)
        vv = jnp.dot(hk, wv_ref[...], preferred_element_type=_F32)
        v_ref[...] = vv
        low = lax.broadcasted_iota(jnp.int32, (R, LANES), 1) < HEAD_DIM

        def dup_heads(x):
            out = []
            for c in range(KD // LANES):
                xc = x[:, c * LANES:(c + 1) * LANES]
                xr = pltpu.roll(xc, HEAD_DIM, 1)
                out += [jnp.where(low, xc, xr), jnp.where(low, xr, xc)]
            return out

        kn = []
        for kg in dup_heads(kk):
            r = lax.rsqrt(jnp.mean(kg * kg, axis=-1, keepdims=True) + EPS)
            kn.append(kg * r * gk_ref[...])
        k_ref[...] = jnp.concatenate(
            [jnp.where(low, kn[2 * c], kn[2 * c + 1]) for c in range(KD // LANES)], axis=1)
        if dup_kv:
            kd_ref[...] = jnp.concatenate(kn, axis=1).astype(_BF)
            vd_ref[...] = jnp.concatenate(dup_heads(vv), axis=1).astype(_BF)
        q_ref[...] = _bdot(_rms(h3, gq_ref[...]), wq_ref[...])


def _ffn_call(h2d, p3d, state4d, o2d, w, *, layer, prompt, with_oproj, with_kvq,
              prev_state=None):
    D, F = D_MODEL, D_FF
    if prompt:
        R = T_TILE
        nt = SEQ // R
        grid = (BATCH, nt)
        row = lambda b, j: (b * nt + j, 0)
        p_spec = pl.BlockSpec((None, R, PLE_DIM), lambda b, j: (layer, b * nt + j, 0))
        n_rows, n_seq = BATCH * SEQ, BATCH
        fs_spec = pl.BlockSpec((None, FFN_CONV_K - 1, 2 * F), lambda b, j: (b, 0, 0))
        scratch = [pltpu.VMEM((2 * F_SLABS, FFN_HALO + R, LANES), _F32),
                   pltpu.VMEM((D_SLABS, R, LANES), _F32)]
    else:
        R = FFN_S_TILE * DEC_SEQ
        grid = (DEC_BATCH // FFN_S_TILE,)
        row = lambda i: (i, 0)
        p_spec = pl.BlockSpec((None, R, PLE_DIM), lambda i: (layer, i, 0))
        n_rows, n_seq = DEC_BATCH * DEC_SEQ, DEC_BATCH
        fs_spec = pl.BlockSpec((FFN_S_TILE, FFN_CONV_K - 1, 2 * F), lambda i: (i, 0, 0))
        scratch = []

    args = [h2d, p3d]
    in_specs = [pl.BlockSpec((R, D), row), p_spec]
    if not prompt:
        args.append(state4d)
        in_specs.append(pl.BlockSpec((None, FFN_S_TILE, FFN_CONV_K - 1, 2 * F),
                                     lambda i: (layer, i, 0, 0)))
    fs_shape = (n_seq, FFN_CONV_K - 1, 2 * F)
    if with_oproj:
        args += [o2d, w["wo"], prev_state]
        in_specs += [pl.BlockSpec((R, D), row), _const_spec((D, D)), fs_spec]
        blk, imap = fs_spec.block_shape, fs_spec.index_map
        fs_spec = pl.BlockSpec((2,) + tuple(blk), lambda *g: (0,) + tuple(imap(*g)))
        fs_shape = (2,) + fs_shape
    args += [w["g_ffn"], w["w_up"], w["w_dw"], w["b_dw"], w["w_down"],
             w["g_ple"], w["w_gate"], w["w_proj"]]
    in_specs += [_layer_spec((1, D), layer), _layer_spec((D, 2 * F), layer),
                 _layer_spec((FFN_CONV_K, 2 * F), layer), _layer_spec((1, 2 * F), layer),
                 _layer_spec((F, D), layer), _layer_spec((1, D), layer),
                 _layer_spec((D, D), layer), _layer_spec((PLE_DIM, D), layer)]
    out_specs = [pl.BlockSpec((R, D), row), fs_spec]
    out_shape = [jax.ShapeDtypeStruct((n_rows, D), _F32),
                 jax.ShapeDtypeStruct(fs_shape, _F32)]
    if with_kvq:
        args += [w["g_kv"], w["w_k"], w["w_v"], w["g_k2"], w["g_q_in"], w["w_q"]]
        in_specs += [_const_spec((1, D)), _const_spec((D, KD)), _const_spec((D, KD)),
                     _const_spec((1, LANES)), _const_spec((1, D)), _const_spec((D, D))]
        out_specs += [pl.BlockSpec((R, KD), row), pl.BlockSpec((R, KD), row),
                      pl.BlockSpec((R, D), row)]
        out_shape += [jax.ShapeDtypeStruct((n_rows, KD), _F32),
                      jax.ShapeDtypeStruct((n_rows, KD), _F32),
                      jax.ShapeDtypeStruct((n_rows, D), _F32)]
        if prompt:
            out_specs += [pl.BlockSpec((R, 2 * KD), row), pl.BlockSpec((R, 2 * KD), row)]
            out_shape += [jax.ShapeDtypeStruct((n_rows, 2 * KD), _BF),
                          jax.ShapeDtypeStruct((n_rows, 2 * KD), _BF)]
    kern = functools.partial(_ffn_kernel, prompt=prompt, with_oproj=with_oproj,
                             with_kvq=with_kvq)
    name = "ffn_%s_%d" % ("prompt" if prompt else "sample", layer)
    return pl.pallas_call(
        kern, grid=grid, in_specs=in_specs, out_specs=out_specs, out_shape=out_shape,
        scratch_shapes=scratch, compiler_params=_params(len(grid)), name=name,
    )(*args)


def _t5_buckets(dist):
    max_exact = N_BUCKETS // 2
    d = np.maximum(dist, 0)
    df = np.maximum(d, 1).astype(np.float32)
    large = max_exact + (np.log(df / np.float32(max_exact))
                         / np.float32(math.log(MAX_DISTANCE / max_exact))
                         * np.float32(N_BUCKETS - max_exact)).astype(np.int32)
    large = np.minimum(large, N_BUCKETS - 1)
    return np.where(d < max_exact, d, large).astype(np.int32)


LOG2E = math.log2(math.e)
NEG2 = NEG * LOG2E


def _low_lanes(rows):
    return lax.broadcasted_iota(jnp.int32, (rows, LANES), 1) < HEAD_DIM


def _bias_rows(revb_ref, table_ref, hd, nq):
    revb = revb_ref[...]
    row = jnp.zeros(revb.shape, _F32)
    for bk in range(N_BUCKETS):
        row = jnp.where(revb == bk, table_ref[bk, hd] * LOG2E, row)
    base = jnp.concatenate([jnp.broadcast_to(row, (nq, LANES)),
                            jnp.full((nq, LANES), NEG2, _F32)], axis=1)
    return pltpu.roll(base, 1, 1, stride=1, stride_axis=0)


def _norm_heads(q, seg, gqs):
    n = q.shape[0]
    nc = D_MODEL // LANES
    sq = jnp.concatenate([q[:, c * LANES:(c + 1) * LANES] for c in range(nc)], axis=0)
    ss = jnp.dot((sq * sq).astype(_BF), seg, preferred_element_type=_F32)
    r = lax.rsqrt(ss * (1.0 / HEAD_DIM) + EPS)
    return [q[:, c * LANES:(c + 1) * LANES] * r[c * n:(c + 1) * n, :] * gqs for c in range(nc)]


def _sink_softmax2(logits2, sink2):
    m = jnp.maximum(jnp.max(logits2, axis=-1, keepdims=True), sink2)
    e = jnp.exp2(logits2 - m)
    denom = jnp.sum(e, axis=-1, keepdims=True) + jnp.exp2(sink2 - m)
    return e * (1.0 / denom)


def _dot_t(a, b):
    return lax.dot_general(a, b, (((1,), (1,)), ((), ())), preferred_element_type=_F32)


def _attend_blocks(blocks, sinks_ref, seg, gqs):
    units = []
    for (q, kd, vd, bias_of) in blocks:
        nq = q.shape[0]
        low = _low_lanes(nq)
        qn = _norm_heads(q, seg, gqs)
        for g in range(N_KV_HEADS):
            qs = jnp.concatenate(
                [jnp.where(low if hl % 2 == 0 else jnp.logical_not(low), qn[2 * g + hl // 2], 0.0)
                 for hl in range(GROUP)], axis=0).astype(_BF)
            units.append((g, nq, bias_of, _dot_t(qs, kd[:, g * LANES:(g + 1) * LANES]),
                          vd[:, g * LANES:(g + 1) * LANES]))
    probs = []
    for (g, nq, bias_of, s, _) in units:
        probs.append(jnp.concatenate(
            [_sink_softmax2(s[hl * nq:(hl + 1) * nq, :] + bias_of(g * GROUP + hl),
                            sinks_ref[g * GROUP + hl] * LOG2E) for hl in range(GROUP)],
            axis=0).astype(_BF))
    res, outs = [], []
    for (g, nq, _, _, vg), p in zip(units, probs):
        low_k = _low_lanes(vg.shape[0])
        zero = jnp.zeros_like(vg)
        vlo, vhi = jnp.where(low_k, vg, zero), jnp.where(low_k, zero, vg)
        for pair in range(2):
            r0 = 2 * pair * nq
            outs.append(jnp.dot(p[r0:r0 + nq], vlo, preferred_element_type=_F32)
                        + jnp.dot(p[r0 + nq:r0 + 2 * nq], vhi, preferred_element_type=_F32))
        if g == N_KV_HEADS - 1:
            res.append(jnp.concatenate(outs, axis=1))
            outs = []
    return res


def _attn_prompt_kernel(q_ref, kp_ref, kc_ref, vp_ref, vc_ref, revb_ref, table_ref,
                        sinks_ref, gq_ref, seg_ref, o_ref, bias_ref):
    W = WINDOW
    i = pl.program_id(1)

    @pl.when(jnp.logical_and(pl.program_id(0) == 0, i == 0))
    def _():
        first_half = lax.broadcasted_iota(jnp.int32, (W, 2 * W), 1) < W
        for hd in range(N_HEADS):
            rows = _bias_rows(revb_ref, table_ref, hd, W)
            bias_ref[0, hd] = rows
            bias_ref[1, hd] = jnp.where(first_half, NEG2, rows)

    gqs = gq_ref[...] * (SCALE * LOG2E)
    kp, kc = kp_ref[...], kc_ref[...]
    vp, vc = vp_ref[...], vc_ref[...]
    first = jnp.where(i == 0, 1, 0)
    blocks = []
    for blk in range(ATT_BLOCKS):
        if blk == 0:
            kd = jnp.concatenate([kp, kc[0:W]], axis=0)
            vd = jnp.concatenate([vp, vc[0:W]], axis=0)
            bias_of = lambda hd: bias_ref[first, hd]
        else:
            kd, vd = kc[(blk - 1) * W:(blk + 1) * W], vc[(blk - 1) * W:(blk + 1) * W]
            bias_of = lambda hd: bias_ref[0, hd]
        blocks.append((q_ref[blk * W:(blk + 1) * W, :], kd, vd, bias_of))
    for blk, o in enumerate(_attend_blocks(blocks, sinks_ref, seg_ref[...], gqs)):
        o_ref[blk * W:(blk + 1) * W, :] = o.astype(o_ref.dtype)


def _attn_prompt(q2d, kd2d, vd2d, revb, table, sinks, gq2, seg):
    W = WINDOW
    R = ATT_BLOCKS * W
    ns = SEQ // R
    cur = lambda b, i: (b * ns + i, 0)
    prev = lambda b, i: (b * (SEQ // W) + jnp.maximum(ATT_BLOCKS * i - 1, 0), 0)
    smem = pl.BlockSpec(memory_space=pltpu.SMEM)
    return pl.pallas_call(
        _attn_prompt_kernel,
        grid=(BATCH, ns),
        in_specs=[pl.BlockSpec((R, D_MODEL), cur),
                  pl.BlockSpec((W, 2 * KD), prev), pl.BlockSpec((R, 2 * KD), cur),
                  pl.BlockSpec((W, 2 * KD), prev), pl.BlockSpec((R, 2 * KD), cur),
                  _const_spec((1, LANES)), smem, smem, _const_spec((1, LANES)),
                  _const_spec((LANES, LANES))],
        out_specs=pl.BlockSpec((R, D_MODEL), cur),
        out_shape=jax.ShapeDtypeStruct((BATCH * SEQ, D_MODEL), _BF),
        scratch_shapes=[pltpu.VMEM((2, N_HEADS, W, 2 * W), _F32)],
        compiler_params=_params(2),
        name="attn_prompt",
    )(q2d, kd2d, kd2d, vd2d, vd2d, revb, table, sinks, gq2, seg)


def _step_logits(q, kb, bias_all, gqs):
    n = DEC_SEQ
    low = _low_lanes(n)
    zero = jnp.zeros((n, LANES), _F32)
    rows = []
    for hd in range(N_HEADS):
        g = hd // GROUP
        x = q[:, (hd // 2) * LANES:(hd // 2 + 1) * LANES]
        if hd % 2 != g % 2:
            x = pltpu.roll(x, HEAD_DIM, 1)
        x = jnp.where(low if g % 2 == 0 else jnp.logical_not(low), x, 0.0)
        ss = jnp.sum(x * x, axis=-1, keepdims=True) * (1.0 / HEAD_DIM)
        x = x * lax.rsqrt(ss + EPS) * gqs
        rows.append(jnp.concatenate([x, zero] if g // 2 == 0 else [zero, x], axis=1))
    q_all = jnp.concatenate(rows, axis=0).astype(_BF)
    return _dot_t(q_all, kb) + bias_all


def _split_heads(o_all):
    n = DEC_SEQ
    low = _low_lanes(n)
    outs = []
    for c in range(N_HEADS // 2):
        halves = []
        for hf in range(2):
            hd = 2 * c + hf
            g = hd // GROUP
            y = o_all[hd * n:(hd + 1) * n, (g // 2) * LANES:(g // 2 + 1) * LANES]
            halves.append(y if g % 2 == hf else pltpu.roll(y, HEAD_DIM, 1))
        outs.append(jnp.where(low, halves[0], halves[1]))
    return jnp.concatenate(outs, axis=1)


def _attn_sample_kernel(q_ref, kn_ref, vn_ref, ck_ref, cv_ref, revb_ref, table_ref,
                        sinks_ref, gq_ref, o_ref, ko_ref, vo_ref, bias_ref, kbuf_ref, vbuf_ref):
    W = WINDOW
    L = 2 * W
    U = ATT_UNROLL

    @pl.when(pl.program_id(0) == 0)
    def _():
        for hd in range(N_HEADS):
            bias_ref[hd * DEC_SEQ:(hd + 1) * DEC_SEQ, :] = _bias_rows(revb_ref, table_ref, hd,
                                                                      DEC_SEQ)
        kbuf_ref[...] = jnp.zeros((U, L, KD), _F32)
        vbuf_ref[...] = jnp.zeros((U, L, KD), _F32)

    gqs = gq_ref[...] * (SCALE * LOG2E)
    bias_all = bias_ref[...]
    sink_col = jnp.concatenate(
        [jnp.full((DEC_SEQ, 1), sinks_ref[hd] * LOG2E, _F32) for hd in range(N_HEADS)], axis=0)

    def body(it, carry):
        logits = []
        for u in range(U):
            s = it * U + u
            kbuf_ref[u, 0:W, :] = ck_ref[s]
            vbuf_ref[u, 0:W, :] = cv_ref[s]
            kbuf_ref[u, W:W + DEC_SEQ, :] = kn_ref[s]
            vbuf_ref[u, W:W + DEC_SEQ, :] = vn_ref[s]
            ko_ref[s] = kbuf_ref[u, DEC_SEQ:DEC_SEQ + W, :]
            vo_ref[s] = vbuf_ref[u, DEC_SEQ:DEC_SEQ + W, :]
            logits.append(_step_logits(q_ref[s], kbuf_ref[u].astype(_BF), bias_all, gqs))
        probs = [_sink_softmax2(l, sink_col).astype(_BF) for l in logits]
        for u in range(U):
            o_all = jnp.dot(probs[u], vbuf_ref[u].astype(_BF), preferred_element_type=_F32)
            o_ref[it * U + u] = _split_heads(o_all)
        return carry

    lax.fori_loop(0, ATT_SEQS // U, body, 0)


def _attn_sample(q3d, kn3d, vn3d, ck, cv, revb, table, sinks, gq2):
    W = WINDOW
    A = ATT_SEQS
    blk = lambda r, c: pl.BlockSpec((A, r, c), lambda i: (i, 0, 0))
    smem = pl.BlockSpec(memory_space=pltpu.SMEM)
    return pl.pallas_call(
        _attn_sample_kernel,
        grid=(DEC_BATCH // A,),
        in_specs=[blk(DEC_SEQ, D_MODEL), blk(DEC_SEQ, KD), blk(DEC_SEQ, KD),
                  blk(W, KD), blk(W, KD),
                  _const_spec((1, LANES)), smem, smem, _const_spec((1, LANES))],
        out_specs=[blk(DEC_SEQ, D_MODEL), blk(W, KD), blk(W, KD)],
        out_shape=[jax.ShapeDtypeStruct((DEC_BATCH, DEC_SEQ, D_MODEL), _F32),
                   jax.ShapeDtypeStruct((DEC_BATCH, W, KD), _F32),
                   jax.ShapeDtypeStruct((DEC_BATCH, W, KD), _F32)],
        scratch_shapes=[pltpu.VMEM((N_HEADS * DEC_SEQ, 2 * W), _F32),
                        pltpu.VMEM((ATT_UNROLL, 2 * W, KD), _F32),
                        pltpu.VMEM((ATT_UNROLL, 2 * W, KD), _F32)],
        compiler_params=_params(1),
        name="attn_sample",
    )(q3d, kn3d, vn3d, ck, cv, revb, table, sinks, gq2)


def kernel(x_prompt, x_sample, state_conv, state_ffn, cache_k, cache_v, p_prompt, p_sample, g_mix, cm_w_pw1, cm_b_pw1, cm_w_dw, cm_b_dw, cm_ln_g, cm_ln_b, cm_w_pw2, cm_b_pw2, at_w_q, at_g_q, at_sinks, at_w_o, kv_g, kv_w_k, kv_w_v, kv_g_k, rel_bias, g_ffn, ffn_w_up, ffn_w_dw, ffn_b_dw, ffn_w_down, g_ple, ple_w_gate, ple_w_proj):
    D, F = D_MODEL, D_FF
    row = lambda a: a.reshape(1, -1)

    mixer_w = [row(g_mix[0]), cm_w_pw1[0], row(cm_b_pw1[0]), cm_w_dw[0], row(cm_b_dw[0]),
               row(cm_ln_g[0]), row(cm_ln_b[0]), cm_w_pw2[0], row(cm_b_pw2[0])]
    g1, conv_s_tm, w_pw1_bf, w_pw2_bf = _mixer_sample(
        x_sample.reshape(DEC_BATCH * DEC_SEQ, D), jnp.transpose(state_conv[0], (1, 0, 2)), mixer_w)
    conv_s = jnp.transpose(conv_s_tm, (1, 0, 2))
    mixer_w[1], mixer_w[7] = w_pw1_bf, w_pw2_bf
    cast_srcs = [ffn_w_up.reshape(2 * D, 2 * F), ffn_w_down.reshape(2 * F, D),
                 ple_w_gate.reshape(2 * D, D), ple_w_proj.reshape(2 * PLE_DIM, D),
                 at_w_q[0], at_w_o[0], kv_w_k, kv_w_v]

    xp = x_prompt.reshape(BATCH * SEQ, D)
    pp = p_prompt.reshape(2, BATCH * SEQ, PLE_DIM)
    (h1, conv_p, w_up_bf, w_down_bf, w_gate_bf, w_proj_bf, w_q_bf, w_o_bf, w_k_bf,
     w_v_bf) = _mixer_prompt(xp, mixer_w, cast_srcs)

    stacked = dict(
        g_ffn=g_ffn[:, None, :], w_up=w_up_bf.reshape(2, D, 2 * F), w_dw=ffn_w_dw,
        b_dw=ffn_b_dw[:, None, :], w_down=w_down_bf.reshape(2, F, D), g_ple=g_ple[:, None, :],
        w_gate=w_gate_bf.reshape(2, D, D), w_proj=w_proj_bf.reshape(2, PLE_DIM, D))
    ffn_w = [dict(stacked), dict(stacked)]
    ffn_w[0].update(g_kv=row(kv_g), w_k=w_k_bf, w_v=w_v_bf,
                    g_k2=row(jnp.tile(kv_g_k, 2)), g_q_in=row(g_mix[1]), w_q=w_q_bf)
    ffn_w[1].update(wo=w_o_bf)
    gq2 = row(jnp.tile(at_g_q[0], 2))
    sinks = at_sinks[0]

    W = WINDOW
    revb = jnp.asarray(_t5_buckets(W - 1 - np.arange(W))[None, :])
    half = np.arange(LANES) // HEAD_DIM
    seg = jnp.asarray(half[:, None] == half[None, :], dtype=_BF)

    h3, ffn_p0, k_p, v_p, q_p, kd_p, vd_p = _ffn_call(
        h1, pp, None, None, ffn_w[0], layer=0, prompt=True, with_oproj=False, with_kvq=True)
    o_p = _attn_prompt(q_p, kd_p, vd_p, revb, rel_bias, sinks, gq2, seg)
    y_p, ffn_p = _ffn_call(h3, pp, None, o_p, ffn_w[1], layer=1, prompt=True,
                           with_oproj=True, with_kvq=False, prev_state=ffn_p0)

    ps = p_sample.reshape(2, DEC_BATCH * DEC_SEQ, PLE_DIM)
    g3, ffn_s0, k_s, v_s, q_s = _ffn_call(g1, ps, state_ffn, None, ffn_w[0], layer=0,
                                          prompt=False, with_oproj=False, with_kvq=True)
    o_s, k_out, v_out = _attn_sample(
        q_s.reshape(DEC_BATCH, DEC_SEQ, D), k_s.reshape(DEC_BATCH, DEC_SEQ, KD),
        v_s.reshape(DEC_BATCH, DEC_SEQ, KD), cache_k.reshape(DEC_BATCH, W, KD),
        cache_v.reshape(DEC_BATCH, W, KD), revb, rel_bias, sinks, gq2)
    y_s, ffn_s = _ffn_call(g3, ps, state_ffn, o_s.reshape(DEC_BATCH * DEC_SEQ, D),
                           ffn_w[1], layer=1, prompt=False, with_oproj=True, with_kvq=False,
                           prev_state=ffn_s0)

    kv_shape_p = (BATCH, W, N_KV_HEADS, HEAD_DIM)
    kv_shape_s = (DEC_BATCH, W, N_KV_HEADS, HEAD_DIM)
    return (y_p.reshape(BATCH, SEQ, D), y_s.reshape(DEC_BATCH, DEC_SEQ, D),
            conv_p[None], conv_s[None],
            ffn_p, ffn_s,
            k_p.reshape(BATCH, SEQ, KD)[:, SEQ - W:].reshape(kv_shape_p),
            k_out.reshape(kv_shape_s),
            v_p.reshape(BATCH, SEQ, KD)[:, SEQ - W:].reshape(kv_shape_p),
            v_out.reshape(kv_shape_s))
```

```python
import functools
import math

import jax
import jax.numpy as jnp
import numpy as np
from jax import lax
from jax.experimental import pallas as pl
from jax.experimental.pallas import tpu as pltpu

D_MODEL = 1024
BATCH = 8
SEQ = 2048
DEC_BATCH = 128
DEC_SEQ = 8
CONV_K = 31
FFN_CONV_K = 3
D_FF = 2816
N_HEADS = 16
N_KV_HEADS = 4
HEAD_DIM = 64
GROUP = N_HEADS // N_KV_HEADS
WINDOW = 128
N_BUCKETS = 32
MAX_DISTANCE = 128
PLE_DIM = 256
EPS = 1e-6
SCALE = HEAD_DIM ** -0.5
NEG = -1e30
KD = N_KV_HEADS * HEAD_DIM

LANES = 128
SUBLANES = 8
V7X_VMEM_BYTES = 64 * 1024 * 1024
VMEM_LIMIT = V7X_VMEM_BYTES - 4 * 1024 * 1024

T_TILE = 512
MIX_TILE = 1024
S_TILE = 32
FFN_S_TILE = 32
CONV_HALO = 32
FFN_HALO = 8
F_CHUNK = 2816
D_SLABS = D_MODEL // LANES
F_SLABS = D_FF // LANES
ATT_SEQS = 16
ATT_UNROLL = 8
ATT_BLOCKS = 2

_BF = jnp.bfloat16
_F32 = jnp.float32


def _bdot(a, w):
    return jnp.dot(a.astype(_BF), w, preferred_element_type=_F32)


def _rms(x, g):
    return x * lax.rsqrt(jnp.mean(x * x, axis=-1, keepdims=True) + EPS) * g


HALF = 0.5


def _one_plus_tanh(half_x):
    return 1.0 + jnp.tanh(half_x)


def _const_spec(shape):
    nd = len(shape)
    return pl.BlockSpec(shape, lambda *_: (0,) * nd, pipeline_mode=pl.Buffered(1))


def _layer_spec(shape, layer):
    nd = len(shape)
    return pl.BlockSpec((None,) + tuple(shape), lambda *_: (layer,) + (0,) * nd,
                        pipeline_mode=pl.Buffered(1))


def _params(n_grid):
    return pltpu.CompilerParams(dimension_semantics=("arbitrary",) * n_grid,
                                vmem_limit_bytes=VMEM_LIMIT)


def _glu_slab(u, b1_ref, c):
    ba = b1_ref[:, c * LANES:(c + 1) * LANES]
    bg = b1_ref[:, D_MODEL + c * LANES:D_MODEL + (c + 1) * LANES]
    return (u[:, :LANES] + HALF * ba) * _one_plus_tanh(u[:, LANES:] + HALF * bg)


def _mixer_tail(x, c, bdw, lng, lnb, w2, b2):
    c = c + bdw
    mu = jnp.mean(c, axis=-1, keepdims=True)
    cz = c - mu
    var = jnp.mean(cz * cz, axis=-1, keepdims=True)
    hy = cz * lax.rsqrt(var + EPS) * (HALF * lng) + HALF * lnb
    s = hy * _one_plus_tanh(hy)
    return x + _bdot(s, w2) + b2


def _mixer_prompt_kernel(x_ref, g_ref, w1_ref, b1_ref, wdw_ref, bdw_ref, lng_ref, lnb_ref,
                         w2_ref, b2_ref, *rest):
    n_cast = len(_CAST_ROWS)
    cast_in, rest = rest[:n_cast], rest[n_cast:]
    h_ref, cs_ref = rest[0], rest[1]
    cast_out, (slab_ref, oslab_ref) = rest[2:2 + n_cast], rest[2 + n_cast:]
    T = MIX_TILE
    H = T // 2
    j = pl.program_id(1)

    for src, dst, scale in zip(cast_in, cast_out, _CAST_SCALE):
        dst[...] = (src[...] if scale == 1.0 else scale * src[...]).astype(_BF)

    @pl.when(j == 0)
    def _():
        slab_ref[:, 0:CONV_HALO, :] = jnp.zeros((D_SLABS, CONV_HALO, LANES), _F32)

    x = x_ref[...]
    xn = _rms(x, g_ref[...]).astype(_BF)
    base = CONV_HALO - (CONV_K - 1)
    tails = []
    for c in range(D_SLABS):
        u = jnp.dot(xn, w1_ref[:, 2 * c * LANES:2 * (c + 1) * LANES],
                    preferred_element_type=_F32)
        glu = _glu_slab(u, b1_ref, c)
        slab_ref[c, CONV_HALO:CONV_HALO + T, :] = glu
        tails.append(glu[T - (CONV_K - 1):, :])
        wc = wdw_ref[:, c * LANES:(c + 1) * LANES]
        for p in range(2):
            acc = slab_ref[c, pl.ds(base + p, H, stride=2), :] * wc[0:1, :]
            for k in range(1, CONV_K):
                acc = acc + slab_ref[c, pl.ds(base + k + p, H, stride=2), :] * wc[k:k + 1, :]
            oslab_ref[c, pl.ds(p, H, stride=2), :] = acc
    cs_ref[...] = jnp.concatenate(tails, axis=1)
    conv = jnp.concatenate([oslab_ref[c] for c in range(D_SLABS)], axis=1)
    h_ref[...] = _mixer_tail(x, conv, bdw_ref[...], lng_ref[...], lnb_ref[...],
                             w2_ref[...], b2_ref[...])
    for c in range(D_SLABS):
        slab_ref[c, 0:CONV_HALO, :] = slab_ref[c, T:T + CONV_HALO, :]


def _mixer_sample_kernel(x_ref, st_ref, g_ref, w1f_ref, b1_ref, wdw_ref, bdw_ref, lng_ref,
                         lnb_ref, w2f_ref, b2_ref, h_ref, cs_ref, w1_ref, w2_ref):
    S = S_TILE
    KH = CONV_K - 1

    @pl.when(pl.program_id(0) == 0)
    def _():
        for c in range(D_SLABS):
            for part in range(2):
                src = w1f_ref[:, part * D_MODEL + c * LANES:part * D_MODEL + (c + 1) * LANES]
                w1_ref[:, (2 * c + part) * LANES:(2 * c + part + 1) * LANES] = (
                    HALF * src).astype(_BF)
        w2_ref[...] = w2f_ref[...].astype(_BF)

    x = x_ref[...]
    u = _bdot(_rms(x, g_ref[...]), w1_ref[...])
    glu = jnp.concatenate(
        [_glu_slab(u[:, 2 * c * LANES:2 * (c + 1) * LANES], b1_ref, c) for c in range(D_SLABS)],
        axis=1)
    glu_tm = jnp.swapaxes(glu.reshape(S, DEC_SEQ, D_MODEL), 0, 1)
    cs_ref[0:KH - DEC_SEQ] = st_ref[DEC_SEQ:KH]
    cs_ref[KH - DEC_SEQ:KH] = glu_tm

    def plane(j):
        return st_ref[j] if j < KH else glu_tm[j - KH]

    outs = []
    for t in range(DEC_SEQ):
        acc = plane(t) * wdw_ref[0:1, :]
        for k in range(1, CONV_K):
            acc = acc + plane(t + k) * wdw_ref[k:k + 1, :]
        outs.append(acc)
    conv = jnp.swapaxes(jnp.stack(outs, axis=0), 0, 1).reshape(S * DEC_SEQ, D_MODEL)
    h_ref[...] = _mixer_tail(x, conv, bdw_ref[...], lng_ref[...], lnb_ref[...],
                             w2_ref[...], b2_ref[...])


def _mixer_weight_specs():
    D = D_MODEL
    return [_const_spec((1, D)), _const_spec((D, 2 * D)), _const_spec((1, 2 * D)),
            _const_spec((CONV_K, D)), _const_spec((1, D)), _const_spec((1, D)),
            _const_spec((1, D)), _const_spec((D, D)), _const_spec((1, D))]


_CAST_ROWS = (2 * D_MODEL, 2 * D_FF, 2 * D_MODEL, 2 * PLE_DIM) + (D_MODEL,) * 4
_CAST_SCALE = (1.0, 1.0, HALF, HALF) + (1.0,) * 4


def _mixer_prompt(x2d, wts, cast_srcs):
    T = MIX_TILE
    nt = SEQ // T
    steps = BATCH * nt
    row = lambda b, j: (b * nt + j, 0)
    cast_specs, cast_shapes = [], []
    for src, rows in zip(cast_srcs, _CAST_ROWS):
        assert src.shape[0] == rows and rows % steps == 0
        cast_specs.append(pl.BlockSpec((rows // steps, src.shape[1]), row))
        cast_shapes.append(jax.ShapeDtypeStruct(src.shape, _BF))
    return pl.pallas_call(
        _mixer_prompt_kernel,
        grid=(BATCH, nt),
        in_specs=[pl.BlockSpec((T, D_MODEL), row)] + _mixer_weight_specs() + cast_specs,
        out_specs=[pl.BlockSpec((T, D_MODEL), row),
                   pl.BlockSpec((None, CONV_K - 1, D_MODEL), lambda b, j: (b, 0, 0))] + cast_specs,
        out_shape=[jax.ShapeDtypeStruct((BATCH * SEQ, D_MODEL), _F32),
                   jax.ShapeDtypeStruct((BATCH, CONV_K - 1, D_MODEL), _F32)] + cast_shapes,
        scratch_shapes=[pltpu.VMEM((D_SLABS, CONV_HALO + T, LANES), _F32),
                        pltpu.VMEM((D_SLABS, T, LANES), _F32)],
        compiler_params=_params(2),
        name="mixer_prompt",
    )(x2d, *wts, *cast_srcs)


def _mixer_sample(x2d, state_tm, wts):
    S = S_TILE
    R = S * DEC_SEQ
    rows = pl.BlockSpec((R, D_MODEL), lambda i: (i, 0))
    planes = pl.BlockSpec((CONV_K - 1, S, D_MODEL), lambda i: (0, i, 0))
    w1_shape, w2_shape = (D_MODEL, 2 * D_MODEL), (D_MODEL, D_MODEL)
    whole = lambda shape: pl.BlockSpec(shape, lambda i: (0, 0))
    return pl.pallas_call(
        _mixer_sample_kernel,
        grid=(DEC_BATCH // S,),
        in_specs=[rows, planes] + _mixer_weight_specs(),
        out_specs=[rows, planes, whole(w1_shape), whole(w2_shape)],
        out_shape=[jax.ShapeDtypeStruct((DEC_BATCH * DEC_SEQ, D_MODEL), _F32),
                   jax.ShapeDtypeStruct((CONV_K - 1, DEC_BATCH, D_MODEL), _F32),
                   jax.ShapeDtypeStruct(w1_shape, _BF), jax.ShapeDtypeStruct(w2_shape, _BF)],
        compiler_params=_params(1),
        name="mixer_sample",
    )(x2d, state_tm, *wts)


def _ffn_kernel(*refs, prompt, with_oproj, with_kvq):
    it = iter(refs)
    h_ref = next(it)
    p_ref = next(it)
    st_ref = None if prompt else next(it)
    if with_oproj:
        o_ref = next(it)
        wo_ref = next(it)
        fs0_ref = next(it)
    gf_ref, wup_ref, wdw_ref, bdw_ref, wdn_ref = (next(it) for _ in range(5))
    gp_ref, wg_ref, wp_ref = (next(it) for _ in range(3))
    if with_kvq:
        gkv_ref, wk_ref, wv_ref, gk_ref, gq_ref, wq_ref = (next(it) for _ in range(6))
    hout_ref = next(it)
    fs_ref = next(it)
    dup_kv = with_kvq and prompt
    if with_kvq:
        k_ref, v_ref, q_ref = (next(it) for _ in range(3))
    if dup_kv:
        kd_ref, vd_ref = next(it), next(it)
    if prompt:
        slab_ref, oslab_ref = next(it), next(it)

    F = D_FF
    R = T_TILE if prompt else FFN_S_TILE * DEC_SEQ
    H = R // 2
    KH = FFN_CONV_K - 1
    assert F_CHUNK == F or prompt

    if prompt:
        @pl.when(pl.program_id(1) == 0)
        def _():
            slab_ref[:, 0:FFN_HALO, :] = jnp.zeros((2 * F_SLABS, FFN_HALO, LANES), _F32)

    h = h_ref[...]
    if with_oproj:
        h = h + _bdot(o_ref[...], wo_ref[...])
    xn = _rms(h, gf_ref[...]).astype(_BF)

    n_chunks = F // F_CHUNK
    spc = F_CHUNK // LANES
    acc = None
    tails = []
    for ci in range(n_chunks):
        halves = []
        for half in range(2):
            c0 = half * F + ci * F_CHUNK
            uc = jnp.dot(xn, wup_ref[:, c0:c0 + F_CHUNK], preferred_element_type=_F32)
            w = wdw_ref[:, c0:c0 + F_CHUNK]
            b = bdw_ref[:, c0:c0 + F_CHUNK]
            if half == 0:
                w, b = HALF * w, HALF * b
            if prompt:
                cols = []
                for s in range(spc):
                    sl = half * F_SLABS + ci * spc + s
                    slab_ref[sl, FFN_HALO:FFN_HALO + R, :] = uc[:, s * LANES:(s + 1) * LANES]
                    ws = w[:, s * LANES:(s + 1) * LANES]
                    ph = []
                    for p in range(2):
                        cv = b[:, s * LANES:(s + 1) * LANES]
                        for k in range(FFN_CONV_K):
                            cv = cv + (slab_ref[sl, pl.ds(FFN_HALO - KH + k + p, H, stride=2), :]
                                       * ws[k:k + 1, :])
                        ph.append(cv)
                    cols.append(jnp.concatenate(ph, axis=0))
                halves.append(jnp.concatenate(cols, axis=1))
            else:
                u3 = uc.reshape(FFN_S_TILE, DEC_SEQ, F_CHUNK)
                hist = st_ref[:, :, c0:c0 + F_CHUNK]
                h0 = jnp.broadcast_to(hist[:, 0:1, :], u3.shape)
                h1 = jnp.broadcast_to(hist[:, 1:2, :], u3.shape)
                step = lax.broadcasted_iota(jnp.int32, u3.shape, 1)
                um1 = jnp.where(step == 0, h1, pltpu.roll(u3, 1, 1))
                um2 = jnp.where(step == 0, h0, jnp.where(step == 1, h1, pltpu.roll(u3, 2, 1)))
                cv = b + um2 * w[0:1, :] + um1 * w[1:2, :] + u3 * w[2:3, :]
                tails.append(u3[:, DEC_SEQ - KH:, :])
                halves.append(cv.reshape(R, F_CHUNK))
        half_gate, val = halves
        act = (half_gate * _one_plus_tanh(half_gate) * val).astype(_BF)
        part = jnp.dot(act, wdn_ref[ci * F_CHUNK:(ci + 1) * F_CHUNK, :],
                       preferred_element_type=_F32)
        acc = part if acc is None else acc + part

    if prompt:
        for c in range(D_SLABS):
            oslab_ref[c, pl.ds(0, H, stride=2), :] = acc[:H, c * LANES:(c + 1) * LANES]
            oslab_ref[c, pl.ds(1, H, stride=2), :] = acc[H:, c * LANES:(c + 1) * LANES]
        f = jnp.concatenate([oslab_ref[c] for c in range(D_SLABS)], axis=1)
        new_state = jnp.concatenate(
            [slab_ref[s, FFN_HALO + R - SUBLANES:FFN_HALO + R, :] for s in range(2 * F_SLABS)],
            axis=1)[SUBLANES - KH:, :]
        for s in range(2 * F_SLABS):
            slab_ref[s, 0:FFN_HALO, :] = slab_ref[s, R:R + FFN_HALO, :]
    else:
        f = acc
        new_state = jnp.concatenate(tails, axis=2)
    if with_oproj:
        fs_ref[0] = fs0_ref[...]
        fs_ref[1] = new_state
    else:
        fs_ref[...] = new_state

    h2 = h + f
    h3 = h2 + (_one_plus_tanh(_bdot(_rms(h2, gp_ref[...]), wg_ref[...]))
               * _bdot(p_ref[...], wp_ref[...]))
    hout_ref[...] = h3

    if with_kvq:
        hk = _rms(h3, gkv_ref[...]).astype(_BF)
        kk = jnp.dot(hk, wk_ref[...], preferred_element_type=_F32)
        vv = jnp.dot(hk, wv_ref[...], preferred_element_type=_F32)
        v_ref[...] = vv
        low = lax.broadcasted_iota(jnp.int32, (R, LANES), 1) < HEAD_DIM

        def dup_heads(x):
            out = []
            for c in range(KD // LANES):
                xc = x[:, c * LANES:(c + 1) * LANES]
                xr = pltpu.roll(xc, HEAD_DIM, 1)
                out += [jnp.where(low, xc, xr), jnp.where(low, xr, xc)]
            return out

        kn = []
        for kg in dup_heads(kk):
            r = lax.rsqrt(jnp.mean(kg * kg, axis=-1, keepdims=True) + EPS)
            kn.append(kg * r * gk_ref[...])
        k_ref[...] = jnp.concatenate(
            [jnp.where(low, kn[2 * c], kn[2 * c + 1]) for c in range(KD // LANES)], axis=1)
        if dup_kv:
            kd_ref[...] = jnp.concatenate(kn, axis=1).astype(_BF)
            vd_ref[...] = jnp.concatenate(dup_heads(vv), axis=1).astype(_BF)
        q_ref[...] = _bdot(_rms(h3, gq_ref[...]), wq_ref[...])


def _ffn_call(h2d, p3d, state4d, o2d, w, *, layer, prompt, with_oproj, with_kvq,
              prev_state=None):
    D, F = D_MODEL, D_FF
    if prompt:
        R = T_TILE
        nt = SEQ // R
        grid = (BATCH, nt)
        row = lambda b, j: (b * nt + j, 0)
        p_spec = pl.BlockSpec((None, R, PLE_DIM), lambda b, j: (layer, b * nt + j, 0))
        n_rows, n_seq = BATCH * SEQ, BATCH
        fs_spec = pl.BlockSpec((None, FFN_CONV_K - 1, 2 * F), lambda b, j: (b, 0, 0))
        scratch = [pltpu.VMEM((2 * F_SLABS, FFN_HALO + R, LANES), _F32),
                   pltpu.VMEM((D_SLABS, R, LANES), _F32)]
    else:
        R = FFN_S_TILE * DEC_SEQ
        grid = (DEC_BATCH // FFN_S_TILE,)
        row = lambda i: (i, 0)
        p_spec = pl.BlockSpec((None, R, PLE_DIM), lambda i: (layer, i, 0))
        n_rows, n_seq = DEC_BATCH * DEC_SEQ, DEC_BATCH
        fs_spec = pl.BlockSpec((FFN_S_TILE, FFN_CONV_K - 1, 2 * F), lambda i: (i, 0, 0))
        scratch = []

    args = [h2d, p3d]
    in_specs = [pl.BlockSpec((R, D), row), p_spec]
    if not prompt:
        args.append(state4d)
        in_specs.append(pl.BlockSpec((None, FFN_S_TILE, FFN_CONV_K - 1, 2 * F),
                                     lambda i: (layer, i, 0, 0)))
    fs_shape = (n_seq, FFN_CONV_K - 1, 2 * F)
    if with_oproj:
        args += [o2d, w["wo"], prev_state]
        in_specs += [pl.BlockSpec((R, D), row), _const_spec((D, D)), fs_spec]
        blk, imap = fs_spec.block_shape, fs_spec.index_map
        fs_spec = pl.BlockSpec((2,) + tuple(blk), lambda *g: (0,) + tuple(imap(*g)))
        fs_shape = (2,) + fs_shape
    args += [w["g_ffn"], w["w_up"], w["w_dw"], w["b_dw"], w["w_down"],
             w["g_ple"], w["w_gate"], w["w_proj"]]
    in_specs += [_layer_spec((1, D), layer), _layer_spec((D, 2 * F), layer),
                 _layer_spec((FFN_CONV_K, 2 * F), layer), _layer_spec((1, 2 * F), layer),
                 _layer_spec((F, D), layer), _layer_spec((1, D), layer),
                 _layer_spec((D, D), layer), _layer_spec((PLE_DIM, D), layer)]
    out_specs = [pl.BlockSpec((R, D), row), fs_spec]
    out_shape = [jax.ShapeDtypeStruct((n_rows, D), _F32),
                 jax.ShapeDtypeStruct(fs_shape, _F32)]
    if with_kvq:
        args += [w["g_kv"], w["w_k"], w["w_v"], w["g_k2"], w["g_q_in"], w["w_q"]]
        in_specs += [_const_spec((1, D)), _const_spec((D, KD)), _const_spec((D, KD)),
                     _const_spec((1, LANES)), _const_spec((1, D)), _const_spec((D, D))]
        out_specs += [pl.BlockSpec((R, KD), row), pl.BlockSpec((R, KD), row),
                      pl.BlockSpec((R, D), row)]
        out_shape += [jax.ShapeDtypeStruct((n_rows, KD), _F32),
                      jax.ShapeDtypeStruct((n_rows, KD), _F32),
                      jax.ShapeDtypeStruct((n_rows, D), _F32)]
        if prompt:
            out_specs += [pl.BlockSpec((R, 2 * KD), row), pl.BlockSpec((R, 2 * KD), row)]
            out_shape += [jax.ShapeDtypeStruct((n_rows, 2 * KD), _BF),
                          jax.ShapeDtypeStruct((n_rows, 2 * KD), _BF)]
    kern = functools.partial(_ffn_kernel, prompt=prompt, with_oproj=with_oproj,
                             with_kvq=with_kvq)
    name = "ffn_%s_%d" % ("prompt" if prompt else "sample", layer)
    return pl.pallas_call(
        kern, grid=grid, in_specs=in_specs, out_specs=out_specs, out_shape=out_shape,
        scratch_shapes=scratch, compiler_params=_params(len(grid)), name=name,
    )(*args)


def _t5_buckets(dist):
    max_exact = N_BUCKETS // 2
    d = np.maximum(dist, 0)
    df = np.maximum(d, 1).astype(np.float32)
    large = max_exact + (np.log(df / np.float32(max_exact))
                         / np.float32(math.log(MAX_DISTANCE / max_exact))
                         * np.float32(N_BUCKETS - max_exact)).astype(np.int32)
    large = np.minimum(large, N_BUCKETS - 1)
    return np.where(d < max_exact, d, large).astype(np.int32)


LOG2E = math.log2(math.e)
NEG2 = NEG * LOG2E


def _low_lanes(rows):
    return lax.broadcasted_iota(jnp.int32, (rows, LANES), 1) < HEAD_DIM


def _bias_rows(revb_ref, table_ref, hd, nq):
    revb = revb_ref[...]
    row = jnp.zeros(revb.shape, _F32)
    for bk in range(N_BUCKETS):
        row = jnp.where(revb == bk, table_ref[bk, hd] * LOG2E, row)
    base = jnp.concatenate([jnp.broadcast_to(row, (nq, LANES)),
                            jnp.full((nq, LANES), NEG2, _F32)], axis=1)
    return pltpu.roll(base, 1, 1, stride=1, stride_axis=0)


def _norm_heads(q, seg, gqs):
    n = q.shape[0]
    nc = D_MODEL // LANES
    sq = jnp.concatenate([q[:, c * LANES:(c + 1) * LANES] for c in range(nc)], axis=0)
    ss = jnp.dot((sq * sq).astype(_BF), seg, preferred_element_type=_F32)
    r = lax.rsqrt(ss * (1.0 / HEAD_DIM) + EPS)
    return [q[:, c * LANES:(c + 1) * LANES] * r[c * n:(c + 1) * n, :] * gqs for c in range(nc)]


def _sink_softmax2(logits2, sink2):
    m = jnp.maximum(jnp.max(logits2, axis=-1, keepdims=True), sink2)
    e = jnp.exp2(logits2 - m)
    denom = jnp.sum(e, axis=-1, keepdims=True) + jnp.exp2(sink2 - m)
    return e * (1.0 / denom)


def _dot_t(a, b):
    return lax.dot_general(a, b, (((1,), (1,)), ((), ())), preferred_element_type=_F32)


def _attend_blocks(blocks, sinks_ref, seg, gqs):
    units = []
    for (q, kd, vd, bias_of) in blocks:
        nq = q.shape[0]
        low = _low_lanes(nq)
        qn = _norm_heads(q, seg, gqs)
        for g in range(N_KV_HEADS):
            qs = jnp.concatenate(
                [jnp.where(low if hl % 2 == 0 else jnp.logical_not(low), qn[2 * g + hl // 2], 0.0)
                 for hl in range(GROUP)], axis=0).astype(_BF)
            units.append((g, nq, bias_of, _dot_t(qs, kd[:, g * LANES:(g + 1) * LANES]),
                          vd[:, g * LANES:(g + 1) * LANES]))
    probs = []
    for (g, nq, bias_of, s, _) in units:
        probs.append(jnp.concatenate(
            [_sink_softmax2(s[hl * nq:(hl + 1) * nq, :] + bias_of(g * GROUP + hl),
                            sinks_ref[g * GROUP + hl] * LOG2E) for hl in range(GROUP)],
            axis=0).astype(_BF))
    res, outs = [], []
    for (g, nq, _, _, vg), p in zip(units, probs):
        low_k = _low_lanes(vg.shape[0])
        zero = jnp.zeros_like(vg)
        vlo, vhi = jnp.where(low_k, vg, zero), jnp.where(low_k, zero, vg)
        for pair in range(2):
            r0 = 2 * pair * nq
            outs.append(jnp.dot(p[r0:r0 + nq], vlo, preferred_element_type=_F32)
                        + jnp.dot(p[r0 + nq:r0 + 2 * nq], vhi, preferred_element_type=_F32))
        if g == N_KV_HEADS - 1:
            res.append(jnp.concatenate(outs, axis=1))
            outs = []
    return res


def _attn_prompt_kernel(q_ref, kp_ref, kc_ref, vp_ref, vc_ref, revb_ref, table_ref,
                        sinks_ref, gq_ref, seg_ref, o_ref, bias_ref):
    W = WINDOW
    i = pl.program_id(1)

    @pl.when(jnp.logical_and(pl.program_id(0) == 0, i == 0))
    def _():
        first_half = lax.broadcasted_iota(jnp.int32, (W, 2 * W), 1) < W
        for hd in range(N_HEADS):
            rows = _bias_rows(revb_ref, table_ref, hd, W)
            bias_ref[0, hd] = rows
            bias_ref[1, hd] = jnp.where(first_half, NEG2, rows)

    gqs = gq_ref[...] * (SCALE * LOG2E)
    kp, kc = kp_ref[...], kc_ref[...]
    vp, vc = vp_ref[...], vc_ref[...]
    first = jnp.where(i == 0, 1, 0)
    blocks = []
    for blk in range(ATT_BLOCKS):
        if blk == 0:
            kd = jnp.concatenate([kp, kc[0:W]], axis=0)
            vd = jnp.concatenate([vp, vc[0:W]], axis=0)
            bias_of = lambda hd: bias_ref[first, hd]
        else:
            kd, vd = kc[(blk - 1) * W:(blk + 1) * W], vc[(blk - 1) * W:(blk + 1) * W]
            bias_of = lambda hd: bias_ref[0, hd]
        blocks.append((q_ref[blk * W:(blk + 1) * W, :], kd, vd, bias_of))
    for blk, o in enumerate(_attend_blocks(blocks, sinks_ref, seg_ref[...], gqs)):
        o_ref[blk * W:(blk + 1) * W, :] = o.astype(o_ref.dtype)


def _attn_prompt(q2d, kd2d, vd2d, revb, table, sinks, gq2, seg):
    W = WINDOW
    R = ATT_BLOCKS * W
    ns = SEQ // R
    cur = lambda b, i: (b * ns + i, 0)
    prev = lambda b, i: (b * (SEQ // W) + jnp.maximum(ATT_BLOCKS * i - 1, 0), 0)
    smem = pl.BlockSpec(memory_space=pltpu.SMEM)
    return pl.pallas_call(
        _attn_prompt_kernel,
        grid=(BATCH, ns),
        in_specs=[pl.BlockSpec((R, D_MODEL), cur),
                  pl.BlockSpec((W, 2 * KD), prev), pl.BlockSpec((R, 2 * KD), cur),
                  pl.BlockSpec((W, 2 * KD), prev), pl.BlockSpec((R, 2 * KD), cur),
                  _const_spec((1, LANES)), smem, smem, _const_spec((1, LANES)),
                  _const_spec((LANES, LANES))],
        out_specs=pl.BlockSpec((R, D_MODEL), cur),
        out_shape=jax.ShapeDtypeStruct((BATCH * SEQ, D_MODEL), _BF),
        scratch_shapes=[pltpu.VMEM((2, N_HEADS, W, 2 * W), _F32)],
        compiler_params=_params(2),
        name="attn_prompt",
    )(q2d, kd2d, kd2d, vd2d, vd2d, revb, table, sinks, gq2, seg)


def _step_logits(q, kb, bias_all, gqs):
    n = DEC_SEQ
    low = _low_lanes(n)
    zero = jnp.zeros((n, LANES), _F32)
    rows = []
    for hd in range(N_HEADS):
        g = hd // GROUP
        x = q[:, (hd // 2) * LANES:(hd // 2 + 1) * LANES]
        if hd % 2 != g % 2:
            x = pltpu.roll(x, HEAD_DIM, 1)
        x = jnp.where(low if g % 2 == 0 else jnp.logical_not(low), x, 0.0)
        ss = jnp.sum(x * x, axis=-1, keepdims=True) * (1.0 / HEAD_DIM)
        x = x * lax.rsqrt(ss + EPS) * gqs
        rows.append(jnp.concatenate([x, zero] if g // 2 == 0 else [zero, x], axis=1))
    q_all = jnp.concatenate(rows, axis=0).astype(_BF)
    return _dot_t(q_all, kb) + bias_all


def _split_heads(o_all):
    n = DEC_SEQ
    low = _low_lanes(n)
    outs = []
    for c in range(N_HEADS // 2):
        halves = []
        for hf in range(2):
            hd = 2 * c + hf
            g = hd // GROUP
            y = o_all[hd * n:(hd + 1) * n, (g // 2) * LANES:(g // 2 + 1) * LANES]
            halves.append(y if g % 2 == hf else pltpu.roll(y, HEAD_DIM, 1))
        outs.append(jnp.where(low, halves[0], halves[1]))
    return jnp.concatenate(outs, axis=1)


def _attn_sample_kernel(q_ref, kn_ref, vn_ref, ck_ref, cv_ref, revb_ref, table_ref,
                        sinks_ref, gq_ref, o_ref, ko_ref, vo_ref, bias_ref, kbuf_ref, vbuf_ref):
    W = WINDOW
    L = 2 * W
    U = ATT_UNROLL

    @pl.when(pl.program_id(0) == 0)
    def _():
        for hd in range(N_HEADS):
            bias_ref[hd * DEC_SEQ:(hd + 1) * DEC_SEQ, :] = _bias_rows(revb_ref, table_ref, hd,
                                                                      DEC_SEQ)
        kbuf_ref[...] = jnp.zeros((U, L, KD), _F32)
        vbuf_ref[...] = jnp.zeros((U, L, KD), _F32)

    gqs = gq_ref[...] * (SCALE * LOG2E)
    bias_all = bias_ref[...]
    sink_col = jnp.concatenate(
        [jnp.full((DEC_SEQ, 1), sinks_ref[hd] * LOG2E, _F32) for hd in range(N_HEADS)], axis=0)

    def body(it, carry):
        logits = []
        for u in range(U):
            s = it * U + u
            kbuf_ref[u, 0:W, :] = ck_ref[s]
            vbuf_ref[u, 0:W, :] = cv_ref[s]
            kbuf_ref[u, W:W + DEC_SEQ, :] = kn_ref[s]
            vbuf_ref[u, W:W + DEC_SEQ, :] = vn_ref[s]
            ko_ref[s] = kbuf_ref[u, DEC_SEQ:DEC_SEQ + W, :]
            vo_ref[s] = vbuf_ref[u, DEC_SEQ:DEC_SEQ + W, :]
            logits.append(_step_logits(q_ref[s], kbuf_ref[u].astype(_BF), bias_all, gqs))
        probs = [_sink_softmax2(l, sink_col).astype(_BF) for l in logits]
        for u in range(U):
            o_all = jnp.dot(probs[u], vbuf_ref[u].astype(_BF), preferred_element_type=_F32)
            o_ref[it * U + u] = _split_heads(o_all)
        return carry

    lax.fori_loop(0, ATT_SEQS // U, body, 0)


def _attn_sample(q3d, kn3d, vn3d, ck, cv, revb, table, sinks, gq2):
    W = WINDOW
    A = ATT_SEQS
    blk = lambda r, c: pl.BlockSpec((A, r, c), lambda i: (i, 0, 0))
    smem = pl.BlockSpec(memory_space=pltpu.SMEM)
    return pl.pallas_call(
        _attn_sample_kernel,
        grid=(DEC_BATCH // A,),
        in_specs=[blk(DEC_SEQ, D_MODEL), blk(DEC_SEQ, KD), blk(DEC_SEQ, KD),
                  blk(W, KD), blk(W, KD),
                  _const_spec((1, LANES)), smem, smem, _const_spec((1, LANES))],
        out_specs=[blk(DEC_SEQ, D_MODEL), blk(W, KD), blk(W, KD)],
        out_shape=[jax.ShapeDtypeStruct((DEC_BATCH, DEC_SEQ, D_MODEL), _F32),
                   jax.ShapeDtypeStruct((DEC_BATCH, W, KD), _F32),
                   jax.ShapeDtypeStruct((DEC_BATCH, W, KD), _F32)],
        scratch_shapes=[pltpu.VMEM((N_HEADS * DEC_SEQ, 2 * W), _F32),
                        pltpu.VMEM((ATT_UNROLL, 2 * W, KD), _F32),
                        pltpu.VMEM((ATT_UNROLL, 2 * W, KD), _F32)],
        compiler_params=_params(1),
        name="attn_sample",
    )(q3d, kn3d, vn3d, ck, cv, revb, table, sinks, gq2)


def kernel(x_prompt, x_sample, state_conv, state_ffn, cache_k, cache_v, p_prompt, p_sample, g_mix, cm_w_pw1, cm_b_pw1, cm_w_dw, cm_b_dw, cm_ln_g, cm_ln_b, cm_w_pw2, cm_b_pw2, at_w_q, at_g_q, at_sinks, at_w_o, kv_g, kv_w_k, kv_w_v, kv_g_k, rel_bias, g_ffn, ffn_w_up, ffn_w_dw, ffn_b_dw, ffn_w_down, g_ple, ple_w_gate, ple_w_proj):
    D, F = D_MODEL, D_FF
    row = lambda a: a.reshape(1, -1)

    mixer_w = [row(g_mix[0]), cm_w_pw1[0], row(cm_b_pw1[0]), cm_w_dw[0], row(cm_b_dw[0]),
               row(cm_ln_g[0]), row(cm_ln_b[0]), cm_w_pw2[0], row(cm_b_pw2[0])]
    g1, conv_s_tm, w_pw1_bf, w_pw2_bf = _mixer_sample(
        x_sample.reshape(DEC_BATCH * DEC_SEQ, D), jnp.transpose(state_conv[0], (1, 0, 2)), mixer_w)
    conv_s = jnp.transpose(conv_s_tm, (1, 0, 2))
    mixer_w[1], mixer_w[7] = w_pw1_bf, w_pw2_bf
    cast_srcs = [ffn_w_up.reshape(2 * D, 2 * F), ffn_w_down.reshape(2 * F, D),
                 ple_w_gate.reshape(2 * D, D), ple_w_proj.reshape(2 * PLE_DIM, D),
                 at_w_q[0], at_w_o[0], kv_w_k, kv_w_v]

    xp = x_prompt.reshape(BATCH * SEQ, D)
    pp = p_prompt.reshape(2, BATCH * SEQ, PLE_DIM)
    (h1, conv_p, w_up_bf, w_down_bf, w_gate_bf, w_proj_bf, w_q_bf, w_o_bf, w_k_bf,
     w_v_bf) = _mixer_prompt(xp, mixer_w, cast_srcs)

    stacked = dict(
        g_ffn=g_ffn[:, None, :], w_up=w_up_bf.reshape(2, D, 2 * F), w_dw=ffn_w_dw,
        b_dw=ffn_b_dw[:, None, :], w_down=w_down_bf.reshape(2, F, D), g_ple=g_ple[:, None, :],
        w_gate=w_gate_bf.reshape(2, D, D), w_proj=w_proj_bf.reshape(2, PLE_DIM, D))
    ffn_w = [dict(stacked), dict(stacked)]
    ffn_w[0].update(g_kv=row(kv_g), w_k=w_k_bf, w_v=w_v_bf,
                    g_k2=row(jnp.tile(kv_g_k, 2)), g_q_in=row(g_mix[1]), w_q=w_q_bf)
    ffn_w[1].update(wo=w_o_bf)
    gq2 = row(jnp.tile(at_g_q[0], 2))
    sinks = at_sinks[0]

    W = WINDOW
    revb = jnp.asarray(_t5_buckets(W - 1 - np.arange(W))[None, :])
    half = np.arange(LANES) // HEAD_DIM
    seg = jnp.asarray(half[:, None] == half[None, :], dtype=_BF)

    h3, ffn_p0, k_p, v_p, q_p, kd_p, vd_p = _ffn_call(
        h1, pp, None, None, ffn_w[0], layer=0, prompt=True, with_oproj=False, with_kvq=True)
    o_p = _attn_prompt(q_p, kd_p, vd_p, revb, rel_bias, sinks, gq2, seg)
    y_p, ffn_p = _ffn_call(h3, pp, None, o_p, ffn_w[1], layer=1, prompt=True,
                           with_oproj=True, with_kvq=False, prev_state=ffn_p0)

    ps = p_sample.reshape(2, DEC_BATCH * DEC_SEQ, PLE_DIM)
    g3, ffn_s0, k_s, v_s, q_s = _ffn_call(g1, ps, state_ffn, None, ffn_w[0], layer=0,
                                          prompt=False, with_oproj=False, with_kvq=True)
    o_s, k_out, v_out = _attn_sample(
        q_s.reshape(DEC_BATCH, DEC_SEQ, D), k_s.reshape(DEC_BATCH, DEC_SEQ, KD),
        v_s.reshape(DEC_BATCH, DEC_SEQ, KD), cache_k.reshape(DEC_BATCH, W, KD),
        cache_v.reshape(DEC_BATCH, W, KD), revb, rel_bias, sinks, gq2)
    y_s, ffn_s = _ffn_call(g3, ps, state_ffn, o_s.reshape(DEC_BATCH * DEC_SEQ, D),
                           ffn_w[1], layer=1, prompt=False, with_oproj=True, with_kvq=False,
                           prev_state=ffn_s0)

    kv_shape_p = (BATCH, W, N_KV_HEADS, HEAD_DIM)
    kv_shape_s = (DEC_BATCH, W, N_KV_HEADS, HEAD_DIM)
    return (y_p.reshape(BATCH, SEQ, D), y_s.reshape(DEC_BATCH, DEC_SEQ, D),
            conv_p[None], conv_s[None],
            ffn_p, ffn_s,
            k_p.reshape(BATCH, SEQ, KD)[:, SEQ - W:].reshape(kv_shape_p),
            k_out.reshape(kv_shape_s),
            v_p.reshape(BATCH, SEQ, KD)[:, SEQ - W:].reshape(kv_shape_p),
            v_out.reshape(kv_shape_s))
```

```python
import functools
import math

import jax
import jax.numpy as jnp
import numpy as np
from jax import lax
from jax.experimental import pallas as pl
from jax.experimental.pallas import tpu as pltpu

D_MODEL = 1024
BATCH = 8
SEQ = 2048
DEC_BATCH = 128
DEC_SEQ = 8
CONV_K = 31
FFN_CONV_K = 3
D_FF = 2816
N_HEADS = 16
N_KV_HEADS = 4
HEAD_DIM = 64
GROUP = N_HEADS // N_KV_HEADS
WINDOW = 128
N_BUCKETS = 32
MAX_DISTANCE = 128
PLE_DIM = 256
EPS = 1e-6
SCALE = HEAD_DIM ** -0.5
NEG = -1e30
KD = N_KV_HEADS * HEAD_DIM

LANES = 128
SUBLANES = 8
V7X_VMEM_BYTES = 64 * 1024 * 1024
VMEM_LIMIT = V7X_VMEM_BYTES - 4 * 1024 * 1024

T_TILE = 512
MIX_TILE = 1024
S_TILE = 32
FFN_S_TILE = 32
CONV_HALO = 32
FFN_HALO = 8
F_CHUNK = 2816
D_SLABS = D_MODEL // LANES
F_SLABS = D_FF // LANES
ATT_SEQS = 16
ATT_UNROLL = 8
ATT_BLOCKS = 2

_BF = jnp.bfloat16
_F32 = jnp.float32


def _bdot(a, w):
    return jnp.dot(a.astype(_BF), w, preferred_element_type=_F32)


def _rms(x, g):
    return x * lax.rsqrt(jnp.mean(x * x, axis=-1, keepdims=True) + EPS) * g


HALF = 0.5


def _one_plus_tanh(half_x):
    return 1.0 + jnp.tanh(half_x)


def _const_spec(shape):
    nd = len(shape)
    return pl.BlockSpec(shape, lambda *_: (0,) * nd, pipeline_mode=pl.Buffered(1))


def _layer_spec(shape, layer):
    nd = len(shape)
    return pl.BlockSpec((None,) + tuple(shape), lambda *_: (layer,) + (0,) * nd,
                        pipeline_mode=pl.Buffered(1))


def _params(n_grid):
    return pltpu.CompilerParams(dimension_semantics=("arbitrary",) * n_grid,
                                vmem_limit_bytes=VMEM_LIMIT)


def _glu_slab(u, b1_ref, c):
    ba = b1_ref[:, c * LANES:(c + 1) * LANES]
    bg = b1_ref[:, D_MODEL + c * LANES:D_MODEL + (c + 1) * LANES]
    return (u[:, :LANES] + HALF * ba) * _one_plus_tanh(u[:, LANES:] + HALF * bg)


def _mixer_tail(x, c, bdw, lng, lnb, w2, b2):
    c = c + bdw
    mu = jnp.mean(c, axis=-1, keepdims=True)
    cz = c - mu
    var = jnp.mean(cz * cz, axis=-1, keepdims=True)
    hy = cz * lax.rsqrt(var + EPS) * (HALF * lng) + HALF * lnb
    s = hy * _one_plus_tanh(hy)
    return x + _bdot(s, w2) + b2


def _mixer_prompt_kernel(x_ref, g_ref, w1_ref, b1_ref, wdw_ref, bdw_ref, lng_ref, lnb_ref,
                         w2_ref, b2_ref, *rest):
    n_cast = len(_CAST_ROWS)
    cast_in, rest = rest[:n_cast], rest[n_cast:]
    h_ref, cs_ref = rest[0], rest[1]
    cast_out, (slab_ref, oslab_ref) = rest[2:2 + n_cast], rest[2 + n_cast:]
    T = MIX_TILE
    H = T // 2
    j = pl.program_id(1)

    for src, dst, scale in zip(cast_in, cast_out, _CAST_SCALE):
        dst[...] = (src[...] if scale == 1.0 else scale * src[...]).astype(_BF)

    @pl.when(j == 0)
    def _():
        slab_ref[:, 0:CONV_HALO, :] = jnp.zeros((D_SLABS, CONV_HALO, LANES), _F32)

    x = x_ref[...]
    xn = _rms(x, g_ref[...]).astype(_BF)
    base = CONV_HALO - (CONV_K - 1)
    tails = []
    for c in range(D_SLABS):
        u = jnp.dot(xn, w1_ref[:, 2 * c * LANES:2 * (c + 1) * LANES],
                    preferred_element_type=_F32)
        glu = _glu_slab(u, b1_ref, c)
        slab_ref[c, CONV_HALO:CONV_HALO + T, :] = glu
        tails.append(glu[T - (CONV_K - 1):, :])
        wc = wdw_ref[:, c * LANES:(c + 1) * LANES]
        for p in range(2):
            acc = slab_ref[c, pl.ds(base + p, H, stride=2), :] * wc[0:1, :]
            for k in range(1, CONV_K):
                acc = acc + slab_ref[c, pl.ds(base + k + p, H, stride=2), :] * wc[k:k + 1, :]
            oslab_ref[c, pl.ds(p, H, stride=2), :] = acc
    cs_ref[...] = jnp.concatenate(tails, axis=1)
    conv = jnp.concatenate([oslab_ref[c] for c in range(D_SLABS)], axis=1)
    h_ref[...] = _mixer_tail(x, conv, bdw_ref[...], lng_ref[...], lnb_ref[...],
                             w2_ref[...], b2_ref[...])
    for c in range(D_SLABS):
        slab_ref[c, 0:CONV_HALO, :] = slab_ref[c, T:T + CONV_HALO, :]


def _mixer_sample_kernel(x_ref, st_ref, g_ref, w1f_ref, b1_ref, wdw_ref, bdw_ref, lng_ref,
                         lnb_ref, w2f_ref, b2_ref, h_ref, cs_ref, w1_ref, w2_ref):
    S = S_TILE
    KH = CONV_K - 1

    @pl.when(pl.program_id(0) == 0)
    def _():
        for c in range(D_SLABS):
            for part in range(2):
                src = w1f_ref[:, part * D_MODEL + c * LANES:part * D_MODEL + (c + 1) * LANES]
                w1_ref[:, (2 * c + part) * LANES:(2 * c + part + 1) * LANES] = (
                    HALF * src).astype(_BF)
        w2_ref[...] = w2f_ref[...].astype(_BF)

    x = x_ref[...]
    u = _bdot(_rms(x, g_ref[...]), w1_ref[...])
    glu = jnp.concatenate(
        [_glu_slab(u[:, 2 * c * LANES:2 * (c + 1) * LANES], b1_ref, c) for c in range(D_SLABS)],
        axis=1)
    glu_tm = jnp.swapaxes(glu.reshape(S, DEC_SEQ, D_MODEL), 0, 1)
    cs_ref[0:KH - DEC_SEQ] = st_ref[DEC_SEQ:KH]
    cs_ref[KH - DEC_SEQ:KH] = glu_tm

    def plane(j):
        return st_ref[j] if j < KH else glu_tm[j - KH]

    outs = []
    for t in range(DEC_SEQ):
        acc = plane(t) * wdw_ref[0:1, :]
        for k in range(1, CONV_K):
            acc = acc + plane(t + k) * wdw_ref[k:k + 1, :]
        outs.append(acc)
    conv = jnp.swapaxes(jnp.stack(outs, axis=0), 0, 1).reshape(S * DEC_SEQ, D_MODEL)
    h_ref[...] = _mixer_tail(x, conv, bdw_ref[...], lng_ref[...], lnb_ref[...],
                             w2_ref[...], b2_ref[...])


def _mixer_weight_specs():
    D = D_MODEL
    return [_const_spec((1, D)), _const_spec((D, 2 * D)), _const_spec((1, 2 * D)),
            _const_spec((CONV_K, D)), _const_spec((1, D)), _const_spec((1, D)),
            _const_spec((1, D)), _const_spec((D, D)), _const_spec((1, D))]


_CAST_ROWS = (2 * D_MODEL, 2 * D_FF, 2 * D_MODEL, 2 * PLE_DIM) + (D_MODEL,) * 4
_CAST_SCALE = (1.0, 1.0, HALF, HALF) + (1.0,) * 4


def _mixer_prompt(x2d, wts, cast_srcs):
    T = MIX_TILE
    nt = SEQ // T
    steps = BATCH * nt
    row = lambda b, j: (b * nt + j, 0)
    cast_specs, cast_shapes = [], []
    for src, rows in zip(cast_srcs, _CAST_ROWS):
        assert src.shape[0] == rows and rows % steps == 0
        cast_specs.append(pl.BlockSpec((rows // steps, src.shape[1]), row))
        cast_shapes.append(jax.ShapeDtypeStruct(src.shape, _BF))
    return pl.pallas_call(
        _mixer_prompt_kernel,
        grid=(BATCH, nt),
        in_specs=[pl.BlockSpec((T, D_MODEL), row)] + _mixer_weight_specs() + cast_specs,
        out_specs=[pl.BlockSpec((T, D_MODEL), row),
                   pl.BlockSpec((None, CONV_K - 1, D_MODEL), lambda b, j: (b, 0, 0))] + cast_specs,
        out_shape=[jax.ShapeDtypeStruct((BATCH * SEQ, D_MODEL), _F32),
                   jax.ShapeDtypeStruct((BATCH, CONV_K - 1, D_MODEL), _F32)] + cast_shapes,
        scratch_shapes=[pltpu.VMEM((D_SLABS, CONV_HALO + T, LANES), _F32),
                        pltpu.VMEM((D_SLABS, T, LANES), _F32)],
        compiler_params=_params(2),
        name="mixer_prompt",
    )(x2d, *wts, *cast_srcs)


def _mixer_sample(x2d, state_tm, wts):
    S = S_TILE
    R = S * DEC_SEQ
    rows = pl.BlockSpec((R, D_MODEL), lambda i: (i, 0))
    planes = pl.BlockSpec((CONV_K - 1, S, D_MODEL), lambda i: (0, i, 0))
    w1_shape, w2_shape = (D_MODEL, 2 * D_MODEL), (D_MODEL, D_MODEL)
    whole = lambda shape: pl.BlockSpec(shape, lambda i: (0, 0))
    return pl.pallas_call(
        _mixer_sample_kernel,
        grid=(DEC_BATCH // S,),
        in_specs=[rows, planes] + _mixer_weight_specs(),
        out_specs=[rows, planes, whole(w1_shape), whole(w2_shape)],
        out_shape=[jax.ShapeDtypeStruct((DEC_BATCH * DEC_SEQ, D_MODEL), _F32),
                   jax.ShapeDtypeStruct((CONV_K - 1, DEC_BATCH, D_MODEL), _F32),
                   jax.ShapeDtypeStruct(w1_shape, _BF), jax.ShapeDtypeStruct(w2_shape, _BF)],
        compiler_params=_params(1),
        name="mixer_sample",
    )(x2d, state_tm, *wts)


def _ffn_kernel(*refs, prompt, with_oproj, with_kvq):
    it = iter(refs)
    h_ref = next(it)
    p_ref = next(it)
    st_ref = None if prompt else next(it)
    if with_oproj:
        o_ref = next(it)
        wo_ref = next(it)
        fs0_ref = next(it)
    gf_ref, wup_ref, wdw_ref, bdw_ref, wdn_ref = (next(it) for _ in range(5))
    gp_ref, wg_ref, wp_ref = (next(it) for _ in range(3))
    if with_kvq:
        gkv_ref, wk_ref, wv_ref, gk_ref, gq_ref, wq_ref = (next(it) for _ in range(6))
    hout_ref = next(it)
    fs_ref = next(it)
    dup_kv = with_kvq and prompt
    if with_kvq:
        k_ref, v_ref, q_ref = (next(it) for _ in range(3))
    if dup_kv:
        kd_ref, vd_ref = next(it), next(it)
    if prompt:
        slab_ref, oslab_ref = next(it), next(it)

    F = D_FF
    R = T_TILE if prompt else FFN_S_TILE * DEC_SEQ
    H = R // 2
    KH = FFN_CONV_K - 1
    assert F_CHUNK == F or prompt

    if prompt:
        @pl.when(pl.program_id(1) == 0)
        def _():
            slab_ref[:, 0:FFN_HALO, :] = jnp.zeros((2 * F_SLABS, FFN_HALO, LANES), _F32)

    h = h_ref[...]
    if with_oproj:
        h = h + _bdot(o_ref[...], wo_ref[...])
    xn = _rms(h, gf_ref[...]).astype(_BF)

    n_chunks = F // F_CHUNK
    spc = F_CHUNK // LANES
    acc = None
    tails = []
    for ci in range(n_chunks):
        halves = []
        for half in range(2):
            c0 = half * F + ci * F_CHUNK
            uc = jnp.dot(xn, wup_ref[:, c0:c0 + F_CHUNK], preferred_element_type=_F32)
            w = wdw_ref[:, c0:c0 + F_CHUNK]
            b = bdw_ref[:, c0:c0 + F_CHUNK]
            if half == 0:
                w, b = HALF * w, HALF * b
            if prompt:
                cols = []
                for s in range(spc):
                    sl = half * F_SLABS + ci * spc + s
                    slab_ref[sl, FFN_HALO:FFN_HALO + R, :] = uc[:, s * LANES:(s + 1) * LANES]
                    ws = w[:, s * LANES:(s + 1) * LANES]
                    ph = []
                    for p in range(2):
                        cv = b[:, s * LANES:(s + 1) * LANES]
                        for k in range(FFN_CONV_K):
                            cv = cv + (slab_ref[sl, pl.ds(FFN_HALO - KH + k + p, H, stride=2), :]
                                       * ws[k:k + 1, :])
                        ph.append(cv)
                    cols.append(jnp.concatenate(ph, axis=0))
                halves.append(jnp.concatenate(cols, axis=1))
            else:
                u3 = uc.reshape(FFN_S_TILE, DEC_SEQ, F_CHUNK)
                hist = st_ref[:, :, c0:c0 + F_CHUNK]
                h0 = jnp.broadcast_to(hist[:, 0:1, :], u3.shape)
                h1 = jnp.broadcast_to(hist[:, 1:2, :], u3.shape)
                step = lax.broadcasted_iota(jnp.int32, u3.shape, 1)
                um1 = jnp.where(step == 0, h1, pltpu.roll(u3, 1, 1))
                um2 = jnp.where(step == 0, h0, jnp.where(step == 1, h1, pltpu.roll(u3, 2, 1)))
                cv = b + um2 * w[0:1, :] + um1 * w[1:2, :] + u3 * w[2:3, :]
                tails.append(u3[:, DEC_SEQ - KH:, :])
                halves.append(cv.reshape(R, F_CHUNK))
        half_gate, val = halves
        act = (half_gate * _one_plus_tanh(half_gate) * val).astype(_BF)
        part = jnp.dot(act, wdn_ref[ci * F_CHUNK:(ci + 1) * F_CHUNK, :],
                       preferred_element_type=_F32)
        acc = part if acc is None else acc + part

    if prompt:
        for c in range(D_SLABS):
            oslab_ref[c, pl.ds(0, H, stride=2), :] = acc[:H, c * LANES:(c + 1) * LANES]
            oslab_ref[c, pl.ds(1, H, stride=2), :] = acc[H:, c * LANES:(c + 1) * LANES]
        f = jnp.concatenate([oslab_ref[c] for c in range(D_SLABS)], axis=1)
        new_state = jnp.concatenate(
            [slab_ref[s, FFN_HALO + R - SUBLANES:FFN_HALO + R, :] for s in range(2 * F_SLABS)],
            axis=1)[SUBLANES - KH:, :]
        for s in range(2 * F_SLABS):
            slab_ref[s, 0:FFN_HALO, :] = slab_ref[s, R:R + FFN_HALO, :]
    else:
        f = acc
        new_state = jnp.concatenate(tails, axis=2)
    if with_oproj:
        fs_ref[0] = fs0_ref[...]
        fs_ref[1] = new_state
    else:
        fs_ref[...] = new_state

    h2 = h + f
    h3 = h2 + (_one_plus_tanh(_bdot(_rms(h2, gp_ref[...]), wg_ref[...]))
               * _bdot(p_ref[...], wp_ref[...]))
    hout_ref[...] = h3

    if with_kvq:
        hk = _rms(h3, gkv_ref[...]).astype(_BF)
        kk = jnp.dot(hk, wk_ref[...], preferred_element_type=_F32)
        vv = jnp.dot(hk, wv_ref[...], preferred_element_type=_F32)
        v_ref[...] = vv
        low = lax.broadcasted_iota(jnp.int32, (R, LANES), 1) < HEAD_DIM

        def dup_heads(x):
            out = []
            for c in range(KD // LANES):
                xc = x[:, c * LANES:(c + 1) * LANES]
                xr = pltpu.roll(xc, HEAD_DIM, 1)
                out += [jnp.where(low, xc, xr), jnp.where(low, xr, xc)]
            return out

        kn = []
        for kg in dup_heads(kk):
            r = lax.rsqrt(jnp.mean(kg * kg, axis=-1, keepdims=True) + EPS)
            kn.append(kg * r * gk_ref[...])
        k_ref[...] = jnp.concatenate(
            [jnp.where(low, kn[2 * c], kn[2 * c + 1]) for c in range(KD // LANES)], axis=1)
        if dup_kv:
            kd_ref[...] = jnp.concatenate(kn, axis=1).astype(_BF)
            vd_ref[...] = jnp.concatenate(dup_heads(vv), axis=1).astype(_BF)
        q_ref[...] = _bdot(_rms(h3, gq_ref[...]), wq_ref[...])


def _ffn_call(h2d, p3d, state4d, o2d, w, *, layer, prompt, with_oproj, with_kvq,
              prev_state=None):
    D, F = D_MODEL, D_FF
    if prompt:
        R = T_TILE
        nt = SEQ // R
        grid = (BATCH, nt)
        row = lambda b, j: (b * nt + j, 0)
        p_spec = pl.BlockSpec((None, R, PLE_DIM), lambda b, j: (layer, b * nt + j, 0))
        n_rows, n_seq = BATCH * SEQ, BATCH
        fs_spec = pl.BlockSpec((None, FFN_CONV_K - 1, 2 * F), lambda b, j: (b, 0, 0))
        scratch = [pltpu.VMEM((2 * F_SLABS, FFN_HALO + R, LANES), _F32),
                   pltpu.VMEM((D_SLABS, R, LANES), _F32)]
    else:
        R = FFN_S_TILE * DEC_SEQ
        grid = (DEC_BATCH // FFN_S_TILE,)
        row = lambda i: (i, 0)
        p_spec = pl.BlockSpec((None, R, PLE_DIM), lambda i: (layer, i, 0))
        n_rows, n_seq = DEC_BATCH * DEC_SEQ, DEC_BATCH
        fs_spec = pl.BlockSpec((FFN_S_TILE, FFN_CONV_K - 1, 2 * F), lambda i: (i, 0, 0))
        scratch = []

    args = [h2d, p3d]
    in_specs = [pl.BlockSpec((R, D), row), p_spec]
    if not prompt:
        args.append(state4d)
        in_specs.append(pl.BlockSpec((None, FFN_S_TILE, FFN_CONV_K - 1, 2 * F),
                                     lambda i: (layer, i, 0, 0)))
    fs_shape = (n_seq, FFN_CONV_K - 1, 2 * F)
    if with_oproj:
        args += [o2d, w["wo"], prev_state]
        in_specs += [pl.BlockSpec((R, D), row), _const_spec((D, D)), fs_spec]
        blk, imap = fs_spec.block_shape, fs_spec.index_map
        fs_spec = pl.BlockSpec((2,) + tuple(blk), lambda *g: (0,) + tuple(imap(*g)))
        fs_shape = (2,) + fs_shape
    args += [w["g_ffn"], w["w_up"], w["w_dw"], w["b_dw"], w["w_down"],
             w["g_ple"], w["w_gate"], w["w_proj"]]
    in_specs += [_layer_spec((1, D), layer), _layer_spec((D, 2 * F), layer),
                 _layer_spec((FFN_CONV_K, 2 * F), layer), _layer_spec((1, 2 * F), layer),
                 _layer_spec((F, D), layer), _layer_spec((1, D), layer),
                 _layer_spec((D, D), layer), _layer_spec((PLE_DIM, D), layer)]
    out_specs = [pl.BlockSpec((R, D), row), fs_spec]
    out_shape = [jax.ShapeDtypeStruct((n_rows, D), _F32),
                 jax.ShapeDtypeStruct(fs_shape, _F32)]
    if with_kvq:
        args += [w["g_kv"], w["w_k"], w["w_v"], w["g_k2"], w["g_q_in"], w["w_q"]]
        in_specs += [_const_spec((1, D)), _const_spec((D, KD)), _const_spec((D, KD)),
                     _const_spec((1, LANES)), _const_spec((1, D)), _const_spec((D, D))]
        out_specs += [pl.BlockSpec((R, KD), row), pl.BlockSpec((R, KD), row),
                      pl.BlockSpec((R, D), row)]
        out_shape += [jax.ShapeDtypeStruct((n_rows, KD), _F32),
                      jax.ShapeDtypeStruct((n_rows, KD), _F32),
                      jax.ShapeDtypeStruct((n_rows, D), _F32)]
        if prompt:
            out_specs += [pl.BlockSpec((R, 2 * KD), row), pl.BlockSpec((R, 2 * KD), row)]
            out_shape += [jax.ShapeDtypeStruct((n_rows, 2 * KD), _BF),
                          jax.ShapeDtypeStruct((n_rows, 2 * KD), _BF)]
    kern = functools.partial(_ffn_kernel, prompt=prompt, with_oproj=with_oproj,
                             with_kvq=with_kvq)
    name = "ffn_%s_%d" % ("prompt" if prompt else "sample", layer)
    return pl.pallas_call(
        kern, grid=grid, in_specs=in_specs, out_specs=out_specs, out_shape=out_shape,
        scratch_shapes=scratch, compiler_params=_params(len(grid)), name=name,
    )(*args)


def _t5_buckets(dist):
    max_exact = N_BUCKETS // 2
    d = np.maximum(dist, 0)
    df = np.maximum(d, 1).astype(np.float32)
    large = max_exact + (np.log(df / np.float32(max_exact))
                         / np.float32(math.log(MAX_DISTANCE / max_exact))
                         * np.float32(N_BUCKETS - max_exact)).astype(np.int32)
    large = np.minimum(large, N_BUCKETS - 1)
    return np.where(d < max_exact, d, large).astype(np.int32)


LOG2E = math.log2(math.e)
NEG2 = NEG * LOG2E


def _low_lanes(rows):
    return lax.broadcasted_iota(jnp.int32, (rows, LANES), 1) < HEAD_DIM


def _bias_rows(revb_ref, table_ref, hd, nq):
    revb = revb_ref[...]
    row = jnp.zeros(revb.shape, _F32)
    for bk in range(N_BUCKETS):
        row = jnp.where(revb == bk, table_ref[bk, hd] * LOG2E, row)
    base = jnp.concatenate([jnp.broadcast_to(row, (nq, LANES)),
                            jnp.full((nq, LANES), NEG2, _F32)], axis=1)
    return pltpu.roll(base, 1, 1, stride=1, stride_axis=0)


def _norm_heads(q, seg, gqs):
    n = q.shape[0]
    nc = D_MODEL // LANES
    sq = jnp.concatenate([q[:, c * LANES:(c + 1) * LANES] for c in range(nc)], axis=0)
    ss = jnp.dot((sq * sq).astype(_BF), seg, preferred_element_type=_F32)
    r = lax.rsqrt(ss * (1.0 / HEAD_DIM) + EPS)
    return [q[:, c * LANES:(c + 1) * LANES] * r[c * n:(c + 1) * n, :] * gqs for c in range(nc)]


def _sink_softmax2(logits2, sink2):
    m = jnp.maximum(jnp.max(logits2, axis=-1, keepdims=True), sink2)
    e = jnp.exp2(logits2 - m)
    denom = jnp.sum(e, axis=-1, keepdims=True) + jnp.exp2(sink2 - m)
    return e * (1.0 / denom)


def _dot_t(a, b):
    return lax.dot_general(a, b, (((1,), (1,)), ((), ())), preferred_element_type=_F32)


def _attend_blocks(blocks, sinks_ref, seg, gqs):
    units = []
    for (q, kd, vd, bias_of) in blocks:
        nq = q.shape[0]
        qn = _norm_heads(q, seg, gqs)
        units += [(g, nq, bias_of, qn, kd[:, g * LANES:(g + 1) * LANES],
                   vd[:, g * LANES:(g + 1) * LANES]) for g in range(N_KV_HEADS)]

    def scores(unit):
        g, nq, _, qn, kg, _ = unit
        low = _low_lanes(nq)
        qs = jnp.concatenate(
            [jnp.where(low if hl % 2 == 0 else jnp.logical_not(low), qn[2 * g + hl // 2], 0.0)
             for hl in range(GROUP)], axis=0).astype(_BF)
        return _dot_t(qs, kg)

    def softmax(unit, s):
        g, nq, bias_of = unit[:3]
        return jnp.concatenate(
            [_sink_softmax2(s[hl * nq:(hl + 1) * nq, :] + bias_of(g * GROUP + hl),
                            sinks_ref[g * GROUP + hl] * LOG2E) for hl in range(GROUP)],
            axis=0).astype(_BF)

    def values(unit, p):
        nq, vg = unit[1], unit[5]
        low_k = _low_lanes(vg.shape[0])
        zero = jnp.zeros_like(vg)
        vlo, vhi = jnp.where(low_k, vg, zero), jnp.where(low_k, zero, vg)
        outs = []
        for pair in range(2):
            r0 = 2 * pair * nq
            outs.append(jnp.dot(p[r0:r0 + nq], vlo, preferred_element_type=_F32)
                        + jnp.dot(p[r0 + nq:r0 + 2 * nq], vhi, preferred_element_type=_F32))
        return outs

    n = len(units)
    s_next = scores(units[0])
    probs, outs = [None] * n, [None] * n
    for i in range(n):
        s_cur, s_next = s_next, (scores(units[i + 1]) if i + 1 < n else None)
        probs[i] = softmax(units[i], s_cur)
        if i >= 1:
            outs[i - 1] = values(units[i - 1], probs[i - 1])
    outs[n - 1] = values(units[n - 1], probs[n - 1])
    return [jnp.concatenate(sum(outs[b * N_KV_HEADS:(b + 1) * N_KV_HEADS], []), axis=1)
            for b in range(len(blocks))]


def _attn_prompt_kernel(q_ref, kp_ref, kc_ref, vp_ref, vc_ref, revb_ref, table_ref,
                        sinks_ref, gq_ref, seg_ref, o_ref, bias_ref):
    W = WINDOW
    i = pl.program_id(1)

    @pl.when(jnp.logical_and(pl.program_id(0) == 0, i == 0))
    def _():
        first_half = lax.broadcasted_iota(jnp.int32, (W, 2 * W), 1) < W
        for hd in range(N_HEADS):
            rows = _bias_rows(revb_ref, table_ref, hd, W)
            bias_ref[0, hd] = rows
            bias_ref[1, hd] = jnp.where(first_half, NEG2, rows)

    gqs = gq_ref[...] * (SCALE * LOG2E)
    kp, kc = kp_ref[...], kc_ref[...]
    vp, vc = vp_ref[...], vc_ref[...]
    first = jnp.where(i == 0, 1, 0)
    blocks = []
    for blk in range(ATT_BLOCKS):
        if blk == 0:
            kd = jnp.concatenate([kp, kc[0:W]], axis=0)
            vd = jnp.concatenate([vp, vc[0:W]], axis=0)
            bias_of = lambda hd: bias_ref[first, hd]
        else:
            kd, vd = kc[(blk - 1) * W:(blk + 1) * W], vc[(blk - 1) * W:(blk + 1) * W]
            bias_of = lambda hd: bias_ref[0, hd]
        blocks.append((q_ref[blk * W:(blk + 1) * W, :], kd, vd, bias_of))
    for blk, o in enumerate(_attend_blocks(blocks, sinks_ref, seg_ref[...], gqs)):
        o_ref[blk * W:(blk + 1) * W, :] = o.astype(o_ref.dtype)


def _attn_prompt(q2d, kd2d, vd2d, revb, table, sinks, gq2, seg):
    W = WINDOW
    R = ATT_BLOCKS * W
    ns = SEQ // R
    cur = lambda b, i: (b * ns + i, 0)
    prev = lambda b, i: (b * (SEQ // W) + jnp.maximum(ATT_BLOCKS * i - 1, 0), 0)
    smem = pl.BlockSpec(memory_space=pltpu.SMEM)
    return pl.pallas_call(
        _attn_prompt_kernel,
        grid=(BATCH, ns),
        in_specs=[pl.BlockSpec((R, D_MODEL), cur),
                  pl.BlockSpec((W, 2 * KD), prev), pl.BlockSpec((R, 2 * KD), cur),
                  pl.BlockSpec((W, 2 * KD), prev), pl.BlockSpec((R, 2 * KD), cur),
                  _const_spec((1, LANES)), smem, smem, _const_spec((1, LANES)),
                  _const_spec((LANES, LANES))],
        out_specs=pl.BlockSpec((R, D_MODEL), cur),
        out_shape=jax.ShapeDtypeStruct((BATCH * SEQ, D_MODEL), _BF),
        scratch_shapes=[pltpu.VMEM((2, N_HEADS, W, 2 * W), _F32)],
        compiler_params=_params(2),
        name="attn_prompt",
    )(q2d, kd2d, kd2d, vd2d, vd2d, revb, table, sinks, gq2, seg)


def _step_logits(q, kb, bias_all, gqs):
    n = DEC_SEQ
    low = _low_lanes(n)
    zero = jnp.zeros((n, LANES), _F32)
    rows = []
    for hd in range(N_HEADS):
        g = hd // GROUP
        x = q[:, (hd // 2) * LANES:(hd // 2 + 1) * LANES]
        if hd % 2 != g % 2:
            x = pltpu.roll(x, HEAD_DIM, 1)
        x = jnp.where(low if g % 2 == 0 else jnp.logical_not(low), x, 0.0)
        ss = jnp.sum(x * x, axis=-1, keepdims=True) * (1.0 / HEAD_DIM)
        x = x * lax.rsqrt(ss + EPS) * gqs
        rows.append(jnp.concatenate([x, zero] if g // 2 == 0 else [zero, x], axis=1))
    q_all = jnp.concatenate(rows, axis=0).astype(_BF)
    return _dot_t(q_all, kb) + bias_all


def _split_heads(o_all):
    n = DEC_SEQ
    low = _low_lanes(n)
    outs = []
    for c in range(N_HEADS // 2):
        halves = []
        for hf in range(2):
            hd = 2 * c + hf
            g = hd // GROUP
            y = o_all[hd * n:(hd + 1) * n, (g // 2) * LANES:(g // 2 + 1) * LANES]
            halves.append(y if g % 2 == hf else pltpu.roll(y, HEAD_DIM, 1))
        outs.append(jnp.where(low, halves[0], halves[1]))
    return jnp.concatenate(outs, axis=1)


def _attn_sample_kernel(q_ref, kn_ref, vn_ref, ck_ref, cv_ref, revb_ref, table_ref,
                        sinks_ref, gq_ref, o_ref, ko_ref, vo_ref, bias_ref, kbuf_ref, vbuf_ref):
    W = WINDOW
    L = 2 * W
    U = ATT_UNROLL

    @pl.when(pl.program_id(0) == 0)
    def _():
        for hd in range(N_HEADS):
            bias_ref[hd * DEC_SEQ:(hd + 1) * DEC_SEQ, :] = _bias_rows(revb_ref, table_ref, hd,
                                                                      DEC_SEQ)
        kbuf_ref[...] = jnp.zeros((U, L, KD), _F32)
        vbuf_ref[...] = jnp.zeros((U, L, KD), _F32)

    gqs = gq_ref[...] * (SCALE * LOG2E)
    bias_all = bias_ref[...]
    sink_col = jnp.concatenate(
        [jnp.full((DEC_SEQ, 1), sinks_ref[hd] * LOG2E, _F32) for hd in range(N_HEADS)], axis=0)

    def body(it, carry):
        logits = []
        for u in range(U):
            s = it * U + u
            kbuf_ref[u, 0:W, :] = ck_ref[s]
            vbuf_ref[u, 0:W, :] = cv_ref[s]
            kbuf_ref[u, W:W + DEC_SEQ, :] = kn_ref[s]
            vbuf_ref[u, W:W + DEC_SEQ, :] = vn_ref[s]
            ko_ref[s] = kbuf_ref[u, DEC_SEQ:DEC_SEQ + W, :]
            vo_ref[s] = vbuf_ref[u, DEC_SEQ:DEC_SEQ + W, :]
            logits.append(_step_logits(q_ref[s], kbuf_ref[u].astype(_BF), bias_all, gqs))
        probs = [_sink_softmax2(l, sink_col).astype(_BF) for l in logits]
        for u in range(U):
            o_all = jnp.dot(probs[u], vbuf_ref[u].astype(_BF), preferred_element_type=_F32)
            o_ref[it * U + u] = _split_heads(o_all)
        return carry

    lax.fori_loop(0, ATT_SEQS // U, body, 0)


def _attn_sample(q3d, kn3d, vn3d, ck, cv, revb, table, sinks, gq2):
    W = WINDOW
    A = ATT_SEQS
    blk = lambda r, c: pl.BlockSpec((A, r, c), lambda i: (i, 0, 0))
    smem = pl.BlockSpec(memory_space=pltpu.SMEM)
    return pl.pallas_call(
        _attn_sample_kernel,
        grid=(DEC_BATCH // A,),
        in_specs=[blk(DEC_SEQ, D_MODEL), blk(DEC_SEQ, KD), blk(DEC_SEQ, KD),
                  blk(W, KD), blk(W, KD),
                  _const_spec((1, LANES)), smem, smem, _const_spec((1, LANES))],
        out_specs=[blk(DEC_SEQ, D_MODEL), blk(W, KD), blk(W, KD)],
        out_shape=[jax.ShapeDtypeStruct((DEC_BATCH, DEC_SEQ, D_MODEL), _F32),
                   jax.ShapeDtypeStruct((DEC_BATCH, W, KD), _F32),
                   jax.ShapeDtypeStruct((DEC_BATCH, W, KD), _F32)],
        scratch_shapes=[pltpu.VMEM((N_HEADS * DEC_SEQ, 2 * W), _F32),
                        pltpu.VMEM((ATT_UNROLL, 2 * W, KD), _F32),
                        pltpu.VMEM((ATT_UNROLL, 2 * W, KD), _F32)],
        compiler_params=_params(1),
        name="attn_sample",
    )(q3d, kn3d, vn3d, ck, cv, revb, table, sinks, gq2)


def kernel(x_prompt, x_sample, state_conv, state_ffn, cache_k, cache_v, p_prompt, p_sample, g_mix, cm_w_pw1, cm_b_pw1, cm_w_dw, cm_b_dw, cm_ln_g, cm_ln_b, cm_w_pw2, cm_b_pw2, at_w_q, at_g_q, at_sinks, at_w_o, kv_g, kv_w_k, kv_w_v, kv_g_k, rel_bias, g_ffn, ffn_w_up, ffn_w_dw, ffn_b_dw, ffn_w_down, g_ple, ple_w_gate, ple_w_proj):
    D, F = D_MODEL, D_FF
    row = lambda a: a.reshape(1, -1)

    mixer_w = [row(g_mix[0]), cm_w_pw1[0], row(cm_b_pw1[0]), cm_w_dw[0], row(cm_b_dw[0]),
               row(cm_ln_g[0]), row(cm_ln_b[0]), cm_w_pw2[0], row(cm_b_pw2[0])]
    g1, conv_s_tm, w_pw1_bf, w_pw2_bf = _mixer_sample(
        x_sample.reshape(DEC_BATCH * DEC_SEQ, D), jnp.transpose(state_conv[0], (1, 0, 2)), mixer_w)
    conv_s = jnp.transpose(conv_s_tm, (1, 0, 2))
    mixer_w[1], mixer_w[7] = w_pw1_bf, w_pw2_bf
    cast_srcs = [ffn_w_up.reshape(2 * D, 2 * F), ffn_w_down.reshape(2 * F, D),
                 ple_w_gate.reshape(2 * D, D), ple_w_proj.reshape(2 * PLE_DIM, D),
                 at_w_q[0], at_w_o[0], kv_w_k, kv_w_v]

    xp = x_prompt.reshape(BATCH * SEQ, D)
    pp = p_prompt.reshape(2, BATCH * SEQ, PLE_DIM)
    (h1, conv_p, w_up_bf, w_down_bf, w_gate_bf, w_proj_bf, w_q_bf, w_o_bf, w_k_bf,
     w_v_bf) = _mixer_prompt(xp, mixer_w, cast_srcs)

    stacked = dict(
        g_ffn=g_ffn[:, None, :], w_up=w_up_bf.reshape(2, D, 2 * F), w_dw=ffn_w_dw,
        b_dw=ffn_b_dw[:, None, :], w_down=w_down_bf.reshape(2, F, D), g_ple=g_ple[:, None, :],
        w_gate=w_gate_bf.reshape(2, D, D), w_proj=w_proj_bf.reshape(2, PLE_DIM, D))
    ffn_w = [dict(stacked), dict(stacked)]
    ffn_w[0].update(g_kv=row(kv_g), w_k=w_k_bf, w_v=w_v_bf,
                    g_k2=row(jnp.tile(kv_g_k, 2)), g_q_in=row(g_mix[1]), w_q=w_q_bf)
    ffn_w[1].update(wo=w_o_bf)
    gq2 = row(jnp.tile(at_g_q[0], 2))
    sinks = at_sinks[0]

    W = WINDOW
    revb = jnp.asarray(_t5_buckets(W - 1 - np.arange(W))[None, :])
    half = np.arange(LANES) // HEAD_DIM
    seg = jnp.asarray(half[:, None] == half[None, :], dtype=_BF)

    h3, ffn_p0, k_p, v_p, q_p, kd_p, vd_p = _ffn_call(
        h1, pp, None, None, ffn_w[0], layer=0, prompt=True, with_oproj=False, with_kvq=True)
    o_p = _attn_prompt(q_p, kd_p, vd_p, revb, rel_bias, sinks, gq2, seg)
    y_p, ffn_p = _ffn_call(h3, pp, None, o_p, ffn_w[1], layer=1, prompt=True,
                           with_oproj=True, with_kvq=False, prev_state=ffn_p0)

    ps = p_sample.reshape(2, DEC_BATCH * DEC_SEQ, PLE_DIM)
    g3, ffn_s0, k_s, v_s, q_s = _ffn_call(g1, ps, state_ffn, None, ffn_w[0], layer=0,
                                          prompt=False, with_oproj=False, with_kvq=True)
    o_s, k_out, v_out = _attn_sample(
        q_s.reshape(DEC_BATCH, DEC_SEQ, D), k_s.reshape(DEC_BATCH, DEC_SEQ, KD),
        v_s.reshape(DEC_BATCH, DEC_SEQ, KD), cache_k.reshape(DEC_BATCH, W, KD),
        cache_v.reshape(DEC_BATCH, W, KD), revb, rel_bias, sinks, gq2)
    y_s, ffn_s = _ffn_call(g3, ps, state_ffn, o_s.reshape(DEC_BATCH * DEC_SEQ, D),
                           ffn_w[1], layer=1, prompt=False, with_oproj=True, with_kvq=False,
                           prev_state=ffn_s0)

    kv_shape_p = (BATCH, W, N_KV_HEADS, HEAD_DIM)
    kv_shape_s = (DEC_BATCH, W, N_KV_HEADS, HEAD_DIM)
    return (y_p.reshape(BATCH, SEQ, D), y_s.reshape(DEC_BATCH, DEC_SEQ, D),
            conv_p[None], conv_s[None],
            ffn_p, ffn_s,
            k_p.reshape(BATCH, SEQ, KD)[:, SEQ - W:].reshape(kv_shape_p),
            k_out.reshape(kv_shape_s),
            v_p.reshape(BATCH, SEQ, KD)[:, SEQ - W:].reshape(kv_shape_p),
            v_out.reshape(kv_shape_s))
```

```python
import functools
import math

import jax
import jax.numpy as jnp
import numpy as np
from jax import lax
from jax.experimental import pallas as pl
from jax.experimental.pallas import tpu as pltpu

D_MODEL = 1024
BATCH = 8
SEQ = 2048
DEC_BATCH = 128
DEC_SEQ = 8
CONV_K = 31
FFN_CONV_K = 3
D_FF = 2816
N_HEADS = 16
N_KV_HEADS = 4
HEAD_DIM = 64
GROUP = N_HEADS // N_KV_HEADS
WINDOW = 128
N_BUCKETS = 32
MAX_DISTANCE = 128
PLE_DIM = 256
EPS = 1e-6
SCALE = HEAD_DIM ** -0.5
NEG = -1e30
KD = N_KV_HEADS * HEAD_DIM

LANES = 128
SUBLANES = 8
V7X_VMEM_BYTES = 64 * 1024 * 1024
VMEM_LIMIT = V7X_VMEM_BYTES - 4 * 1024 * 1024

T_TILE = 512
MIX_TILE = 1024
S_TILE = 32
FFN_S_TILE = 32
CONV_HALO = 32
FFN_HALO = 8
F_CHUNK = 2816
D_SLABS = D_MODEL // LANES
F_SLABS = D_FF // LANES
ATT_SEQS = 16
ATT_UNROLL = 8
ATT_BLOCKS = 4

_BF = jnp.bfloat16
_F32 = jnp.float32


def _bdot(a, w):
    return jnp.dot(a.astype(_BF), w, preferred_element_type=_F32)


def _rms(x, g):
    return x * lax.rsqrt(jnp.mean(x * x, axis=-1, keepdims=True) + EPS) * g


HALF = 0.5


def _one_plus_tanh(half_x):
    return 1.0 + jnp.tanh(half_x)


def _const_spec(shape):
    nd = len(shape)
    return pl.BlockSpec(shape, lambda *_: (0,) * nd, pipeline_mode=pl.Buffered(1))


def _layer_spec(shape, layer):
    nd = len(shape)
    return pl.BlockSpec((None,) + tuple(shape), lambda *_: (layer,) + (0,) * nd,
                        pipeline_mode=pl.Buffered(1))


def _params(n_grid):
    return pltpu.CompilerParams(dimension_semantics=("arbitrary",) * n_grid,
                                vmem_limit_bytes=VMEM_LIMIT)


def _glu_slab(u, b1_ref, c):
    ba = b1_ref[:, c * LANES:(c + 1) * LANES]
    bg = b1_ref[:, D_MODEL + c * LANES:D_MODEL + (c + 1) * LANES]
    return (u[:, :LANES] + HALF * ba) * _one_plus_tanh(u[:, LANES:] + HALF * bg)


def _mixer_tail(x, c, bdw, lng, lnb, w2, b2):
    c = c + bdw
    mu = jnp.mean(c, axis=-1, keepdims=True)
    cz = c - mu
    var = jnp.mean(cz * cz, axis=-1, keepdims=True)
    hy = cz * lax.rsqrt(var + EPS) * (HALF * lng) + HALF * lnb
    s = hy * _one_plus_tanh(hy)
    return x + _bdot(s, w2) + b2


def _mixer_prompt_kernel(x_ref, g_ref, w1_ref, b1_ref, wdw_ref, bdw_ref, lng_ref, lnb_ref,
                         w2_ref, b2_ref, *rest):
    n_cast = len(_CAST_ROWS)
    cast_in, rest = rest[:n_cast], rest[n_cast:]
    h_ref, cs_ref = rest[0], rest[1]
    cast_out, (slab_ref, oslab_ref) = rest[2:2 + n_cast], rest[2 + n_cast:]
    T = MIX_TILE
    H = T // 2
    j = pl.program_id(1)

    for src, dst, scale in zip(cast_in, cast_out, _CAST_SCALE):
        dst[...] = (src[...] if scale == 1.0 else scale * src[...]).astype(_BF)

    @pl.when(j == 0)
    def _():
        slab_ref[:, 0:CONV_HALO, :] = jnp.zeros((D_SLABS, CONV_HALO, LANES), _F32)

    x = x_ref[...]
    xn = _rms(x, g_ref[...]).astype(_BF)
    base = CONV_HALO - (CONV_K - 1)
    tails = []
    for c in range(D_SLABS):
        u = jnp.dot(xn, w1_ref[:, 2 * c * LANES:2 * (c + 1) * LANES],
                    preferred_element_type=_F32)
        glu = _glu_slab(u, b1_ref, c)
        slab_ref[c, CONV_HALO:CONV_HALO + T, :] = glu
        tails.append(glu[T - (CONV_K - 1):, :])
        wc = wdw_ref[:, c * LANES:(c + 1) * LANES]
        for p in range(2):
            acc = slab_ref[c, pl.ds(base + p, H, stride=2), :] * wc[0:1, :]
            for k in range(1, CONV_K):
                acc = acc + slab_ref[c, pl.ds(base + k + p, H, stride=2), :] * wc[k:k + 1, :]
            oslab_ref[c, pl.ds(p, H, stride=2), :] = acc
    cs_ref[...] = jnp.concatenate(tails, axis=1)
    conv = jnp.concatenate([oslab_ref[c] for c in range(D_SLABS)], axis=1)
    h_ref[...] = _mixer_tail(x, conv, bdw_ref[...], lng_ref[...], lnb_ref[...],
                             w2_ref[...], b2_ref[...])
    for c in range(D_SLABS):
        slab_ref[c, 0:CONV_HALO, :] = slab_ref[c, T:T + CONV_HALO, :]


def _mixer_sample_kernel(x_ref, st_ref, g_ref, w1f_ref, b1_ref, wdw_ref, bdw_ref, lng_ref,
                         lnb_ref, w2f_ref, b2_ref, h_ref, cs_ref, w1_ref, w2_ref):
    S = S_TILE
    KH = CONV_K - 1

    @pl.when(pl.program_id(0) == 0)
    def _():
        for c in range(D_SLABS):
            for part in range(2):
                src = w1f_ref[:, part * D_MODEL + c * LANES:part * D_MODEL + (c + 1) * LANES]
                w1_ref[:, (2 * c + part) * LANES:(2 * c + part + 1) * LANES] = (
                    HALF * src).astype(_BF)
        w2_ref[...] = w2f_ref[...].astype(_BF)

    x = x_ref[...]
    u = _bdot(_rms(x, g_ref[...]), w1_ref[...])
    glu = jnp.concatenate(
        [_glu_slab(u[:, 2 * c * LANES:2 * (c + 1) * LANES], b1_ref, c) for c in range(D_SLABS)],
        axis=1)
    glu_tm = jnp.swapaxes(glu.reshape(S, DEC_SEQ, D_MODEL), 0, 1)
    cs_ref[0:KH - DEC_SEQ] = st_ref[DEC_SEQ:KH]
    cs_ref[KH - DEC_SEQ:KH] = glu_tm

    def plane(j):
        return st_ref[j] if j < KH else glu_tm[j - KH]

    outs = []
    for t in range(DEC_SEQ):
        acc = plane(t) * wdw_ref[0:1, :]
        for k in range(1, CONV_K):
            acc = acc + plane(t + k) * wdw_ref[k:k + 1, :]
        outs.append(acc)
    conv = jnp.swapaxes(jnp.stack(outs, axis=0), 0, 1).reshape(S * DEC_SEQ, D_MODEL)
    h_ref[...] = _mixer_tail(x, conv, bdw_ref[...], lng_ref[...], lnb_ref[...],
                             w2_ref[...], b2_ref[...])


def _mixer_weight_specs():
    D = D_MODEL
    return [_const_spec((1, D)), _const_spec((D, 2 * D)), _const_spec((1, 2 * D)),
            _const_spec((CONV_K, D)), _const_spec((1, D)), _const_spec((1, D)),
            _const_spec((1, D)), _const_spec((D, D)), _const_spec((1, D))]


_CAST_ROWS = (2 * D_MODEL, 2 * D_FF, 2 * D_MODEL, 2 * PLE_DIM) + (D_MODEL,) * 4
_CAST_SCALE = (1.0, 1.0, HALF, HALF) + (1.0,) * 4


def _mixer_prompt(x2d, wts, cast_srcs):
    T = MIX_TILE
    nt = SEQ // T
    steps = BATCH * nt
    row = lambda b, j: (b * nt + j, 0)
    cast_specs, cast_shapes = [], []
    for src, rows in zip(cast_srcs, _CAST_ROWS):
        assert src.shape[0] == rows and rows % steps == 0
        cast_specs.append(pl.BlockSpec((rows // steps, src.shape[1]), row))
        cast_shapes.append(jax.ShapeDtypeStruct(src.shape, _BF))
    return pl.pallas_call(
        _mixer_prompt_kernel,
        grid=(BATCH, nt),
        in_specs=[pl.BlockSpec((T, D_MODEL), row)] + _mixer_weight_specs() + cast_specs,
        out_specs=[pl.BlockSpec((T, D_MODEL), row),
                   pl.BlockSpec((None, CONV_K - 1, D_MODEL), lambda b, j: (b, 0, 0))] + cast_specs,
        out_shape=[jax.ShapeDtypeStruct((BATCH * SEQ, D_MODEL), _F32),
                   jax.ShapeDtypeStruct((BATCH, CONV_K - 1, D_MODEL), _F32)] + cast_shapes,
        scratch_shapes=[pltpu.VMEM((D_SLABS, CONV_HALO + T, LANES), _F32),
                        pltpu.VMEM((D_SLABS, T, LANES), _F32)],
        compiler_params=_params(2),
        name="mixer_prompt",
    )(x2d, *wts, *cast_srcs)


def _mixer_sample(x2d, state_tm, wts):
    S = S_TILE
    R = S * DEC_SEQ
    rows = pl.BlockSpec((R, D_MODEL), lambda i: (i, 0))
    planes = pl.BlockSpec((CONV_K - 1, S, D_MODEL), lambda i: (0, i, 0))
    w1_shape, w2_shape = (D_MODEL, 2 * D_MODEL), (D_MODEL, D_MODEL)
    whole = lambda shape: pl.BlockSpec(shape, lambda i: (0, 0))
    return pl.pallas_call(
        _mixer_sample_kernel,
        grid=(DEC_BATCH // S,),
        in_specs=[rows, planes] + _mixer_weight_specs(),
        out_specs=[rows, planes, whole(w1_shape), whole(w2_shape)],
        out_shape=[jax.ShapeDtypeStruct((DEC_BATCH * DEC_SEQ, D_MODEL), _F32),
                   jax.ShapeDtypeStruct((CONV_K - 1, DEC_BATCH, D_MODEL), _F32),
                   jax.ShapeDtypeStruct(w1_shape, _BF), jax.ShapeDtypeStruct(w2_shape, _BF)],
        compiler_params=_params(1),
        name="mixer_sample",
    )(x2d, state_tm, *wts)


def _ffn_kernel(*refs, prompt, with_oproj, with_kvq):
    it = iter(refs)
    h_ref = next(it)
    p_ref = next(it)
    st_ref = None if prompt else next(it)
    if with_oproj:
        o_ref = next(it)
        wo_ref = next(it)
        fs0_ref = next(it)
    gf_ref, wup_ref, wdw_ref, bdw_ref, wdn_ref = (next(it) for _ in range(5))
    gp_ref, wg_ref, wp_ref = (next(it) for _ in range(3))
    if with_kvq:
        gkv_ref, wk_ref, wv_ref, gk_ref, gq_ref, wq_ref = (next(it) for _ in range(6))
    hout_ref = next(it)
    fs_ref = next(it)
    dup_kv = with_kvq and prompt
    if with_kvq:
        k_ref, v_ref, q_ref = (next(it) for _ in range(3))
    if dup_kv:
        kd_ref, vd_ref = next(it), next(it)
    if prompt:
        slab_ref, oslab_ref = next(it), next(it)

    F = D_FF
    R = T_TILE if prompt else FFN_S_TILE * DEC_SEQ
    H = R // 2
    KH = FFN_CONV_K - 1
    assert F_CHUNK == F or prompt

    if prompt:
        @pl.when(pl.program_id(1) == 0)
        def _():
            slab_ref[:, 0:FFN_HALO, :] = jnp.zeros((2 * F_SLABS, FFN_HALO, LANES), _F32)

    h = h_ref[...]
    if with_oproj:
        h = h + _bdot(o_ref[...], wo_ref[...])
    xn = _rms(h, gf_ref[...]).astype(_BF)

    n_chunks = F // F_CHUNK
    spc = F_CHUNK // LANES
    acc = None
    tails = []
    for ci in range(n_chunks):
        halves = []
        for half in range(2):
            c0 = half * F + ci * F_CHUNK
            uc = jnp.dot(xn, wup_ref[:, c0:c0 + F_CHUNK], preferred_element_type=_F32)
            w = wdw_ref[:, c0:c0 + F_CHUNK]
            b = bdw_ref[:, c0:c0 + F_CHUNK]
            if half == 0:
                w, b = HALF * w, HALF * b
            if prompt:
                cols = []
                for s in range(spc):
                    sl = half * F_SLABS + ci * spc + s
                    slab_ref[sl, FFN_HALO:FFN_HALO + R, :] = uc[:, s * LANES:(s + 1) * LANES]
                    ws = w[:, s * LANES:(s + 1) * LANES]
                    ph = []
                    for p in range(2):
                        cv = b[:, s * LANES:(s + 1) * LANES]
                        for k in range(FFN_CONV_K):
                            cv = cv + (slab_ref[sl, pl.ds(FFN_HALO - KH + k + p, H, stride=2), :]
                                       * ws[k:k + 1, :])
                        ph.append(cv)
                    cols.append(jnp.concatenate(ph, axis=0))
                halves.append(jnp.concatenate(cols, axis=1))
            else:
                u3 = uc.reshape(FFN_S_TILE, DEC_SEQ, F_CHUNK)
                hist = st_ref[:, :, c0:c0 + F_CHUNK]
                h0 = jnp.broadcast_to(hist[:, 0:1, :], u3.shape)
                h1 = jnp.broadcast_to(hist[:, 1:2, :], u3.shape)
                step = lax.broadcasted_iota(jnp.int32, u3.shape, 1)
                um1 = jnp.where(step == 0, h1, pltpu.roll(u3, 1, 1))
                um2 = jnp.where(step == 0, h0, jnp.where(step == 1, h1, pltpu.roll(u3, 2, 1)))
                cv = b + um2 * w[0:1, :] + um1 * w[1:2, :] + u3 * w[2:3, :]
                tails.append(u3[:, DEC_SEQ - KH:, :])
                halves.append(cv.reshape(R, F_CHUNK))
        half_gate, val = halves
        act = (half_gate * _one_plus_tanh(half_gate) * val).astype(_BF)
        part = jnp.dot(act, wdn_ref[ci * F_CHUNK:(ci + 1) * F_CHUNK, :],
                       preferred_element_type=_F32)
        acc = part if acc is None else acc + part

    if prompt:
        for c in range(D_SLABS):
            oslab_ref[c, pl.ds(0, H, stride=2), :] = acc[:H, c * LANES:(c + 1) * LANES]
            oslab_ref[c, pl.ds(1, H, stride=2), :] = acc[H:, c * LANES:(c + 1) * LANES]
        f = jnp.concatenate([oslab_ref[c] for c in range(D_SLABS)], axis=1)
        new_state = jnp.concatenate(
            [slab_ref[s, FFN_HALO + R - SUBLANES:FFN_HALO + R, :] for s in range(2 * F_SLABS)],
            axis=1)[SUBLANES - KH:, :]
        for s in range(2 * F_SLABS):
            slab_ref[s, 0:FFN_HALO, :] = slab_ref[s, R:R + FFN_HALO, :]
    else:
        f = acc
        new_state = jnp.concatenate(tails, axis=2)
    if with_oproj:
        fs_ref[0] = fs0_ref[...]
        fs_ref[1] = new_state
    else:
        fs_ref[...] = new_state

    h2 = h + f
    h3 = h2 + (_one_plus_tanh(_bdot(_rms(h2, gp_ref[...]), wg_ref[...]))
               * _bdot(p_ref[...], wp_ref[...]))
    hout_ref[...] = h3

    if with_kvq:
        hk = _rms(h3, gkv_ref[...]).astype(_BF)
        kk = jnp.dot(hk, wk_ref[...], preferred_element_type=_F32)
        vv = jnp.dot(hk, wv_ref[...], preferred_element_type=_F32)
        v_ref[...] = vv
        low = lax.broadcasted_iota(jnp.int32, (R, LANES), 1) < HEAD_DIM

        def dup_heads(x):
            out = []
            for c in range(KD // LANES):
                xc = x[:, c * LANES:(c + 1) * LANES]
                xr = pltpu.roll(xc, HEAD_DIM, 1)
                out += [jnp.where(low, xc, xr), jnp.where(low, xr, xc)]
            return out

        kn = []
        for kg in dup_heads(kk):
            r = lax.rsqrt(jnp.mean(kg * kg, axis=-1, keepdims=True) + EPS)
            kn.append(kg * r * gk_ref[...])
        k_ref[...] = jnp.concatenate(
            [jnp.where(low, kn[2 * c], kn[2 * c + 1]) for c in range(KD // LANES)], axis=1)
        if dup_kv:
            kd_ref[...] = jnp.concatenate(kn, axis=1).astype(_BF)
            vd_ref[...] = jnp.concatenate(dup_heads(vv), axis=1).astype(_BF)
        q_ref[...] = _bdot(_rms(h3, gq_ref[...]), wq_ref[...])


def _ffn_call(h2d, p3d, state4d, o2d, w, *, layer, prompt, with_oproj, with_kvq,
              prev_state=None):
    D, F = D_MODEL, D_FF
    if prompt:
        R = T_TILE
        nt = SEQ // R
        grid = (BATCH, nt)
        row = lambda b, j: (b * nt + j, 0)
        p_spec = pl.BlockSpec((None, R, PLE_DIM), lambda b, j: (layer, b * nt + j, 0))
        n_rows, n_seq = BATCH * SEQ, BATCH
        fs_spec = pl.BlockSpec((None, FFN_CONV_K - 1, 2 * F), lambda b, j: (b, 0, 0))
        scratch = [pltpu.VMEM((2 * F_SLABS, FFN_HALO + R, LANES), _F32),
                   pltpu.VMEM((D_SLABS, R, LANES), _F32)]
    else:
        R = FFN_S_TILE * DEC_SEQ
        grid = (DEC_BATCH // FFN_S_TILE,)
        row = lambda i: (i, 0)
        p_spec = pl.BlockSpec((None, R, PLE_DIM), lambda i: (layer, i, 0))
        n_rows, n_seq = DEC_BATCH * DEC_SEQ, DEC_BATCH
        fs_spec = pl.BlockSpec((FFN_S_TILE, FFN_CONV_K - 1, 2 * F), lambda i: (i, 0, 0))
        scratch = []

    args = [h2d, p3d]
    in_specs = [pl.BlockSpec((R, D), row), p_spec]
    if not prompt:
        args.append(state4d)
        in_specs.append(pl.BlockSpec((None, FFN_S_TILE, FFN_CONV_K - 1, 2 * F),
                                     lambda i: (layer, i, 0, 0)))
    fs_shape = (n_seq, FFN_CONV_K - 1, 2 * F)
    if with_oproj:
        args += [o2d, w["wo"], prev_state]
        in_specs += [pl.BlockSpec((R, D), row), _const_spec((D, D)), fs_spec]
        blk, imap = fs_spec.block_shape, fs_spec.index_map
        fs_spec = pl.BlockSpec((2,) + tuple(blk), lambda *g: (0,) + tuple(imap(*g)))
        fs_shape = (2,) + fs_shape
    args += [w["g_ffn"], w["w_up"], w["w_dw"], w["b_dw"], w["w_down"],
             w["g_ple"], w["w_gate"], w["w_proj"]]
    in_specs += [_layer_spec((1, D), layer), _layer_spec((D, 2 * F), layer),
                 _layer_spec((FFN_CONV_K, 2 * F), layer), _layer_spec((1, 2 * F), layer),
                 _layer_spec((F, D), layer), _layer_spec((1, D), layer),
                 _layer_spec((D, D), layer), _layer_spec((PLE_DIM, D), layer)]
    out_specs = [pl.BlockSpec((R, D), row), fs_spec]
    out_shape = [jax.ShapeDtypeStruct((n_rows, D), _F32),
                 jax.ShapeDtypeStruct(fs_shape, _F32)]
    if with_kvq:
        args += [w["g_kv"], w["w_k"], w["w_v"], w["g_k2"], w["g_q_in"], w["w_q"]]
        in_specs += [_const_spec((1, D)), _const_spec((D, KD)), _const_spec((D, KD)),
                     _const_spec((1, LANES)), _const_spec((1, D)), _const_spec((D, D))]
        out_specs += [pl.BlockSpec((R, KD), row), pl.BlockSpec((R, KD), row),
                      pl.BlockSpec((R, D), row)]
        out_shape += [jax.ShapeDtypeStruct((n_rows, KD), _F32),
                      jax.ShapeDtypeStruct((n_rows, KD), _F32),
                      jax.ShapeDtypeStruct((n_rows, D), _F32)]
        if prompt:
            out_specs += [pl.BlockSpec((R, 2 * KD), row), pl.BlockSpec((R, 2 * KD), row)]
            out_shape += [jax.ShapeDtypeStruct((n_rows, 2 * KD), _BF),
                          jax.ShapeDtypeStruct((n_rows, 2 * KD), _BF)]
    kern = functools.partial(_ffn_kernel, prompt=prompt, with_oproj=with_oproj,
                             with_kvq=with_kvq)
    name = "ffn_%s_%d" % ("prompt" if prompt else "sample", layer)
    return pl.pallas_call(
        kern, grid=grid, in_specs=in_specs, out_specs=out_specs, out_shape=out_shape,
        scratch_shapes=scratch, compiler_params=_params(len(grid)), name=name,
    )(*args)


def _t5_buckets(dist):
    max_exact = N_BUCKETS // 2
    d = np.maximum(dist, 0)
    df = np.maximum(d, 1).astype(np.float32)
    large = max_exact + (np.log(df / np.float32(max_exact))
                         / np.float32(math.log(MAX_DISTANCE / max_exact))
                         * np.float32(N_BUCKETS - max_exact)).astype(np.int32)
    large = np.minimum(large, N_BUCKETS - 1)
    return np.where(d < max_exact, d, large).astype(np.int32)


LOG2E = math.log2(math.e)
NEG2 = NEG * LOG2E


def _low_lanes(rows):
    return lax.broadcasted_iota(jnp.int32, (rows, LANES), 1) < HEAD_DIM


def _bias_rows(revb_ref, table_ref, hd, nq):
    revb = revb_ref[...]
    row = jnp.zeros(revb.shape, _F32)
    for bk in range(N_BUCKETS):
        row = jnp.where(revb == bk, table_ref[bk, hd] * LOG2E, row)
    base = jnp.concatenate([jnp.broadcast_to(row, (nq, LANES)),
                            jnp.full((nq, LANES), NEG2, _F32)], axis=1)
    return pltpu.roll(base, 1, 1, stride=1, stride_axis=0)


def _norm_heads(q, seg, gqs):
    n = q.shape[0]
    nc = D_MODEL // LANES
    sq = jnp.concatenate([q[:, c * LANES:(c + 1) * LANES] for c in range(nc)], axis=0)
    ss = jnp.dot((sq * sq).astype(_BF), seg, preferred_element_type=_F32)
    r = lax.rsqrt(ss * (1.0 / HEAD_DIM) + EPS)
    return [q[:, c * LANES:(c + 1) * LANES] * r[c * n:(c + 1) * n, :] * gqs for c in range(nc)]


def _sink_softmax2(logits2, sink2):
    m = jnp.maximum(jnp.max(logits2, axis=-1, keepdims=True), sink2)
    e = jnp.exp2(logits2 - m)
    denom = jnp.sum(e, axis=-1, keepdims=True) + jnp.exp2(sink2 - m)
    return e * (1.0 / denom)


def _dot_t(a, b):
    return lax.dot_general(a, b, (((1,), (1,)), ((), ())), preferred_element_type=_F32)


def _attend_blocks(blocks, sinks_ref, seg, gqs):
    units = []
    for (q, kd, vd, bias_of) in blocks:
        nq = q.shape[0]
        qn = _norm_heads(q, seg, gqs)
        units += [(g, nq, bias_of, qn, kd[:, g * LANES:(g + 1) * LANES],
                   vd[:, g * LANES:(g + 1) * LANES]) for g in range(N_KV_HEADS)]

    def scores(unit):
        g, nq, _, qn, kg, _ = unit
        low = _low_lanes(nq)
        qs = jnp.concatenate(
            [jnp.where(low if hl % 2 == 0 else jnp.logical_not(low), qn[2 * g + hl // 2], 0.0)
             for hl in range(GROUP)], axis=0).astype(_BF)
        return _dot_t(qs, kg)

    def softmax(unit, s):
        g, nq, bias_of = unit[:3]
        return jnp.concatenate(
            [_sink_softmax2(s[hl * nq:(hl + 1) * nq, :] + bias_of(g * GROUP + hl),
                            sinks_ref[g * GROUP + hl] * LOG2E) for hl in range(GROUP)],
            axis=0).astype(_BF)

    def values(unit, p):
        nq, vg = unit[1], unit[5]
        low_k = _low_lanes(vg.shape[0])
        zero = jnp.zeros_like(vg)
        vlo, vhi = jnp.where(low_k, vg, zero), jnp.where(low_k, zero, vg)
        outs = []
        for pair in range(2):
            r0 = 2 * pair * nq
            outs.append(jnp.dot(p[r0:r0 + nq], vlo, preferred_element_type=_F32)
                        + jnp.dot(p[r0 + nq:r0 + 2 * nq], vhi, preferred_element_type=_F32))
        return outs

    n = len(units)
    s_next = scores(units[0])
    probs, outs = [None] * n, [None] * n
    for i in range(n):
        s_cur, s_next = s_next, (scores(units[i + 1]) if i + 1 < n else None)
        probs[i] = softmax(units[i], s_cur)
        if i >= 1:
            outs[i - 1] = values(units[i - 1], probs[i - 1])
    outs[n - 1] = values(units[n - 1], probs[n - 1])
    return [jnp.concatenate(sum(outs[b * N_KV_HEADS:(b + 1) * N_KV_HEADS], []), axis=1)
            for b in range(len(blocks))]


def _attn_prompt_kernel(q_ref, kp_ref, kc_ref, vp_ref, vc_ref, revb_ref, table_ref,
                        sinks_ref, gq_ref, seg_ref, o_ref, bias_ref):
    W = WINDOW
    i = pl.program_id(1)

    @pl.when(jnp.logical_and(pl.program_id(0) == 0, i == 0))
    def _():
        first_half = lax.broadcasted_iota(jnp.int32, (W, 2 * W), 1) < W
        for hd in range(N_HEADS):
            rows = _bias_rows(revb_ref, table_ref, hd, W)
            bias_ref[0, hd] = rows
            bias_ref[1, hd] = jnp.where(first_half, NEG2, rows)

    gqs = gq_ref[...] * (SCALE * LOG2E)
    kp, kc = kp_ref[...], kc_ref[...]
    vp, vc = vp_ref[...], vc_ref[...]
    first = jnp.where(i == 0, 1, 0)
    blocks = []
    for blk in range(ATT_BLOCKS):
        if blk == 0:
            kd = jnp.concatenate([kp, kc[0:W]], axis=0)
            vd = jnp.concatenate([vp, vc[0:W]], axis=0)
            bias_of = lambda hd: bias_ref[first, hd]
        else:
            kd, vd = kc[(blk - 1) * W:(blk + 1) * W], vc[(blk - 1) * W:(blk + 1) * W]
            bias_of = lambda hd: bias_ref[0, hd]
        blocks.append((q_ref[blk * W:(blk + 1) * W, :], kd, vd, bias_of))
    for blk, o in enumerate(_attend_blocks(blocks, sinks_ref, seg_ref[...], gqs)):
        o_ref[blk * W:(blk + 1) * W, :] = o.astype(o_ref.dtype)


def _attn_prompt(q2d, kd2d, vd2d, revb, table, sinks, gq2, seg):
    W = WINDOW
    R = ATT_BLOCKS * W
    ns = SEQ // R
    cur = lambda b, i: (b * ns + i, 0)
    prev = lambda b, i: (b * (SEQ // W) + jnp.maximum(ATT_BLOCKS * i - 1, 0), 0)
    smem = pl.BlockSpec(memory_space=pltpu.SMEM)
    return pl.pallas_call(
        _attn_prompt_kernel,
        grid=(BATCH, ns),
        in_specs=[pl.BlockSpec((R, D_MODEL), cur),
                  pl.BlockSpec((W, 2 * KD), prev), pl.BlockSpec((R, 2 * KD), cur),
                  pl.BlockSpec((W, 2 * KD), prev), pl.BlockSpec((R, 2 * KD), cur),
                  _const_spec((1, LANES)), smem, smem, _const_spec((1, LANES)),
                  _const_spec((LANES, LANES))],
        out_specs=pl.BlockSpec((R, D_MODEL), cur),
        out_shape=jax.ShapeDtypeStruct((BATCH * SEQ, D_MODEL), _BF),
        scratch_shapes=[pltpu.VMEM((2, N_HEADS, W, 2 * W), _F32)],
        compiler_params=_params(2),
        name="attn_prompt",
    )(q2d, kd2d, kd2d, vd2d, vd2d, revb, table, sinks, gq2, seg)


def _step_logits(q, kb, bias_all, gqs):
    n = DEC_SEQ
    low = _low_lanes(n)
    zero = jnp.zeros((n, LANES), _F32)
    rows = []
    for hd in range(N_HEADS):
        g = hd // GROUP
        x = q[:, (hd // 2) * LANES:(hd // 2 + 1) * LANES]
        if hd % 2 != g % 2:
            x = pltpu.roll(x, HEAD_DIM, 1)
        x = jnp.where(low if g % 2 == 0 else jnp.logical_not(low), x, 0.0)
        ss = jnp.sum(x * x, axis=-1, keepdims=True) * (1.0 / HEAD_DIM)
        x = x * lax.rsqrt(ss + EPS) * gqs
        rows.append(jnp.concatenate([x, zero] if g // 2 == 0 else [zero, x], axis=1))
    q_all = jnp.concatenate(rows, axis=0).astype(_BF)
    return _dot_t(q_all, kb) + bias_all


def _split_heads(o_all):
    n = DEC_SEQ
    low = _low_lanes(n)
    outs = []
    for c in range(N_HEADS // 2):
        halves = []
        for hf in range(2):
            hd = 2 * c + hf
            g = hd // GROUP
            y = o_all[hd * n:(hd + 1) * n, (g // 2) * LANES:(g // 2 + 1) * LANES]
            halves.append(y if g % 2 == hf else pltpu.roll(y, HEAD_DIM, 1))
        outs.append(jnp.where(low, halves[0], halves[1]))
    return jnp.concatenate(outs, axis=1)


def _attn_sample_kernel(q_ref, kn_ref, vn_ref, ck_ref, cv_ref, revb_ref, table_ref,
                        sinks_ref, gq_ref, o_ref, ko_ref, vo_ref, bias_ref, kbuf_ref, vbuf_ref):
    W = WINDOW
    L = 2 * W
    U = ATT_UNROLL

    @pl.when(pl.program_id(0) == 0)
    def _():
        for hd in range(N_HEADS):
            bias_ref[hd * DEC_SEQ:(hd + 1) * DEC_SEQ, :] = _bias_rows(revb_ref, table_ref, hd,
                                                                      DEC_SEQ)
        kbuf_ref[...] = jnp.zeros((U, L, KD), _F32)
        vbuf_ref[...] = jnp.zeros((U, L, KD), _F32)

    gqs = gq_ref[...] * (SCALE * LOG2E)
    bias_all = bias_ref[...]
    sink_col = jnp.concatenate(
        [jnp.full((DEC_SEQ, 1), sinks_ref[hd] * LOG2E, _F32) for hd in range(N_HEADS)], axis=0)

    def body(it, carry):
        logits = []
        for u in range(U):
            s = it * U + u
            kbuf_ref[u, 0:W, :] = ck_ref[s]
            vbuf_ref[u, 0:W, :] = cv_ref[s]
            kbuf_ref[u, W:W + DEC_SEQ, :] = kn_ref[s]
            vbuf_ref[u, W:W + DEC_SEQ, :] = vn_ref[s]
            ko_ref[s] = kbuf_ref[u, DEC_SEQ:DEC_SEQ + W, :]
            vo_ref[s] = vbuf_ref[u, DEC_SEQ:DEC_SEQ + W, :]
            logits.append(_step_logits(q_ref[s], kbuf_ref[u].astype(_BF), bias_all, gqs))
        probs = [_sink_softmax2(l, sink_col).astype(_BF) for l in logits]
        for u in range(U):
            o_all = jnp.dot(probs[u], vbuf_ref[u].astype(_BF), preferred_element_type=_F32)
            o_ref[it * U + u] = _split_heads(o_all)
        return carry

    lax.fori_loop(0, ATT_SEQS // U, body, 0)


def _attn_sample(q3d, kn3d, vn3d, ck, cv, revb, table, sinks, gq2):
    W = WINDOW
    A = ATT_SEQS
    blk = lambda r, c: pl.BlockSpec((A, r, c), lambda i: (i, 0, 0))
    smem = pl.BlockSpec(memory_space=pltpu.SMEM)
    return pl.pallas_call(
        _attn_sample_kernel,
        grid=(DEC_BATCH // A,),
        in_specs=[blk(DEC_SEQ, D_MODEL), blk(DEC_SEQ, KD), blk(DEC_SEQ, KD),
                  blk(W, KD), blk(W, KD),
                  _const_spec((1, LANES)), smem, smem, _const_spec((1, LANES))],
        out_specs=[blk(DEC_SEQ, D_MODEL), blk(W, KD), blk(W, KD)],
        out_shape=[jax.ShapeDtypeStruct((DEC_BATCH, DEC_SEQ, D_MODEL), _F32),
                   jax.ShapeDtypeStruct((DEC_BATCH, W, KD), _F32),
                   jax.ShapeDtypeStruct((DEC_BATCH, W, KD), _F32)],
        scratch_shapes=[pltpu.VMEM((N_HEADS * DEC_SEQ, 2 * W), _F32),
                        pltpu.VMEM((ATT_UNROLL, 2 * W, KD), _F32),
                        pltpu.VMEM((ATT_UNROLL, 2 * W, KD), _F32)],
        compiler_params=_params(1),
        name="attn_sample",
    )(q3d, kn3d, vn3d, ck, cv, revb, table, sinks, gq2)


def kernel(x_prompt, x_sample, state_conv, state_ffn, cache_k, cache_v, p_prompt, p_sample, g_mix, cm_w_pw1, cm_b_pw1, cm_w_dw, cm_b_dw, cm_ln_g, cm_ln_b, cm_w_pw2, cm_b_pw2, at_w_q, at_g_q, at_sinks, at_w_o, kv_g, kv_w_k, kv_w_v, kv_g_k, rel_bias, g_ffn, ffn_w_up, ffn_w_dw, ffn_b_dw, ffn_w_down, g_ple, ple_w_gate, ple_w_proj):
    D, F = D_MODEL, D_FF
    row = lambda a: a.reshape(1, -1)

    mixer_w = [row(g_mix[0]), cm_w_pw1[0], row(cm_b_pw1[0]), cm_w_dw[0], row(cm_b_dw[0]),
               row(cm_ln_g[0]), row(cm_ln_b[0]), cm_w_pw2[0], row(cm_b_pw2[0])]
    g1, conv_s_tm, w_pw1_bf, w_pw2_bf = _mixer_sample(
        x_sample.reshape(DEC_BATCH * DEC_SEQ, D), jnp.transpose(state_conv[0], (1, 0, 2)), mixer_w)
    conv_s = jnp.transpose(conv_s_tm, (1, 0, 2))
    mixer_w[1], mixer_w[7] = w_pw1_bf, w_pw2_bf
    cast_srcs = [ffn_w_up.reshape(2 * D, 2 * F), ffn_w_down.reshape(2 * F, D),
                 ple_w_gate.reshape(2 * D, D), ple_w_proj.reshape(2 * PLE_DIM, D),
                 at_w_q[0], at_w_o[0], kv_w_k, kv_w_v]

    xp = x_prompt.reshape(BATCH * SEQ, D)
    pp = p_prompt.reshape(2, BATCH * SEQ, PLE_DIM)
    (h1, conv_p, w_up_bf, w_down_bf, w_gate_bf, w_proj_bf, w_q_bf, w_o_bf, w_k_bf,
     w_v_bf) = _mixer_prompt(xp, mixer_w, cast_srcs)

    stacked = dict(
        g_ffn=g_ffn[:, None, :], w_up=w_up_bf.reshape(2, D, 2 * F), w_dw=ffn_w_dw,
        b_dw=ffn_b_dw[:, None, :], w_down=w_down_bf.reshape(2, F, D), g_ple=g_ple[:, None, :],
        w_gate=w_gate_bf.reshape(2, D, D), w_proj=w_proj_bf.reshape(2, PLE_DIM, D))
    ffn_w = [dict(stacked), dict(stacked)]
    ffn_w[0].update(g_kv=row(kv_g), w_k=w_k_bf, w_v=w_v_bf,
                    g_k2=row(jnp.tile(kv_g_k, 2)), g_q_in=row(g_mix[1]), w_q=w_q_bf)
    ffn_w[1].update(wo=w_o_bf)
    gq2 = row(jnp.tile(at_g_q[0], 2))
    sinks = at_sinks[0]

    W = WINDOW
    revb = jnp.asarray(_t5_buckets(W - 1 - np.arange(W))[None, :])
    half = np.arange(LANES) // HEAD_DIM
    seg = jnp.asarray(half[:, None] == half[None, :], dtype=_BF)

    h3, ffn_p0, k_p, v_p, q_p, kd_p, vd_p = _ffn_call(
        h1, pp, None, None, ffn_w[0], layer=0, prompt=True, with_oproj=False, with_kvq=True)
    o_p = _attn_prompt(q_p, kd_p, vd_p, revb, rel_bias, sinks, gq2, seg)
    y_p, ffn_p = _ffn_call(h3, pp, None, o_p, ffn_w[1], layer=1, prompt=True,
                           with_oproj=True, with_kvq=False, prev_state=ffn_p0)

    ps = p_sample.reshape(2, DEC_BATCH * DEC_SEQ, PLE_DIM)
    g3, ffn_s0, k_s, v_s, q_s = _ffn_call(g1, ps, state_ffn, None, ffn_w[0], layer=0,
                                          prompt=False, with_oproj=False, with_kvq=True)
    o_s, k_out, v_out = _attn_sample(
        q_s.reshape(DEC_BATCH, DEC_SEQ, D), k_s.reshape(DEC_BATCH, DEC_SEQ, KD),
        v_s.reshape(DEC_BATCH, DEC_SEQ, KD), cache_k.reshape(DEC_BATCH, W, KD),
        cache_v.reshape(DEC_BATCH, W, KD), revb, rel_bias, sinks, gq2)
    y_s, ffn_s = _ffn_call(g3, ps, state_ffn, o_s.reshape(DEC_BATCH * DEC_SEQ, D),
                           ffn_w[1], layer=1, prompt=False, with_oproj=True, with_kvq=False,
                           prev_state=ffn_s0)

    kv_shape_p = (BATCH, W, N_KV_HEADS, HEAD_DIM)
    kv_shape_s = (DEC_BATCH, W, N_KV_HEADS, HEAD_DIM)
    return (y_p.reshape(BATCH, SEQ, D), y_s.reshape(DEC_BATCH, DEC_SEQ, D),
            conv_p[None], conv_s[None],
            ffn_p, ffn_s,
            k_p.reshape(BATCH, SEQ, KD)[:, SEQ - W:].reshape(kv_shape_p),
            k_out.reshape(kv_shape_s),
            v_p.reshape(BATCH, SEQ, KD)[:, SEQ - W:].reshape(kv_shape_p),
            v_out.reshape(kv_shape_s))
```

```python
import functools
import math

import jax
import jax.numpy as jnp
import numpy as np
from jax import lax
from jax.experimental import pallas as pl
from jax.experimental.pallas import tpu as pltpu

D_MODEL = 1024
BATCH = 8
SEQ = 2048
DEC_BATCH = 128
DEC_SEQ = 8
CONV_K = 31
FFN_CONV_K = 3
D_FF = 2816
N_HEADS = 16
N_KV_HEADS = 4
HEAD_DIM = 64
GROUP = N_HEADS // N_KV_HEADS
WINDOW = 128
N_BUCKETS = 32
MAX_DISTANCE = 128
PLE_DIM = 256
EPS = 1e-6
SCALE = HEAD_DIM ** -0.5
NEG = -1e30
KD = N_KV_HEADS * HEAD_DIM

LANES = 128
SUBLANES = 8
V7X_VMEM_BYTES = 64 * 1024 * 1024
VMEM_LIMIT = V7X_VMEM_BYTES - 4 * 1024 * 1024

T_TILE = 512
MIX_TILE = 1024
S_TILE = 32
FFN_S_TILE = 32
CONV_HALO = 32
FFN_HALO = 8
F_CHUNK = 2816
D_SLABS = D_MODEL // LANES
F_SLABS = D_FF // LANES
ATT_SEQS = 16
ATT_UNROLL = 8
ATT_BLOCKS = 8

_BF = jnp.bfloat16
_F32 = jnp.float32


def _bdot(a, w):
    return jnp.dot(a.astype(_BF), w, preferred_element_type=_F32)


def _rms(x, g):
    return x * lax.rsqrt(jnp.mean(x * x, axis=-1, keepdims=True) + EPS) * g


HALF = 0.5


def _one_plus_tanh(half_x):
    return 1.0 + jnp.tanh(half_x)


def _const_spec(shape):
    nd = len(shape)
    return pl.BlockSpec(shape, lambda *_: (0,) * nd, pipeline_mode=pl.Buffered(1))


def _layer_spec(shape, layer):
    nd = len(shape)
    return pl.BlockSpec((None,) + tuple(shape), lambda *_: (layer,) + (0,) * nd,
                        pipeline_mode=pl.Buffered(1))


def _params(n_grid):
    return pltpu.CompilerParams(dimension_semantics=("arbitrary",) * n_grid,
                                vmem_limit_bytes=VMEM_LIMIT)


def _glu_slab(u, b1_ref, c):
    ba = b1_ref[:, c * LANES:(c + 1) * LANES]
    bg = b1_ref[:, D_MODEL + c * LANES:D_MODEL + (c + 1) * LANES]
    return (u[:, :LANES] + HALF * ba) * _one_plus_tanh(u[:, LANES:] + HALF * bg)


def _mixer_tail(x, c, bdw, lng, lnb, w2, b2):
    c = c + bdw
    mu = jnp.mean(c, axis=-1, keepdims=True)
    cz = c - mu
    var = jnp.mean(cz * cz, axis=-1, keepdims=True)
    hy = cz * lax.rsqrt(var + EPS) * (HALF * lng) + HALF * lnb
    s = hy * _one_plus_tanh(hy)
    return x + _bdot(s, w2) + b2


def _mixer_prompt_kernel(x_ref, g_ref, w1_ref, b1_ref, wdw_ref, bdw_ref, lng_ref, lnb_ref,
                         w2_ref, b2_ref, *rest):
    n_cast = len(_CAST_ROWS)
    cast_in, rest = rest[:n_cast], rest[n_cast:]
    h_ref, cs_ref = rest[0], rest[1]
    cast_out, (slab_ref, oslab_ref) = rest[2:2 + n_cast], rest[2 + n_cast:]
    T = MIX_TILE
    H = T // 2
    j = pl.program_id(1)

    for src, dst, scale in zip(cast_in, cast_out, _CAST_SCALE):
        dst[...] = (src[...] if scale == 1.0 else scale * src[...]).astype(_BF)

    @pl.when(j == 0)
    def _():
        slab_ref[:, 0:CONV_HALO, :] = jnp.zeros((D_SLABS, CONV_HALO, LANES), _F32)

    x = x_ref[...]
    xn = _rms(x, g_ref[...]).astype(_BF)
    base = CONV_HALO - (CONV_K - 1)
    tails = []
    for c in range(D_SLABS):
        u = jnp.dot(xn, w1_ref[:, 2 * c * LANES:2 * (c + 1) * LANES],
                    preferred_element_type=_F32)
        glu = _glu_slab(u, b1_ref, c)
        slab_ref[c, CONV_HALO:CONV_HALO + T, :] = glu
        tails.append(glu[T - (CONV_K - 1):, :])
        wc = wdw_ref[:, c * LANES:(c + 1) * LANES]
        for p in range(2):
            acc = slab_ref[c, pl.ds(base + p, H, stride=2), :] * wc[0:1, :]
            for k in range(1, CONV_K):
                acc = acc + slab_ref[c, pl.ds(base + k + p, H, stride=2), :] * wc[k:k + 1, :]
            oslab_ref[c, pl.ds(p, H, stride=2), :] = acc
    cs_ref[...] = jnp.concatenate(tails, axis=1)
    conv = jnp.concatenate([oslab_ref[c] for c in range(D_SLABS)], axis=1)
    h_ref[...] = _mixer_tail(x, conv, bdw_ref[...], lng_ref[...], lnb_ref[...],
                             w2_ref[...], b2_ref[...])
    for c in range(D_SLABS):
        slab_ref[c, 0:CONV_HALO, :] = slab_ref[c, T:T + CONV_HALO, :]


def _mixer_sample_kernel(x_ref, st_ref, g_ref, w1f_ref, b1_ref, wdw_ref, bdw_ref, lng_ref,
                         lnb_ref, w2f_ref, b2_ref, h_ref, cs_ref, w1_ref, w2_ref):
    S = S_TILE
    KH = CONV_K - 1

    @pl.when(pl.program_id(0) == 0)
    def _():
        for c in range(D_SLABS):
            for part in range(2):
                src = w1f_ref[:, part * D_MODEL + c * LANES:part * D_MODEL + (c + 1) * LANES]
                w1_ref[:, (2 * c + part) * LANES:(2 * c + part + 1) * LANES] = (
                    HALF * src).astype(_BF)
        w2_ref[...] = w2f_ref[...].astype(_BF)

    x = x_ref[...]
    u = _bdot(_rms(x, g_ref[...]), w1_ref[...])
    glu = jnp.concatenate(
        [_glu_slab(u[:, 2 * c * LANES:2 * (c + 1) * LANES], b1_ref, c) for c in range(D_SLABS)],
        axis=1)
    glu_tm = jnp.swapaxes(glu.reshape(S, DEC_SEQ, D_MODEL), 0, 1)
    cs_ref[0:KH - DEC_SEQ] = st_ref[DEC_SEQ:KH]
    cs_ref[KH - DEC_SEQ:KH] = glu_tm

    def plane(j):
        return st_ref[j] if j < KH else glu_tm[j - KH]

    outs = []
    for t in range(DEC_SEQ):
        acc = plane(t) * wdw_ref[0:1, :]
        for k in range(1, CONV_K):
            acc = acc + plane(t + k) * wdw_ref[k:k + 1, :]
        outs.append(acc)
    conv = jnp.swapaxes(jnp.stack(outs, axis=0), 0, 1).reshape(S * DEC_SEQ, D_MODEL)
    h_ref[...] = _mixer_tail(x, conv, bdw_ref[...], lng_ref[...], lnb_ref[...],
                             w2_ref[...], b2_ref[...])


def _mixer_weight_specs():
    D = D_MODEL
    return [_const_spec((1, D)), _const_spec((D, 2 * D)), _const_spec((1, 2 * D)),
            _const_spec((CONV_K, D)), _const_spec((1, D)), _const_spec((1, D)),
            _const_spec((1, D)), _const_spec((D, D)), _const_spec((1, D))]


_CAST_ROWS = (2 * D_MODEL, 2 * D_FF, 2 * D_MODEL, 2 * PLE_DIM) + (D_MODEL,) * 4
_CAST_SCALE = (1.0, 1.0, HALF, HALF) + (1.0,) * 4


def _mixer_prompt(x2d, wts, cast_srcs):
    T = MIX_TILE
    nt = SEQ // T
    steps = BATCH * nt
    row = lambda b, j: (b * nt + j, 0)
    cast_specs, cast_shapes = [], []
    for src, rows in zip(cast_srcs, _CAST_ROWS):
        assert src.shape[0] == rows and rows % steps == 0
        cast_specs.append(pl.BlockSpec((rows // steps, src.shape[1]), row))
        cast_shapes.append(jax.ShapeDtypeStruct(src.shape, _BF))
    return pl.pallas_call(
        _mixer_prompt_kernel,
        grid=(BATCH, nt),
        in_specs=[pl.BlockSpec((T, D_MODEL), row)] + _mixer_weight_specs() + cast_specs,
        out_specs=[pl.BlockSpec((T, D_MODEL), row),
                   pl.BlockSpec((None, CONV_K - 1, D_MODEL), lambda b, j: (b, 0, 0))] + cast_specs,
        out_shape=[jax.ShapeDtypeStruct((BATCH * SEQ, D_MODEL), _F32),
                   jax.ShapeDtypeStruct((BATCH, CONV_K - 1, D_MODEL), _F32)] + cast_shapes,
        scratch_shapes=[pltpu.VMEM((D_SLABS, CONV_HALO + T, LANES), _F32),
                        pltpu.VMEM((D_SLABS, T, LANES), _F32)],
        compiler_params=_params(2),
        name="mixer_prompt",
    )(x2d, *wts, *cast_srcs)


def _mixer_sample(x2d, state_tm, wts):
    S = S_TILE
    R = S * DEC_SEQ
    rows = pl.BlockSpec((R, D_MODEL), lambda i: (i, 0))
    planes = pl.BlockSpec((CONV_K - 1, S, D_MODEL), lambda i: (0, i, 0))
    w1_shape, w2_shape = (D_MODEL, 2 * D_MODEL), (D_MODEL, D_MODEL)
    whole = lambda shape: pl.BlockSpec(shape, lambda i: (0, 0))
    return pl.pallas_call(
        _mixer_sample_kernel,
        grid=(DEC_BATCH // S,),
        in_specs=[rows, planes] + _mixer_weight_specs(),
        out_specs=[rows, planes, whole(w1_shape), whole(w2_shape)],
        out_shape=[jax.ShapeDtypeStruct((DEC_BATCH * DEC_SEQ, D_MODEL), _F32),
                   jax.ShapeDtypeStruct((CONV_K - 1, DEC_BATCH, D_MODEL), _F32),
                   jax.ShapeDtypeStruct(w1_shape, _BF), jax.ShapeDtypeStruct(w2_shape, _BF)],
        compiler_params=_params(1),
        name="mixer_sample",
    )(x2d, state_tm, *wts)


def _ffn_kernel(*refs, prompt, with_oproj, with_kvq):
    it = iter(refs)
    h_ref = next(it)
    p_ref = next(it)
    st_ref = None if prompt else next(it)
    if with_oproj:
        o_ref = next(it)
        wo_ref = next(it)
        fs0_ref = next(it)
    gf_ref, wup_ref, wdw_ref, bdw_ref, wdn_ref = (next(it) for _ in range(5))
    gp_ref, wg_ref, wp_ref = (next(it) for _ in range(3))
    if with_kvq:
        gkv_ref, wk_ref, wv_ref, gk_ref, gq_ref, wq_ref = (next(it) for _ in range(6))
    hout_ref = next(it)
    fs_ref = next(it)
    dup_kv = with_kvq and prompt
    if with_kvq:
        k_ref, v_ref, q_ref = (next(it) for _ in range(3))
    if dup_kv:
        kd_ref, vd_ref = next(it), next(it)
    if prompt:
        slab_ref, oslab_ref = next(it), next(it)

    F = D_FF
    R = T_TILE if prompt else FFN_S_TILE * DEC_SEQ
    H = R // 2
    KH = FFN_CONV_K - 1
    assert F_CHUNK == F or prompt

    if prompt:
        @pl.when(pl.program_id(1) == 0)
        def _():
            slab_ref[:, 0:FFN_HALO, :] = jnp.zeros((2 * F_SLABS, FFN_HALO, LANES), _F32)

    h = h_ref[...]
    if with_oproj:
        h = h + _bdot(o_ref[...], wo_ref[...])
    xn = _rms(h, gf_ref[...]).astype(_BF)

    n_chunks = F // F_CHUNK
    spc = F_CHUNK // LANES
    acc = None
    tails = []
    for ci in range(n_chunks):
        halves = []
        for half in range(2):
            c0 = half * F + ci * F_CHUNK
            uc = jnp.dot(xn, wup_ref[:, c0:c0 + F_CHUNK], preferred_element_type=_F32)
            w = wdw_ref[:, c0:c0 + F_CHUNK]
            b = bdw_ref[:, c0:c0 + F_CHUNK]
            if half == 0:
                w, b = HALF * w, HALF * b
            if prompt:
                cols = []
                for s in range(spc):
                    sl = half * F_SLABS + ci * spc + s
                    slab_ref[sl, FFN_HALO:FFN_HALO + R, :] = uc[:, s * LANES:(s + 1) * LANES]
                    ws = w[:, s * LANES:(s + 1) * LANES]
                    ph = []
                    for p in range(2):
                        cv = b[:, s * LANES:(s + 1) * LANES]
                        for k in range(FFN_CONV_K):
                            cv = cv + (slab_ref[sl, pl.ds(FFN_HALO - KH + k + p, H, stride=2), :]
                                       * ws[k:k + 1, :])
                        ph.append(cv)
                    cols.append(jnp.concatenate(ph, axis=0))
                halves.append(jnp.concatenate(cols, axis=1))
            else:
                u3 = uc.reshape(FFN_S_TILE, DEC_SEQ, F_CHUNK)
                hist = st_ref[:, :, c0:c0 + F_CHUNK]
                h0 = jnp.broadcast_to(hist[:, 0:1, :], u3.shape)
                h1 = jnp.broadcast_to(hist[:, 1:2, :], u3.shape)
                step = lax.broadcasted_iota(jnp.int32, u3.shape, 1)
                um1 = jnp.where(step == 0, h1, pltpu.roll(u3, 1, 1))
                um2 = jnp.where(step == 0, h0, jnp.where(step == 1, h1, pltpu.roll(u3, 2, 1)))
                cv = b + um2 * w[0:1, :] + um1 * w[1:2, :] + u3 * w[2:3, :]
                tails.append(u3[:, DEC_SEQ - KH:, :])
                halves.append(cv.reshape(R, F_CHUNK))
        half_gate, val = halves
        act = (half_gate * _one_plus_tanh(half_gate) * val).astype(_BF)
        part = jnp.dot(act, wdn_ref[ci * F_CHUNK:(ci + 1) * F_CHUNK, :],
                       preferred_element_type=_F32)
        acc = part if acc is None else acc + part

    if prompt:
        for c in range(D_SLABS):
            oslab_ref[c, pl.ds(0, H, stride=2), :] = acc[:H, c * LANES:(c + 1) * LANES]
            oslab_ref[c, pl.ds(1, H, stride=2), :] = acc[H:, c * LANES:(c + 1) * LANES]
        f = jnp.concatenate([oslab_ref[c] for c in range(D_SLABS)], axis=1)
        new_state = jnp.concatenate(
            [slab_ref[s, FFN_HALO + R - SUBLANES:FFN_HALO + R, :] for s in range(2 * F_SLABS)],
            axis=1)[SUBLANES - KH:, :]
        for s in range(2 * F_SLABS):
            slab_ref[s, 0:FFN_HALO, :] = slab_ref[s, R:R + FFN_HALO, :]
    else:
        f = acc
        new_state = jnp.concatenate(tails, axis=2)
    if with_oproj:
        fs_ref[0] = fs0_ref[...]
        fs_ref[1] = new_state
    else:
        fs_ref[...] = new_state

    h2 = h + f
    h3 = h2 + (_one_plus_tanh(_bdot(_rms(h2, gp_ref[...]), wg_ref[...]))
               * _bdot(p_ref[...], wp_ref[...]))
    hout_ref[...] = h3

    if with_kvq:
        hk = _rms(h3, gkv_ref[...]).astype(_BF)
        kk = jnp.dot(hk, wk_ref[...], preferred_element_type=_F32)
        vv = jnp.dot(hk, wv_ref[...], preferred_element_type=_F32)
        v_ref[...] = vv
        low = lax.broadcasted_iota(jnp.int32, (R, LANES), 1) < HEAD_DIM

        def dup_heads(x):
            out = []
            for c in range(KD // LANES):
                xc = x[:, c * LANES:(c + 1) * LANES]
                xr = pltpu.roll(xc, HEAD_DIM, 1)
                out += [jnp.where(low, xc, xr), jnp.where(low, xr, xc)]
            return out

        kn = []
        for kg in dup_heads(kk):
            r = lax.rsqrt(jnp.mean(kg * kg, axis=-1, keepdims=True) + EPS)
            kn.append(kg * r * gk_ref[...])
        k_ref[...] = jnp.concatenate(
            [jnp.where(low, kn[2 * c], kn[2 * c + 1]) for c in range(KD // LANES)], axis=1)
        if dup_kv:
            kd_ref[...] = jnp.concatenate(kn, axis=1).astype(_BF)
            vd_ref[...] = jnp.concatenate(dup_heads(vv), axis=1).astype(_BF)
        q_ref[...] = _bdot(_rms(h3, gq_ref[...]), wq_ref[...])


def _ffn_call(h2d, p3d, state4d, o2d, w, *, layer, prompt, with_oproj, with_kvq,
              prev_state=None):
    D, F = D_MODEL, D_FF
    if prompt:
        R = T_TILE
        nt = SEQ // R
        grid = (BATCH, nt)
        row = lambda b, j: (b * nt + j, 0)
        p_spec = pl.BlockSpec((None, R, PLE_DIM), lambda b, j: (layer, b * nt + j, 0))
        n_rows, n_seq = BATCH * SEQ, BATCH
        fs_spec = pl.BlockSpec((None, FFN_CONV_K - 1, 2 * F), lambda b, j: (b, 0, 0))
        scratch = [pltpu.VMEM((2 * F_SLABS, FFN_HALO + R, LANES), _F32),
                   pltpu.VMEM((D_SLABS, R, LANES), _F32)]
    else:
        R = FFN_S_TILE * DEC_SEQ
        grid = (DEC_BATCH // FFN_S_TILE,)
        row = lambda i: (i, 0)
        p_spec = pl.BlockSpec((None, R, PLE_DIM), lambda i: (layer, i, 0))
        n_rows, n_seq = DEC_BATCH * DEC_SEQ, DEC_BATCH
        fs_spec = pl.BlockSpec((FFN_S_TILE, FFN_CONV_K - 1, 2 * F), lambda i: (i, 0, 0))
        scratch = []

    args = [h2d, p3d]
    in_specs = [pl.BlockSpec((R, D), row), p_spec]
    if not prompt:
        args.append(state4d)
        in_specs.append(pl.BlockSpec((None, FFN_S_TILE, FFN_CONV_K - 1, 2 * F),
                                     lambda i: (layer, i, 0, 0)))
    fs_shape = (n_seq, FFN_CONV_K - 1, 2 * F)
    if with_oproj:
        args += [o2d, w["wo"], prev_state]
        in_specs += [pl.BlockSpec((R, D), row), _const_spec((D, D)), fs_spec]
        blk, imap = fs_spec.block_shape, fs_spec.index_map
        fs_spec = pl.BlockSpec((2,) + tuple(blk), lambda *g: (0,) + tuple(imap(*g)))
        fs_shape = (2,) + fs_shape
    args += [w["g_ffn"], w["w_up"], w["w_dw"], w["b_dw"], w["w_down"],
             w["g_ple"], w["w_gate"], w["w_proj"]]
    in_specs += [_layer_spec((1, D), layer), _layer_spec((D, 2 * F), layer),
                 _layer_spec((FFN_CONV_K, 2 * F), layer), _layer_spec((1, 2 * F), layer),
                 _layer_spec((F, D), layer), _layer_spec((1, D), layer),
                 _layer_spec((D, D), layer), _layer_spec((PLE_DIM, D), layer)]
    out_specs = [pl.BlockSpec((R, D), row), fs_spec]
    out_shape = [jax.ShapeDtypeStruct((n_rows, D), _F32),
                 jax.ShapeDtypeStruct(fs_shape, _F32)]
    if with_kvq:
        args += [w["g_kv"], w["w_k"], w["w_v"], w["g_k2"], w["g_q_in"], w["w_q"]]
        in_specs += [_const_spec((1, D)), _const_spec((D, KD)), _const_spec((D, KD)),
                     _const_spec((1, LANES)), _const_spec((1, D)), _const_spec((D, D))]
        out_specs += [pl.BlockSpec((R, KD), row), pl.BlockSpec((R, KD), row),
                      pl.BlockSpec((R, D), row)]
        out_shape += [jax.ShapeDtypeStruct((n_rows, KD), _F32),
                      jax.ShapeDtypeStruct((n_rows, KD), _F32),
                      jax.ShapeDtypeStruct((n_rows, D), _F32)]
        if prompt:
            out_specs += [pl.BlockSpec((R, 2 * KD), row), pl.BlockSpec((R, 2 * KD), row)]
            out_shape += [jax.ShapeDtypeStruct((n_rows, 2 * KD), _BF),
                          jax.ShapeDtypeStruct((n_rows, 2 * KD), _BF)]
    kern = functools.partial(_ffn_kernel, prompt=prompt, with_oproj=with_oproj,
                             with_kvq=with_kvq)
    name = "ffn_%s_%d" % ("prompt" if prompt else "sample", layer)
    return pl.pallas_call(
        kern, grid=grid, in_specs=in_specs, out_specs=out_specs, out_shape=out_shape,
        scratch_shapes=scratch, compiler_params=_params(len(grid)), name=name,
    )(*args)


def _t5_buckets(dist):
    max_exact = N_BUCKETS // 2
    d = np.maximum(dist, 0)
    df = np.maximum(d, 1).astype(np.float32)
    large = max_exact + (np.log(df / np.float32(max_exact))
                         / np.float32(math.log(MAX_DISTANCE / max_exact))
                         * np.float32(N_BUCKETS - max_exact)).astype(np.int32)
    large = np.minimum(large, N_BUCKETS - 1)
    return np.where(d < max_exact, d, large).astype(np.int32)


LOG2E = math.log2(math.e)
NEG2 = NEG * LOG2E


def _low_lanes(rows):
    return lax.broadcasted_iota(jnp.int32, (rows, LANES), 1) < HEAD_DIM


def _bias_rows(revb_ref, table_ref, hd, nq):
    revb = revb_ref[...]
    row = jnp.zeros(revb.shape, _F32)
    for bk in range(N_BUCKETS):
        row = jnp.where(revb == bk, table_ref[bk, hd] * LOG2E, row)
    base = jnp.concatenate([jnp.broadcast_to(row, (nq, LANES)),
                            jnp.full((nq, LANES), NEG2, _F32)], axis=1)
    return pltpu.roll(base, 1, 1, stride=1, stride_axis=0)


def _norm_heads(q, seg, gqs):
    n = q.shape[0]
    nc = D_MODEL // LANES
    sq = jnp.concatenate([q[:, c * LANES:(c + 1) * LANES] for c in range(nc)], axis=0)
    ss = jnp.dot((sq * sq).astype(_BF), seg, preferred_element_type=_F32)
    r = lax.rsqrt(ss * (1.0 / HEAD_DIM) + EPS)
    return [q[:, c * LANES:(c + 1) * LANES] * r[c * n:(c + 1) * n, :] * gqs for c in range(nc)]


def _sink_softmax2(logits2, sink2):
    m = jnp.maximum(jnp.max(logits2, axis=-1, keepdims=True), sink2)
    e = jnp.exp2(logits2 - m)
    denom = jnp.sum(e, axis=-1, keepdims=True) + jnp.exp2(sink2 - m)
    return e * (1.0 / denom)


def _dot_t(a, b):
    return lax.dot_general(a, b, (((1,), (1,)), ((), ())), preferred_element_type=_F32)


def _attend_blocks(blocks, sinks_ref, seg, gqs):
    units = []
    for (q, kd, vd, bias_of) in blocks:
        nq = q.shape[0]
        qn = _norm_heads(q, seg, gqs)
        units += [(g, nq, bias_of, qn, kd[:, g * LANES:(g + 1) * LANES],
                   vd[:, g * LANES:(g + 1) * LANES]) for g in range(N_KV_HEADS)]

    def scores(unit):
        g, nq, _, qn, kg, _ = unit
        low = _low_lanes(nq)
        qs = jnp.concatenate(
            [jnp.where(low if hl % 2 == 0 else jnp.logical_not(low), qn[2 * g + hl // 2], 0.0)
             for hl in range(GROUP)], axis=0).astype(_BF)
        return _dot_t(qs, kg)

    def softmax(unit, s):
        g, nq, bias_of = unit[:3]
        return jnp.concatenate(
            [_sink_softmax2(s[hl * nq:(hl + 1) * nq, :] + bias_of(g * GROUP + hl),
                            sinks_ref[g * GROUP + hl] * LOG2E) for hl in range(GROUP)],
            axis=0).astype(_BF)

    def values(unit, p):
        nq, vg = unit[1], unit[5]
        low_k = _low_lanes(vg.shape[0])
        zero = jnp.zeros_like(vg)
        vlo, vhi = jnp.where(low_k, vg, zero), jnp.where(low_k, zero, vg)
        outs = []
        for pair in range(2):
            r0 = 2 * pair * nq
            outs.append(jnp.dot(p[r0:r0 + nq], vlo, preferred_element_type=_F32)
                        + jnp.dot(p[r0 + nq:r0 + 2 * nq], vhi, preferred_element_type=_F32))
        return outs

    n = len(units)
    s_next = scores(units[0])
    probs, outs = [None] * n, [None] * n
    for i in range(n):
        s_cur, s_next = s_next, (scores(units[i + 1]) if i + 1 < n else None)
        probs[i] = softmax(units[i], s_cur)
        if i >= 1:
            outs[i - 1] = values(units[i - 1], probs[i - 1])
    outs[n - 1] = values(units[n - 1], probs[n - 1])
    return [jnp.concatenate(sum(outs[b * N_KV_HEADS:(b + 1) * N_KV_HEADS], []), axis=1)
            for b in range(len(blocks))]


def _attn_prompt_kernel(q_ref, kp_ref, kc_ref, vp_ref, vc_ref, revb_ref, table_ref,
                        sinks_ref, gq_ref, seg_ref, o_ref, bias_ref):
    W = WINDOW
    i = pl.program_id(1)

    @pl.when(jnp.logical_and(pl.program_id(0) == 0, i == 0))
    def _():
        first_half = lax.broadcasted_iota(jnp.int32, (W, 2 * W), 1) < W
        for hd in range(N_HEADS):
            rows = _bias_rows(revb_ref, table_ref, hd, W)
            bias_ref[0, hd] = rows
            bias_ref[1, hd] = jnp.where(first_half, NEG2, rows)

    gqs = gq_ref[...] * (SCALE * LOG2E)
    kp, kc = kp_ref[...], kc_ref[...]
    vp, vc = vp_ref[...], vc_ref[...]
    first = jnp.where(i == 0, 1, 0)
    blocks = []
    for blk in range(ATT_BLOCKS):
        if blk == 0:
            kd = jnp.concatenate([kp, kc[0:W]], axis=0)
            vd = jnp.concatenate([vp, vc[0:W]], axis=0)
            bias_of = lambda hd: bias_ref[first, hd]
        else:
            kd, vd = kc[(blk - 1) * W:(blk + 1) * W], vc[(blk - 1) * W:(blk + 1) * W]
            bias_of = lambda hd: bias_ref[0, hd]
        blocks.append((q_ref[blk * W:(blk + 1) * W, :], kd, vd, bias_of))
    for blk, o in enumerate(_attend_blocks(blocks, sinks_ref, seg_ref[...], gqs)):
        o_ref[blk * W:(blk + 1) * W, :] = o.astype(o_ref.dtype)


def _attn_prompt(q2d, kd2d, vd2d, revb, table, sinks, gq2, seg):
    W = WINDOW
    R = ATT_BLOCKS * W
    ns = SEQ // R
    cur = lambda b, i: (b * ns + i, 0)
    prev = lambda b, i: (b * (SEQ // W) + jnp.maximum(ATT_BLOCKS * i - 1, 0), 0)
    smem = pl.BlockSpec(memory_space=pltpu.SMEM)
    return pl.pallas_call(
        _attn_prompt_kernel,
        grid=(BATCH, ns),
        in_specs=[pl.BlockSpec((R, D_MODEL), cur),
                  pl.BlockSpec((W, 2 * KD), prev), pl.BlockSpec((R, 2 * KD), cur),
                  pl.BlockSpec((W, 2 * KD), prev), pl.BlockSpec((R, 2 * KD), cur),
                  _const_spec((1, LANES)), smem, smem, _const_spec((1, LANES)),
                  _const_spec((LANES, LANES))],
        out_specs=pl.BlockSpec((R, D_MODEL), cur),
        out_shape=jax.ShapeDtypeStruct((BATCH * SEQ, D_MODEL), _BF),
        scratch_shapes=[pltpu.VMEM((2, N_HEADS, W, 2 * W), _F32)],
        compiler_params=_params(2),
        name="attn_prompt",
    )(q2d, kd2d, kd2d, vd2d, vd2d, revb, table, sinks, gq2, seg)


def _step_logits(q, kb, bias_all, gqs):
    n = DEC_SEQ
    low = _low_lanes(n)
    zero = jnp.zeros((n, LANES), _F32)
    rows = []
    for hd in range(N_HEADS):
        g = hd // GROUP
        x = q[:, (hd // 2) * LANES:(hd // 2 + 1) * LANES]
        if hd % 2 != g % 2:
            x = pltpu.roll(x, HEAD_DIM, 1)
        x = jnp.where(low if g % 2 == 0 else jnp.logical_not(low), x, 0.0)
        ss = jnp.sum(x * x, axis=-1, keepdims=True) * (1.0 / HEAD_DIM)
        x = x * lax.rsqrt(ss + EPS) * gqs
        rows.append(jnp.concatenate([x, zero] if g // 2 == 0 else [zero, x], axis=1))
    q_all = jnp.concatenate(rows, axis=0).astype(_BF)
    return _dot_t(q_all, kb) + bias_all


def _split_heads(o_all):
    n = DEC_SEQ
    low = _low_lanes(n)
    outs = []
    for c in range(N_HEADS // 2):
        halves = []
        for hf in range(2):
            hd = 2 * c + hf
            g = hd // GROUP
            y = o_all[hd * n:(hd + 1) * n, (g // 2) * LANES:(g // 2 + 1) * LANES]
            halves.append(y if g % 2 == hf else pltpu.roll(y, HEAD_DIM, 1))
        outs.append(jnp.where(low, halves[0], halves[1]))
    return jnp.concatenate(outs, axis=1)


def _attn_sample_kernel(q_ref, kn_ref, vn_ref, ck_ref, cv_ref, revb_ref, table_ref,
                        sinks_ref, gq_ref, o_ref, ko_ref, vo_ref, bias_ref, kbuf_ref, vbuf_ref):
    W = WINDOW
    L = 2 * W
    U = ATT_UNROLL

    @pl.when(pl.program_id(0) == 0)
    def _():
        for hd in range(N_HEADS):
            bias_ref[hd * DEC_SEQ:(hd + 1) * DEC_SEQ, :] = _bias_rows(revb_ref, table_ref, hd,
                                                                      DEC_SEQ)
        kbuf_ref[...] = jnp.zeros((U, L, KD), _F32)
        vbuf_ref[...] = jnp.zeros((U, L, KD), _F32)

    gqs = gq_ref[...] * (SCALE * LOG2E)
    bias_all = bias_ref[...]
    sink_col = jnp.concatenate(
        [jnp.full((DEC_SEQ, 1), sinks_ref[hd] * LOG2E, _F32) for hd in range(N_HEADS)], axis=0)

    def body(it, carry):
        logits = []
        for u in range(U):
            s = it * U + u
            kbuf_ref[u, 0:W, :] = ck_ref[s]
            vbuf_ref[u, 0:W, :] = cv_ref[s]
            kbuf_ref[u, W:W + DEC_SEQ, :] = kn_ref[s]
            vbuf_ref[u, W:W + DEC_SEQ, :] = vn_ref[s]
            ko_ref[s] = kbuf_ref[u, DEC_SEQ:DEC_SEQ + W, :]
            vo_ref[s] = vbuf_ref[u, DEC_SEQ:DEC_SEQ + W, :]
            logits.append(_step_logits(q_ref[s], kbuf_ref[u].astype(_BF), bias_all, gqs))
        probs = [_sink_softmax2(l, sink_col).astype(_BF) for l in logits]
        for u in range(U):
            o_all = jnp.dot(probs[u], vbuf_ref[u].astype(_BF), preferred_element_type=_F32)
            o_ref[it * U + u] = _split_heads(o_all)
        return carry

    lax.fori_loop(0, ATT_SEQS // U, body, 0)


def _attn_sample(q3d, kn3d, vn3d, ck, cv, revb, table, sinks, gq2):
    W = WINDOW
    A = ATT_SEQS
    blk = lambda r, c: pl.BlockSpec((A, r, c), lambda i: (i, 0, 0))
    smem = pl.BlockSpec(memory_space=pltpu.SMEM)
    return pl.pallas_call(
        _attn_sample_kernel,
        grid=(DEC_BATCH // A,),
        in_specs=[blk(DEC_SEQ, D_MODEL), blk(DEC_SEQ, KD), blk(DEC_SEQ, KD),
                  blk(W, KD), blk(W, KD),
                  _const_spec((1, LANES)), smem, smem, _const_spec((1, LANES))],
        out_specs=[blk(DEC_SEQ, D_MODEL), blk(W, KD), blk(W, KD)],
        out_shape=[jax.ShapeDtypeStruct((DEC_BATCH, DEC_SEQ, D_MODEL), _F32),
                   jax.ShapeDtypeStruct((DEC_BATCH, W, KD), _F32),
                   jax.ShapeDtypeStruct((DEC_BATCH, W, KD), _F32)],
        scratch_shapes=[pltpu.VMEM((N_HEADS * DEC_SEQ, 2 * W), _F32),
                        pltpu.VMEM((ATT_UNROLL, 2 * W, KD), _F32),
                        pltpu.VMEM((ATT_UNROLL, 2 * W, KD), _F32)],
        compiler_params=_params(1),
        name="attn_sample",
    )(q3d, kn3d, vn3d, ck, cv, revb, table, sinks, gq2)


def kernel(x_prompt, x_sample, state_conv, state_ffn, cache_k, cache_v, p_prompt, p_sample, g_mix, cm_w_pw1, cm_b_pw1, cm_w_dw, cm_b_dw, cm_ln_g, cm_ln_b, cm_w_pw2, cm_b_pw2, at_w_q, at_g_q, at_sinks, at_w_o, kv_g, kv_w_k, kv_w_v, kv_g_k, rel_bias, g_ffn, ffn_w_up, ffn_w_dw, ffn_b_dw, ffn_w_down, g_ple, ple_w_gate, ple_w_proj):
    D, F = D_MODEL, D_FF
    row = lambda a: a.reshape(1, -1)

    mixer_w = [row(g_mix[0]), cm_w_pw1[0], row(cm_b_pw1[0]), cm_w_dw[0], row(cm_b_dw[0]),
               row(cm_ln_g[0]), row(cm_ln_b[0]), cm_w_pw2[0], row(cm_b_pw2[0])]
    g1, conv_s_tm, w_pw1_bf, w_pw2_bf = _mixer_sample(
        x_sample.reshape(DEC_BATCH * DEC_SEQ, D), jnp.transpose(state_conv[0], (1, 0, 2)), mixer_w)
    conv_s = jnp.transpose(conv_s_tm, (1, 0, 2))
    mixer_w[1], mixer_w[7] = w_pw1_bf, w_pw2_bf
    cast_srcs = [ffn_w_up.reshape(2 * D, 2 * F), ffn_w_down.reshape(2 * F, D),
                 ple_w_gate.reshape(2 * D, D), ple_w_proj.reshape(2 * PLE_DIM, D),
                 at_w_q[0], at_w_o[0], kv_w_k, kv_w_v]

    xp = x_prompt.reshape(BATCH * SEQ, D)
    pp = p_prompt.reshape(2, BATCH * SEQ, PLE_DIM)
    (h1, conv_p, w_up_bf, w_down_bf, w_gate_bf, w_proj_bf, w_q_bf, w_o_bf, w_k_bf,
     w_v_bf) = _mixer_prompt(xp, mixer_w, cast_srcs)

    stacked = dict(
        g_ffn=g_ffn[:, None, :], w_up=w_up_bf.reshape(2, D, 2 * F), w_dw=ffn_w_dw,
        b_dw=ffn_b_dw[:, None, :], w_down=w_down_bf.reshape(2, F, D), g_ple=g_ple[:, None, :],
        w_gate=w_gate_bf.reshape(2, D, D), w_proj=w_proj_bf.reshape(2, PLE_DIM, D))
    ffn_w = [dict(stacked), dict(stacked)]
    ffn_w[0].update(g_kv=row(kv_g), w_k=w_k_bf, w_v=w_v_bf,
                    g_k2=row(jnp.tile(kv_g_k, 2)), g_q_in=row(g_mix[1]), w_q=w_q_bf)
    ffn_w[1].update(wo=w_o_bf)
    gq2 = row(jnp.tile(at_g_q[0], 2))
    sinks = at_sinks[0]

    W = WINDOW
    revb = jnp.asarray(_t5_buckets(W - 1 - np.arange(W))[None, :])
    half = np.arange(LANES) // HEAD_DIM
    seg = jnp.asarray(half[:, None] == half[None, :], dtype=_BF)

    h3, ffn_p0, k_p, v_p, q_p, kd_p, vd_p = _ffn_call(
        h1, pp, None, None, ffn_w[0], layer=0, prompt=True, with_oproj=False, with_kvq=True)
    o_p = _attn_prompt(q_p, kd_p, vd_p, revb, rel_bias, sinks, gq2, seg)
    y_p, ffn_p = _ffn_call(h3, pp, None, o_p, ffn_w[1], layer=1, prompt=True,
                           with_oproj=True, with_kvq=False, prev_state=ffn_p0)

    ps = p_sample.reshape(2, DEC_BATCH * DEC_SEQ, PLE_DIM)
    g3, ffn_s0, k_s, v_s, q_s = _ffn_call(g1, ps, state_ffn, None, ffn_w[0], layer=0,
                                          prompt=False, with_oproj=False, with_kvq=True)
    o_s, k_out, v_out = _attn_sample(
        q_s.reshape(DEC_BATCH, DEC_SEQ, D), k_s.reshape(DEC_BATCH, DEC_SEQ, KD),
        v_s.reshape(DEC_BATCH, DEC_SEQ, KD), cache_k.reshape(DEC_BATCH, W, KD),
        cache_v.reshape(DEC_BATCH, W, KD), revb, rel_bias, sinks, gq2)
    y_s, ffn_s = _ffn_call(g3, ps, state_ffn, o_s.reshape(DEC_BATCH * DEC_SEQ, D),
                           ffn_w[1], layer=1, prompt=False, with_oproj=True, with_kvq=False,
                           prev_state=ffn_s0)

    kv_shape_p = (BATCH, W, N_KV_HEADS, HEAD_DIM)
    kv_shape_s = (DEC_BATCH, W, N_KV_HEADS, HEAD_DIM)
    return (y_p.reshape(BATCH, SEQ, D), y_s.reshape(DEC_BATCH, DEC_SEQ, D),
            conv_p[None], conv_s[None],
            ffn_p, ffn_s,
            k_p.reshape(BATCH, SEQ, KD)[:, SEQ - W:].reshape(kv_shape_p),
            k_out.reshape(kv_shape_s),
            v_p.reshape(BATCH, SEQ, KD)[:, SEQ - W:].reshape(kv_shape_p),
            v_out.reshape(kv_shape_s))
```
